```python
import math
import jax
import jax.numpy as jnp
from jax import lax
import numpy as np

D_MODEL = 2048
BATCH = 4
SEQ = 2048
DEPTH = 2
DEC_BATCH = 128
DEC_SEQ = 8
PAST_LEN = 16384
PAGE_SIZE = 128

MIX = D_MODEL
GROUP = MIX // 4
RW_HD = 64
RW_H = GROUP // RW_HD
RW_W_LORA = 64
RW_A_LORA = 64
RW_G_LORA = 128
RW_LN_EPS = 64e-5
RW_IN = 3 * GROUP + RW_W_LORA + RW_A_LORA + RW_G_LORA
MB_HD = 64
MB_H = GROUP // MB_HD
MB_N = 128
MB_G = 2
MB_CONV = 4
MB_CHUNK = 128
MB_CONV_DIM = GROUP + 2 * MB_G * MB_N
MB_IN = GROUP + MB_CONV_DIM + MB_H
GLA_H = 4
GLA_DK = GROUP // 2 // GLA_H
GLA_DV = GROUP // GLA_H
GLA_LORA = 16
GLA_TAU = 16.0
GLA_CHUNK = 16
GLA_QK = GLA_H * GLA_DK
GLA_IN = 2 * GLA_QK + GROUP + GLA_LORA + GROUP
RET_H = 4
RET_HD = GROUP // RET_H
RET_CHUNK = 128
ROPE_BASE = 10000.0
RET_IN = 4 * GROUP
N_IN = RW_IN + MB_IN + GLA_IN + RET_IN
N_MEM = 256
XA_H = 4
XA_HD = D_MODEL // XA_H
D_FF = -(-8 * D_MODEL // (3 * 256)) * 256
DN_ALPHA = (2 * DEPTH) ** 0.25
DN_BETA = (8 * DEPTH) ** -0.25

RW_SPLITS = (GROUP, 2 * GROUP, 3 * GROUP, 3 * GROUP + RW_W_LORA, 3 * GROUP + RW_W_LORA + RW_A_LORA)
MB_SPLITS = (GROUP, GROUP + MB_CONV_DIM)
GLA_SPLITS = (GLA_QK, 2 * GLA_QK, 2 * GLA_QK + GROUP, 2 * GLA_QK + GROUP + GLA_LORA)
IN_SPLITS = (RW_IN, RW_IN + MB_IN, RW_IN + MB_IN + GLA_IN)

kernel_name = 'hybrid_rwkv7_ssd_gla_retnet_decoder_step'


def _layer_norm(x, g, b, eps=1e-5):
    xf = x.astype(jnp.float32)
    xc = xf - jnp.mean(xf, -1, keepdims=True)
    var = jnp.mean(xc * xc, -1, keepdims=True)
    return (xc * lax.rsqrt(var + eps)).astype(x.dtype) * g + b


def _head_layer_norm(x, g, b, eps):
    xc = x - jnp.mean(x, -1, keepdims=True)
    return xc * lax.rsqrt(jnp.mean(xc * xc, -1, keepdims=True) + eps) * g + b


def _rms_norm(x, g, eps=1e-6):
    return x * lax.rsqrt(jnp.mean(x * x, -1, keepdims=True) + eps) * g


def _causal_mask(c):
    return jnp.tril(jnp.ones((c, c), dtype=bool))


def _chunked_scalar_decay(q, k, v, log_a, h0, chunk):
    bsz, t, h, dk = q.shape
    dv = v.shape[-1]
    c = math.gcd(t, chunk)
    n = t // c
    q = q.reshape(bsz, n, c, h, dk)
    k = k.reshape(bsz, n, c, h, dk)
    v = v.reshape(bsz, n, c, h, dv)
    cum = jnp.cumsum(log_a.reshape(bsz, n, c, h), axis=2)
    mask = _causal_mask(c)[None, None, :, :, None]
    seg = cum[:, :, :, None, :] - cum[:, :, None, :, :]
    decay = jnp.exp(jnp.where(mask, seg, -jnp.inf))
    scores = jnp.einsum('bnihk,bnjhk->bnijh', q, k) * decay
    y_intra = jnp.einsum('bnijh,bnjhv->bnihv', scores, v)
    dec_end = jnp.exp(cum[:, :, -1:, :] - cum)
    chunk_state = jnp.einsum('bnjhk,bnjh,bnjhv->bnhkv', k, dec_end, v)
    chunk_decay = jnp.exp(cum[:, :, -1, :])

    def step(hs, inp):
        cs, cd = inp
        return hs * cd[:, :, None, None] + cs, hs

    h_final, h_in = lax.scan(step, h0, (chunk_state.swapaxes(0, 1), chunk_decay.swapaxes(0, 1)))
    h_in = h_in.swapaxes(0, 1)
    y_inter = jnp.einsum('bnihk,bnhkv->bnihv', q * jnp.exp(cum)[..., None], h_in)
    return (y_intra + y_inter).reshape(bsz, t, h, dv), h_final


def _chunked_vector_decay(q, k, v, log_a, h0, chunk):
    bsz, t, h, dk = q.shape
    dv = v.shape[-1]
    c = math.gcd(t, chunk)
    n = t // c
    q = q.reshape(bsz, n, c, h, dk)
    k = k.reshape(bsz, n, c, h, dk)
    v = v.reshape(bsz, n, c, h, dv)
    cum = jnp.cumsum(log_a.reshape(bsz, n, c, h, dk), axis=2)
    mask = _causal_mask(c)[None, None, :, :, None, None]
    seg = cum[:, :, :, None] - cum[:, :, None, :]
    decay = jnp.exp(jnp.where(mask, seg, -jnp.inf))
    scores = jnp.einsum('bnihk,bnjhk,bnijhk->bnijh', q, k, decay)
    y_intra = jnp.einsum('bnijh,bnjhv->bnihv', scores, v)
    dec_end = jnp.exp(cum[:, :, -1:] - cum)
    chunk_state = jnp.einsum('bnjhk,bnjhv->bnhkv', k * dec_end, v)
    chunk_decay = jnp.exp(cum[:, :, -1])

    def step(hs, inp):
        cs, cd = inp
        return hs * cd[..., None] + cs, hs

    h_final, h_in = lax.scan(step, h0, (chunk_state.swapaxes(0, 1), chunk_decay.swapaxes(0, 1)))
    h_in = h_in.swapaxes(0, 1)
    y_inter = jnp.einsum('bnihk,bnhkv->bnihv', q * jnp.exp(cum), h_in)
    return (y_intra + y_inter).reshape(bsz, t, h, dv), h_final


def _rotary(x, pos):
    half = x.shape[-1] // 2
    inv = 1.0 / (ROPE_BASE ** jnp.linspace(0.0, 1.0, half, dtype=jnp.float32))
    ang = pos[:, None] * inv[None, :]
    cos = jnp.cos(ang)[None, :, None, :]
    sin = jnp.sin(ang)[None, :, None, :]
    x1 = x[..., 0::2]
    x2 = x[..., 1::2]
    return jnp.stack([x1 * cos - x2 * sin, x1 * sin + x2 * cos], axis=-1).reshape(x.shape)


def _rwkv7(p, shift_prev, state, lp):
    bsz, t = p.shape[:2]
    f32 = jnp.float32
    prev = jnp.concatenate([shift_prev[:, None].astype(p.dtype), p[:, :-1]], axis=1)
    xs = p + (prev - p) * lp['rw_mu']
    r, k, v, wd, ad, gd = jnp.split(xs, RW_SPLITS, axis=-1)
    w = -jax.nn.softplus(-(lp['rw_w0'] + jnp.tanh(wd) @ lp['rw_w_up'])) - 0.5
    decay = jnp.exp(-jnp.exp(w.astype(f32)))
    a = jax.nn.sigmoid((lp['rw_a0'] + ad @ lp['rw_a_up']).astype(f32))
    g = jax.nn.sigmoid(gd) @ lp['rw_g_up']
    heads = lambda z: z.reshape(bsz, t, RW_H, RW_HD).astype(f32)
    r, k, v, a, decay = heads(r), heads(k), heads(v), heads(a), heads(decay)
    kk = k * lp['rw_k_k'].reshape(RW_H, RW_HD).astype(f32)
    kk = kk * lax.rsqrt(jnp.maximum(jnp.sum(kk * kk, -1, keepdims=True), 1e-24))
    k = k * (1.0 + (a - 1.0) * lp['rw_k_a'].reshape(RW_H, RW_HD).astype(f32))

    def step(s, inp):
        r_t, d_t, k_t, v_t, kk_t, a_t = inp
        sa = jnp.einsum('bhvk,bhk->bhv', s, kk_t)
        s = (s * d_t[:, :, None, :] - sa[..., None] * (kk_t * a_t)[:, :, None, :]
             + v_t[..., None] * k_t[:, :, None, :])
        return s, jnp.einsum('bhvk,bhk->bhv', s, r_t)

    seq = tuple(z.swapaxes(0, 1) for z in (r, decay, k, v, kk, a))
    s_new, o = lax.scan(step, state.astype(f32), seq)
    o = o.swapaxes(0, 1)
    o = _head_layer_norm(o, lp['rw_ln_g'].reshape(RW_H, RW_HD), lp['rw_ln_b'].reshape(RW_H, RW_HD), RW_LN_EPS)
    o = o + jnp.sum(r * k * lp['rw_r_k'].astype(f32), -1, keepdims=True) * v
    o = o.reshape(bsz, t, GROUP) * g.astype(f32)
    return o.astype(p.dtype), p[:, -1], s_new.astype(state.dtype)


def _mamba2(p, conv_prev, ssm_state, lp):
    bsz, t = p.shape[:2]
    f32 = jnp.float32
    z, xbc, dt = jnp.split(p, MB_SPLITS, axis=-1)
    xpad = jnp.concatenate([conv_prev.astype(p.dtype), xbc], axis=1)
    cw = lp['mb_conv_w']
    conv = lp['mb_conv_b'] + sum(xpad[:, i:i + t] * cw[i] for i in range(MB_CONV))
    new_conv = xpad[:, t:]
    xbc = jax.nn.silu(conv.astype(f32))
    xh, bm, cm = jnp.split(xbc, (GROUP, GROUP + MB_G * MB_N), axis=-1)
    xh = xh.reshape(bsz, t, MB_H, MB_HD)
    rep = MB_H // MB_G
    bm = jnp.repeat(bm.reshape(bsz, t, MB_G, MB_N), rep, axis=2)
    cm = jnp.repeat(cm.reshape(bsz, t, MB_G, MB_N), rep, axis=2)
    dt = jax.nn.softplus((dt + lp['mb_dt_bias']).astype(f32))
    a_neg = -jnp.exp(lp['mb_a_log'].astype(f32))
    y, s_new = _chunked_scalar_decay(cm, bm, xh * dt[..., None], dt * a_neg,
                                     ssm_state.astype(f32), MB_CHUNK)
    y = y + lp['mb_d'].astype(f32)[:, None] * xh
    y = y.reshape(bsz, t, GROUP) * jax.nn.silu(z.astype(f32))
    y = _rms_norm(y.reshape(bsz, t, MB_G, GROUP // MB_G),
                  lp['mb_norm_g'].reshape(MB_G, GROUP // MB_G), 1e-5).reshape(bsz, t, GROUP)
    return y.astype(p.dtype), new_conv, s_new.astype(ssm_state.dtype)


def _gla(p, state, lp):
    bsz, t = p.shape[:2]
    f32 = jnp.float32
    q, k, v, gkd, g = jnp.split(p, GLA_SPLITS, axis=-1)
    log_a = jax.nn.log_sigmoid((gkd @ lp['gla_gk_up'] + lp['gla_gk_b']).astype(f32)) / GLA_TAU
    hk = lambda z: z.reshape(bsz, t, GLA_H, GLA_DK).astype(f32)
    q = hk(q) * GLA_DK ** -0.5
    v = v.reshape(bsz, t, GLA_H, GLA_DV).astype(f32)
    o, s_new = _chunked_vector_decay(q, hk(k), v, hk(log_a), state.astype(f32), GLA_CHUNK)
    o = _rms_norm(o, lp['gla_norm_g'].reshape(GLA_H, GLA_DV), 1e-5)
    o = o.reshape(bsz, t, GROUP) * jax.nn.silu(g.astype(f32))
    return o.astype(p.dtype), s_new.astype(state.dtype)


def _retention(p, pos0, state, lp):
    bsz, t = p.shape[:2]
    f32 = jnp.float32
    q, k, v, g = jnp.split(p, 4, axis=-1)
    heads = lambda z: z.reshape(bsz, t, RET_H, RET_HD).astype(f32)
    pos = pos0 + jnp.arange(t, dtype=f32)
    q = _rotary(heads(q), pos)
    k = _rotary(heads(k), pos) * RET_HD ** -0.5
    log_gamma = jnp.log1p(-jnp.exp2(-5.0 - jnp.arange(RET_H, dtype=f32)))
    log_a = jnp.broadcast_to(log_gamma, (bsz, t, RET_H))
    o, s_new = _chunked_scalar_decay(q, k, heads(v), log_a, state.astype(f32), RET_CHUNK)
    o = _rms_norm(o, lp['ret_norm_g'].reshape(RET_H, RET_HD), 1e-5)
    o = o.reshape(bsz, t, GROUP) * jax.nn.silu(g.astype(f32))
    return o.astype(p.dtype), s_new.astype(state.dtype)


def _mem_kv(mem, wk, wv):
    bsz = mem.shape[0]
    k = (mem @ wk).reshape(bsz, N_MEM, XA_H, XA_HD)
    v = (mem @ wv).reshape(bsz, N_MEM, XA_H, XA_HD)
    return k, v


def _cross_attn(x, mem_k, mem_v, wq, wo):
    bsz, t = x.shape[:2]
    q = (x @ wq).reshape(bsz, t, XA_H, XA_HD)
    s = jnp.einsum('bthd,bmhd->bhtm', q, mem_k).astype(jnp.float32) * XA_HD ** -0.5
    pr = jax.nn.softmax(s, axis=-1).astype(x.dtype)
    o = jnp.einsum('bhtm,bmhd->bthd', pr, mem_v).reshape(bsz, t, D_MODEL)
    return o @ wo


def _layer(x, pos0, rw_shift, rw_state, mb_conv, mb_state, gla_state, ret_state, mem_k, mem_v, lp):
    p = x @ lp['w_in']
    p_rw, p_mb, p_gla, p_ret = jnp.split(p, IN_SPLITS, axis=-1)
    o_rw, rw_shift, rw_state = _rwkv7(p_rw, rw_shift, rw_state, lp)
    o_mb, mb_conv, mb_state = _mamba2(p_mb, mb_conv, mb_state, lp)
    o_gla, gla_state = _gla(p_gla, gla_state, lp)
    o_ret, ret_state = _retention(p_ret, pos0, ret_state, lp)
    mix = jnp.concatenate([o_rw, o_mb, o_gla, o_ret], axis=-1) @ lp['w_out']
    x = _layer_norm(DN_ALPHA * x + mix, lp['ln1_g'], lp['ln1_b'])
    x = _layer_norm(DN_ALPHA * x + _cross_attn(x, mem_k, mem_v, lp['xa_wq'], lp['xa_wo']),
                    lp['ln2_g'], lp['ln2_b'])
    h = jax.nn.silu(x @ lp['ffn_w_gate']) * (x @ lp['ffn_w_up'])
    x = _layer_norm(DN_ALPHA * x + h @ lp['ffn_w_down'], lp['ln3_g'], lp['ln3_b'])
    return x, rw_shift, rw_state, mb_conv, mb_state, gla_state, ret_state


def setup_inputs(seed: int = 0) -> dict:
    key = jax.random.key(seed)
    ks = iter(jax.random.split(key, 64))
    f32 = jnp.float32
    nrm = lambda shape, scale: jax.random.normal(next(ks), shape, f32) * scale
    gain = lambda shape: 1.0 + nrm(shape, 0.02)
    uni = lambda shape, lo, hi: jax.random.uniform(next(ks), shape, f32, minval=lo, maxval=hi)
    L = DEPTH
    mb_dt = jnp.exp(uni((L, MB_H), math.log(1e-3), math.log(1e-1)))
    return {
        'x_prompt': nrm((BATCH, SEQ, D_MODEL), 1.0),
        'x_sample': nrm((DEC_BATCH, DEC_SEQ, D_MODEL), 1.0),
        'state_rwkv_shift': nrm((L, DEC_BATCH, RW_IN), 1.0),
        'state_rwkv_wkv': nrm((L, DEC_BATCH, RW_H, RW_HD, RW_HD), 0.3),
        'state_mamba_conv': nrm((L, DEC_BATCH, MB_CONV - 1, MB_CONV_DIM), 1.0),
        'state_mamba_ssm': nrm((L, DEC_BATCH, MB_H, MB_N, MB_HD), 0.3),
        'state_gla': nrm((L, DEC_BATCH, GLA_H, GLA_DK, GLA_DV), 0.3),
        'state_ret': nrm((L, DEC_BATCH, RET_H, RET_HD, RET_HD), 0.3),
        'cache_mem_k': nrm((L, DEC_BATCH, N_MEM, XA_H, XA_HD), 1.0),
        'cache_mem_v': nrm((L, DEC_BATCH, N_MEM, XA_H, XA_HD), 1.0),
        'mem_prompt': nrm((BATCH, N_MEM, D_MODEL), 1.0),
        'w_in': nrm((L, D_MODEL, N_IN), D_MODEL ** -0.5),
        'w_out': nrm((L, MIX, D_MODEL), MIX ** -0.5 * DN_BETA),
        'ln1_g': gain((L, D_MODEL)),
        'ln1_b': nrm((L, D_MODEL), 0.02),
        'rw_mu': uni((L, RW_IN), 0.0, 1.0),
        'rw_w0': uni((L, GROUP), -6.0, -1.0),
        'rw_w_up': nrm((L, RW_W_LORA, GROUP), 0.1),
        'rw_a0': nrm((L, GROUP), 0.1),
        'rw_a_up': nrm((L, RW_A_LORA, GROUP), 0.1),
        'rw_g_up': nrm((L, RW_G_LORA, GROUP), RW_G_LORA ** -0.5),
        'rw_k_k': 0.85 + nrm((L, GROUP), 0.02),
        'rw_k_a': gain((L, GROUP)),
        'rw_r_k': nrm((L, RW_H, RW_HD), 0.1),
        'rw_ln_g': gain((L, GROUP)),
        'rw_ln_b': nrm((L, GROUP), 0.02),
        'mb_conv_w': nrm((L, MB_CONV, MB_CONV_DIM), MB_CONV ** -0.5),
        'mb_conv_b': nrm((L, MB_CONV_DIM), 0.02),
        'mb_dt_bias': mb_dt + jnp.log(-jnp.expm1(-mb_dt)),
        'mb_a_log': jnp.log(uni((L, MB_H), 1.0, 16.0)),
        'mb_d': gain((L, MB_H)),
        'mb_norm_g': gain((L, GROUP)),
        'gla_gk_up': nrm((L, GLA_LORA, GLA_QK), GLA_LORA ** -0.5),
        'gla_gk_b': nrm((L, GLA_QK), 0.02),
        'gla_norm_g': gain((L, GROUP)),
        'ret_norm_g': gain((L, GROUP)),
        'ln2_g': gain((L, D_MODEL)),
        'ln2_b': nrm((L, D_MODEL), 0.02),
        'xa_wq': nrm((L, D_MODEL, D_MODEL), D_MODEL ** -0.5),
        'xa_wk': nrm((L, D_MODEL, D_MODEL), D_MODEL ** -0.5),
        'xa_wv': nrm((L, D_MODEL, D_MODEL), D_MODEL ** -0.5),
        'xa_wo': nrm((L, D_MODEL, D_MODEL), D_MODEL ** -0.5 * DN_BETA),
        'ln3_g': gain((L, D_MODEL)),
        'ln3_b': nrm((L, D_MODEL), 0.02),
        'ffn_w_gate': nrm((L, D_MODEL, D_FF), D_MODEL ** -0.5),
        'ffn_w_up': nrm((L, D_MODEL, D_FF), D_MODEL ** -0.5),
        'ffn_w_down': nrm((L, D_FF, D_MODEL), D_FF ** -0.5 * DN_BETA),
    }


def reference(x_prompt, x_sample, state_rwkv_shift, state_rwkv_wkv, state_mamba_conv, state_mamba_ssm,
              state_gla, state_ret, cache_mem_k, cache_mem_v, mem_prompt,
              w_in, w_out, ln1_g, ln1_b, rw_mu, rw_w0, rw_w_up, rw_a0, rw_a_up, rw_g_up, rw_k_k, rw_k_a,
              rw_r_k, rw_ln_g, rw_ln_b, mb_conv_w, mb_conv_b, mb_dt_bias, mb_a_log, mb_d, mb_norm_g,
              gla_gk_up, gla_gk_b, gla_norm_g, ret_norm_g, ln2_g, ln2_b, xa_wq, xa_wk, xa_wv, xa_wo,
              ln3_g, ln3_b, ffn_w_gate, ffn_w_up, ffn_w_down):
    bp = x_prompt.shape[0]
    dt_p = x_prompt.dtype
    yp = x_prompt
    ys = x_sample
    p_sh, p_wkv, p_cv, p_ssm, p_gl, p_rt, p_mk, p_mv = [], [], [], [], [], [], [], []
    s_sh, s_wkv, s_cv, s_ssm, s_gl, s_rt = [], [], [], [], [], []
    for i in range(DEPTH):
        lp = {
            'w_in': w_in[i], 'w_out': w_out[i], 'ln1_g': ln1_g[i], 'ln1_b': ln1_b[i],
            'rw_mu': rw_mu[i], 'rw_w0': rw_w0[i], 'rw_w_up': rw_w_up[i], 'rw_a0': rw_a0[i],
            'rw_a_up': rw_a_up[i], 'rw_g_up': rw_g_up[i], 'rw_k_k': rw_k_k[i], 'rw_k_a': rw_k_a[i],
            'rw_r_k': rw_r_k[i], 'rw_ln_g': rw_ln_g[i], 'rw_ln_b': rw_ln_b[i],
            'mb_conv_w': mb_conv_w[i], 'mb_conv_b': mb_conv_b[i], 'mb_dt_bias': mb_dt_bias[i],
            'mb_a_log': mb_a_log[i], 'mb_d': mb_d[i], 'mb_norm_g': mb_norm_g[i],
            'gla_gk_up': gla_gk_up[i], 'gla_gk_b': gla_gk_b[i], 'gla_norm_g': gla_norm_g[i],
            'ret_norm_g': ret_norm_g[i], 'ln2_g': ln2_g[i], 'ln2_b': ln2_b[i],
            'xa_wq': xa_wq[i], 'xa_wo': xa_wo[i], 'ln3_g': ln3_g[i], 'ln3_b': ln3_b[i],
            'ffn_w_gate': ffn_w_gate[i], 'ffn_w_up': ffn_w_up[i], 'ffn_w_down': ffn_w_down[i],
        }
        mk, mv = _mem_kv(mem_prompt, xa_wk[i], xa_wv[i])
        yp, sh, wkv, cv, ssm, gl, rt = _layer(
            yp, 0,
            jnp.zeros((bp, RW_IN), dt_p), jnp.zeros((bp, RW_H, RW_HD, RW_HD), dt_p),
            jnp.zeros((bp, MB_CONV - 1, MB_CONV_DIM), dt_p), jnp.zeros((bp, MB_H, MB_N, MB_HD), dt_p),
            jnp.zeros((bp, GLA_H, GLA_DK, GLA_DV), dt_p), jnp.zeros((bp, RET_H, RET_HD, RET_HD), dt_p),
            mk, mv, lp)
        p_sh.append(sh); p_wkv.append(wkv); p_cv.append(cv); p_ssm.append(ssm)
        p_gl.append(gl); p_rt.append(rt); p_mk.append(mk); p_mv.append(mv)
        ys, sh, wkv, cv, ssm, gl, rt = _layer(
            ys, PAST_LEN, state_rwkv_shift[i], state_rwkv_wkv[i], state_mamba_conv[i],
            state_mamba_ssm[i], state_gla[i], state_ret[i], cache_mem_k[i], cache_mem_v[i], lp)
        s_sh.append(sh); s_wkv.append(wkv); s_cv.append(cv); s_ssm.append(ssm)
        s_gl.append(gl); s_rt.append(rt)
    p_rwkv_shift = jnp.stack(p_sh)
    p_rwkv_wkv = jnp.stack(p_wkv)
    p_mamba_conv = jnp.stack(p_cv)
    p_mamba_ssm = jnp.stack(p_ssm)
    p_gla = jnp.stack(p_gl)
    p_ret = jnp.stack(p_rt)
    p_mem_k = jnp.stack(p_mk)
    p_mem_v = jnp.stack(p_mv)
    s_rwkv_shift = jnp.stack(s_sh)
    s_rwkv_wkv = jnp.stack(s_wkv)
    s_mamba_conv = jnp.stack(s_cv)
    s_mamba_ssm = jnp.stack(s_ssm)
    s_gla = jnp.stack(s_gl)
    s_ret = jnp.stack(s_rt)
    return (yp, ys, p_rwkv_shift, p_rwkv_wkv, p_mamba_conv, p_mamba_ssm, p_gla, p_ret, p_mem_k, p_mem_v,
            s_rwkv_shift, s_rwkv_wkv, s_mamba_conv, s_mamba_ssm, s_gla, s_ret)
```

```python
import functools
import math

import jax
import jax.numpy as jnp
from jax import lax
from jax.experimental import pallas as pl
from jax.experimental.pallas import tpu as pltpu

F32 = jnp.float32
BF16 = jnp.bfloat16

D_MODEL = 2048
DEPTH = 2
PAST_LEN = 16384
GROUP = 512
RW_H, RW_HD = 8, 64
RW_IN = 1792
RW_LN_EPS = 64e-5
MB_H, MB_HD, MB_N, MB_G = 8, 64, 128, 2
MB_CONV = 4
MB_CONV_DIM = 1024
GLA_H, GLA_DK, GLA_DV = 4, 64, 128
GLA_QK = 256
GLA_LORA = 16
GLA_TAU = 16.0
RET_H, RET_HD = 4, 128
ROPE_BASE = 10000.0
N_MEM = 256
XA_H, XA_HD = 4, 512
D_FF = 5632
DN_ALPHA = (2 * DEPTH) ** 0.25
RET_LOG_GAMMA = tuple(math.log1p(-(2.0 ** (-5.0 - h))) for h in range(RET_H))

N_PAD = 7168
COL_MB_DT = 1792
COL_GLA_GK = 1920
COL_MB_XBC = 2048
COL_MB_Z = 3072
COL_GLA_V = 3584
COL_GLA_G = 4096
COL_GLA_Q = 4608
COL_GLA_K = 4864
COL_RET = 5120

CHUNK = 128
VMEM_LIMIT = 56 * 1024 * 1024


def _cparams(n_axes, vmem=VMEM_LIMIT):
    return pltpu.CompilerParams(dimension_semantics=("arbitrary",) * n_axes, vmem_limit_bytes=vmem)


def _dot(a, b):
    return jnp.dot(a.astype(BF16), b.astype(BF16), preferred_element_type=F32)


def _dot_nt(a, b):
    return lax.dot_general(a.astype(BF16), b.astype(BF16), (((1,), (1,)), ((), ())), preferred_element_type=F32)


def _dot_tn(a, b):
    return lax.dot_general(a.astype(BF16), b.astype(BF16), (((0,), (0,)), ((), ())), preferred_element_type=F32)


def _sigmoid(x):
    return 1.0 / (1.0 + jnp.exp(-x))


def _silu(x):
    return x * _sigmoid(x)


def _softplus(x):
    return jnp.maximum(x, 0.0) + jnp.log(1.0 + jnp.exp(-jnp.abs(x)))


def _split_dot(m_bf16, x):
    hi = x.astype(BF16)
    r1 = x - hi.astype(F32)
    mid = r1.astype(BF16)
    lo = (r1 - mid.astype(F32)).astype(BF16)
    return (jnp.dot(m_bf16, hi, preferred_element_type=F32)
            + jnp.dot(m_bf16, mid, preferred_element_type=F32)
            + jnp.dot(m_bf16, lo, preferred_element_type=F32))


def _mo(x, m):
    return x if isinstance(x, int) else pl.multiple_of(x, m)


def _seg_masks(c, seg):
    sh = jnp.int32(int(math.log2(seg)))
    r = lax.broadcasted_iota(jnp.int32, (c, c), 0)
    q = lax.broadcasted_iota(jnp.int32, (c, c), 1)
    same = lax.shift_right_arithmetic(r, sh) == lax.shift_right_arithmetic(q, sh)
    incl = jnp.logical_and(same, r >= q)
    strict = jnp.logical_and(same, r > q)
    return incl, strict, same, r, q


def _mm_kernel(*refs, k_sizes):
    n_x = len(k_sizes)
    x_refs, w_ref, o_ref, wbf = refs[:n_x], refs[n_x], refs[n_x + 1], refs[n_x + 2]

    @pl.when(pl.program_id(1) == 0)
    def _():
        wbf[...] = w_ref[...].astype(BF16)

    acc = None
    off = 0
    for xr, ks in zip(x_refs, k_sizes):
        part = jnp.dot(xr[...].astype(BF16), wbf[off:off + ks, :], preferred_element_type=F32)
        acc = part if acc is None else acc + part
        off += ks
    o_ref[...] = acc.astype(o_ref.dtype)


def _matmul(xs, w, out_dtype, tm, tn):
    m = xs[0].shape[0]
    k, n = w.shape
    k_sizes = tuple(x.shape[1] for x in xs)
    assert sum(k_sizes) == k and m % tm == 0 and n % tn == 0
    in_specs = [pl.BlockSpec((tm, ks), lambda j, i: (i, 0)) for ks in k_sizes]
    in_specs.append(pl.BlockSpec((k, tn), lambda j, i: (0, j)))
    return pl.pallas_call(
        functools.partial(_mm_kernel, k_sizes=k_sizes),
        grid=(n // tn, m // tm),
        in_specs=in_specs,
        out_specs=pl.BlockSpec((tm, tn), lambda j, i: (i, j)),
        out_shape=jax.ShapeDtypeStruct((m, n), out_dtype),
        scratch_shapes=[pltpu.VMEM((k, tn), BF16)],
        compiler_params=_cparams(2),
    )(*xs, w)


def _ffn_gu_kernel(x_ref, wg_ref, wu_ref, o_ref, wg_bf, wu_bf):
    @pl.when(pl.program_id(1) == 0)
    def _():
        wg_bf[...] = wg_ref[...].astype(BF16)
        wu_bf[...] = wu_ref[...].astype(BF16)

    x = x_ref[...]
    gate = jnp.dot(x, wg_bf[...], preferred_element_type=F32)
    up = jnp.dot(x, wu_bf[...], preferred_element_type=F32)
    o_ref[...] = (_silu(gate) * up).astype(o_ref.dtype)


def _ffn_gate_up(x_bf, wg, wu, tm, tn):
    m, k = x_bf.shape
    n = wg.shape[1]
    return pl.pallas_call(
        _ffn_gu_kernel,
        grid=(n // tn, m // tm),
        in_specs=[pl.BlockSpec((tm, k), lambda j, i: (i, 0)),
                  pl.BlockSpec((k, tn), lambda j, i: (0, j)),
                  pl.BlockSpec((k, tn), lambda j, i: (0, j))],
        out_specs=pl.BlockSpec((tm, tn), lambda j, i: (i, j)),
        out_shape=jax.ShapeDtypeStruct((m, n), BF16),
        scratch_shapes=[pltpu.VMEM((k, tn), BF16), pltpu.VMEM((k, tn), BF16)],
        compiler_params=_cparams(2),
    )(x_bf, wg, wu)


def _ln_kernel(x_ref, y_ref, g_ref, b_ref, of_ref, ob_ref):
    z = DN_ALPHA * x_ref[...] + y_ref[...]
    zc = z - jnp.mean(z, axis=-1, keepdims=True)
    var = jnp.mean(zc * zc, axis=-1, keepdims=True)
    out = zc * lax.rsqrt(var + 1e-5) * g_ref[...] + b_ref[...]
    of_ref[...] = out
    ob_ref[...] = out.astype(BF16)


def _res_ln(x, y, g, b, tm=256):
    m, d = x.shape
    row = pl.BlockSpec((tm, d), lambda i: (i, 0))
    vec = pl.BlockSpec((1, d), lambda i: (0, 0))
    return pl.pallas_call(
        _ln_kernel,
        grid=(m // tm,),
        in_specs=[row, row, vec, vec],
        out_specs=[row, row],
        out_shape=[jax.ShapeDtypeStruct((m, d), F32), jax.ShapeDtypeStruct((m, d), BF16)],
        compiler_params=_cparams(1),
    )(x, y, g.reshape(1, d), b.reshape(1, d))


def _xattn_kernel(q_ref, k_ref, v_ref, o_ref, *, nb, tq):
    for j in range(nb):
        rows = slice(j * tq, (j + 1) * tq)
        for h in range(XA_H):
            cols = slice(h * XA_HD, (h + 1) * XA_HD)
            s = _dot_nt(q_ref[rows, cols], k_ref[j, :, cols]) * (XA_HD ** -0.5)
            e = jnp.exp(s - jnp.max(s, axis=-1, keepdims=True))
            pr = e / jnp.sum(e, axis=-1, keepdims=True)
            o_ref[rows, cols] = _dot(pr, v_ref[j, :, cols]).astype(o_ref.dtype)


def _cross_attn(q, mem_k, mem_v, bsz, t, nb, tq, out_dtype):
    nt = t // tq
    rows = nb * tq
    return pl.pallas_call(
        functools.partial(_xattn_kernel, nb=nb, tq=tq),
        grid=(bsz // nb, nt),
        in_specs=[pl.BlockSpec((rows, D_MODEL), lambda i, j: (i * nt + j, 0)),
                  pl.BlockSpec((nb, N_MEM, D_MODEL), lambda i, j: (i, 0, 0)),
                  pl.BlockSpec((nb, N_MEM, D_MODEL), lambda i, j: (i, 0, 0))],
        out_specs=pl.BlockSpec((rows, D_MODEL), lambda i, j: (i * nt + j, 0)),
        out_shape=jax.ShapeDtypeStruct((bsz * t, D_MODEL), out_dtype),
        compiler_params=_cparams(2),
    )(q, mem_k, mem_v)


def _state_io(carry, h0_ref, hout_ref, hst):
    if carry:
        return (lambda s, h: hst[h]), (lambda s, h, val: hst.__setitem__(h, val))
    return (lambda s, h: h0_ref[s, h]), (lambda s, h, val: hout_ref.__setitem__((s, h), val))


def _for_segments(nseg, fn):
    if nseg == 1:
        fn(0)
    else:
        def body(s, c):
            fn(s)
            return c
        lax.fori_loop(0, nseg, body, 0)


def _ret_kernel(q_ref, k_ref, v_ref, g_ref, cos_ref, sin_ref, gn_ref, h0_ref, o_ref, hout_ref,
                hst, qs_sc, ks_sc, y_sc, *, rows, seg, carry):
    c = CHUNK
    nseg = c // seg
    if carry:
        @pl.when(pl.program_id(1) == 0)
        def _():
            hst[...] = h0_ref[0]
    get_h, set_h = _state_io(carry, h0_ref, hout_ref, hst)
    incl, _, _, r_i, c_i = _seg_masks(c, seg)
    dpos = (r_i - c_i).astype(F32)
    tau = (lax.broadcasted_iota(jnp.int32, (c, RET_HD), 0) & (seg - 1)).astype(F32)
    even = (lax.broadcasted_iota(jnp.int32, (c, GROUP), 1) & 1) == 0

    for ci in range(rows // c):
        rws = slice(ci * c, (ci + 1) * c)
        cosb = jnp.concatenate([cos_ref[rws, :]] * RET_H, axis=1)
        sinb = jnp.concatenate([sin_ref[rws, :]] * RET_H, axis=1)

        def rot(x):
            swapped = jnp.where(even, pltpu.roll(x, GROUP - 1, 1), pltpu.roll(x, 1, 1))
            return x * cosb + swapped * sinb

        qr = rot(q_ref[rws, :])
        kr = rot(k_ref[rws, :]) * (RET_HD ** -0.5)
        for h in range(RET_H):
            lgam = RET_LOG_GAMMA[h]
            cols = slice(h * RET_HD, (h + 1) * RET_HD)
            qh, kh = qr[:, cols], kr[:, cols]
            dm = jnp.where(incl, jnp.exp(dpos * lgam), 0.0)
            y_sc[...] = _dot(_dot_nt(qh, kh) * dm, v_ref[rws, cols])
            qs_sc[...] = qh * jnp.exp((tau + 1.0) * lgam)
            ks_sc[...] = kh * jnp.exp((seg - 1.0 - tau) * lgam)
            cd = math.exp(seg * lgam)

            def seg_step(s, h=h, cols=cols, cd=cd, ci=ci):
                sr = pl.ds(_mo(s * seg, seg), seg)
                vr = pl.ds(_mo(ci * c + s * seg, seg), seg)
                hs = get_h(s, h)
                y_sc[sr, :] += _dot(qs_sc[sr, :], hs)
                set_h(s, h, cd * hs + _dot_tn(ks_sc[sr, :], v_ref[vr, cols]))

            _for_segments(nseg, seg_step)
            y = y_sc[...]
            yn = y * lax.rsqrt(jnp.mean(y * y, axis=-1, keepdims=True) + 1e-5) * gn_ref[:, cols]
            o_ref[rws, cols] = (yn * _silu(g_ref[rws, cols])).astype(o_ref.dtype)

    if carry:
        @pl.when(pl.program_id(1) == pl.num_programs(1) - 1)
        def _():
            hout_ref[0] = hst[...]


def _mamba_kernel(z_ref, xbc_ref, dt_ref, cprev_ref, h0_ref, cw_ref, cb_ref, dtb_ref, alog_ref, dd_ref,
                  ng_ref, o_ref, cout_ref, hout_ref,
                  hst, xpad, act_sc, qs_sc, ks_sc, xs_sc, y_sc, yf_sc, ecum_sc, etot_sc, *, rows, seg, carry):
    c = CHUNK
    nseg = c // seg
    get_h, set_h = _state_io(carry, h0_ref, hout_ref, hst)

    def conv_act(window, n):
        acc = cb_ref[...] + window[5:5 + n] * cw_ref[0:1, :]
        for i in range(1, MB_CONV):
            acc = acc + window[5 + i:5 + i + n] * cw_ref[i:i + 1, :]
        return _silu(acc)

    if carry:
        @pl.when(pl.program_id(1) == 0)
        def _():
            hst[...] = h0_ref[0]
            xpad[0:8, :] = jnp.zeros((8, MB_CONV_DIM), F32)
            xpad[5:8, :] = cprev_ref[0]

        xpad[8:8 + rows, :] = xbc_ref[...]
        for ci in range(rows // c):
            acc = cb_ref[...] + xpad[ci * c + 5:ci * c + 5 + c, :] * cw_ref[0:1, :]
            for i in range(1, MB_CONV):
                acc = acc + xpad[ci * c + 5 + i:ci * c + 5 + i + c, :] * cw_ref[i:i + 1, :]
            act_sc[ci * c:(ci + 1) * c, :] = _silu(acc)
        xpad[0:8, :] = xpad[rows:rows + 8, :]

        @pl.when(pl.program_id(1) == pl.num_programs(1) - 1)
        def _():
            cout_ref[0] = xpad[5:8, :]
    else:
        def conv_seq(s, carry_):
            sr = pl.ds(pl.multiple_of(s * seg, seg), seg)
            xpad[5:8, :] = cprev_ref[s]
            xs = xbc_ref[sr, :]
            window = jnp.concatenate([xpad[0:8, :], xs], axis=0)
            act_sc[sr, :] = conv_act(window, seg)
            cout_ref[s] = xs[seg - 3:seg]
            return carry_

        xpad[0:8, :] = jnp.zeros((8, MB_CONV_DIM), F32)
        lax.fori_loop(0, rows // seg, conv_seq, 0)

    incl, _, same, _, _ = _seg_masks(c, seg)
    lt_bf = jnp.where(incl, 1.0, 0.0).astype(BF16)
    same_bf = jnp.where(same, 1.0, 0.0).astype(BF16)
    a_neg = -jnp.exp(alog_ref[...])

    for ci in range(rows // c):
        rws = slice(ci * c, (ci + 1) * c)
        dtv = _softplus(dt_ref[rws, :] + dtb_ref[...])
        la = dtv * a_neg
        cum = _split_dot(lt_bf, la)
        tot = _split_dot(same_bf, la)
        cum_t = cum.T
        ecum_sc[...] = jnp.exp(cum)
        etot_sc[...] = jnp.exp(tot)
        edec = jnp.exp(tot - cum)
        for g in range(MB_G):
            cg = act_sc[rws, 768 + g * MB_N:768 + (g + 1) * MB_N]
            bg = act_sc[rws, 512 + g * MB_N:512 + (g + 1) * MB_N]
            gmat = _dot_nt(cg, bg)
            qs_sc[...] = cg
            ks_sc[...] = bg
            for hh in range(MB_H // MB_G):
                h = g * (MB_H // MB_G) + hh
                cols = slice(h * MB_HD, (h + 1) * MB_HD)
                lmat = jnp.exp(jnp.where(incl, cum[:, h:h + 1] - cum_t[h:h + 1, :], -jnp.inf))
                xh = act_sc[rws, cols]
                xdt = xh * dtv[:, h:h + 1]
                y_sc[...] = _dot(gmat * lmat, xdt)
                xs_sc[...] = xdt * edec[:, h:h + 1]

                def seg_step(s, h=h):
                    sr = pl.ds(_mo(s * seg, seg), seg)
                    first = pl.ds(_mo(s * seg, seg), 1)
                    hs = get_h(s, h)
                    y_sc[sr, :] += _dot(qs_sc[sr, :], hs) * ecum_sc[sr, h:h + 1]
                    set_h(s, h, etot_sc[first, h:h + 1] * hs + _dot_tn(ks_sc[sr, :], xs_sc[sr, :]))

                _for_segments(nseg, seg_step)
                yf_sc[:, cols] = y_sc[...] + dd_ref[:, h:h + 1] * xh
        yz = yf_sc[...] * _silu(z_ref[rws, :])
        gw = GROUP // MB_G
        for g in range(MB_G):
            cols = slice(g * gw, (g + 1) * gw)
            part = yz[:, cols]
            nrm = part * lax.rsqrt(jnp.mean(part * part, axis=-1, keepdims=True) + 1e-5) * ng_ref[:, cols]
            o_ref[rws, cols] = nrm.astype(o_ref.dtype)

    if carry:
        @pl.when(pl.program_id(1) == pl.num_programs(1) - 1)
        def _():
            hout_ref[0] = hst[...]


def _gla_kernel(q_ref, k_ref, v_ref, gk_ref, g_ref, h0_ref, gkup_ref, gkb_ref, ng_ref, e_ref,
                o_ref, hout_ref,
                hst, cum_sc, q_sc, qe_sc, ke_sc, etot_sc, p_sc, y_sc, *, rows, seg, carry):
    c = CHUNK
    nsub = c // seg
    if carry:
        @pl.when(pl.program_id(1) == 0)
        def _():
            for h in range(GLA_H):
                hst[h] = h0_ref[0, h].T
    incl, _, same, _, _ = _seg_masks(c, seg)
    lt_bf = jnp.where(incl, 1.0, 0.0).astype(BF16)
    same_bf = jnp.where(same, 1.0, 0.0).astype(BF16)
    row_i = lax.broadcasted_iota(jnp.int32, (seg, GLA_QK), 0)

    for ci in range(rows // c):
        rws = slice(ci * c, (ci + 1) * c)
        pre = _dot(gk_ref[rws, :], gkup_ref[...]) + gkb_ref[...]
        la = -_softplus(-pre) * (1.0 / GLA_TAU)
        cum = _split_dot(lt_bf, la)
        tot = _split_dot(same_bf, la)
        qv = q_ref[rws, :] * (GLA_DK ** -0.5)
        cum_sc[...] = cum
        q_sc[...] = qv
        qe_sc[...] = qv * jnp.exp(cum)
        ke_sc[...] = k_ref[rws, :] * jnp.exp(tot - cum)
        etot_sc[...] = jnp.exp(tot)

        def sub_step(u, ci=ci):
            base = pl.multiple_of(u * seg, seg)
            sr = pl.ds(base, seg)
            gbase = ci * c + base
            cu = cum_sc[sr, :]
            qu = q_sc[sr, :]
            for j in range(seg):
                cj = cum_sc[pl.ds(base + j, 1), :]
                kj = k_ref[pl.ds(gbase + j, 1), :]
                pj = qu * jnp.exp(jnp.where(row_i >= j, cu - cj, -jnp.inf)) * kj
                p_sc[j * seg:(j + 1) * seg, :] = pj
            rm = _dot(p_sc[...], e_ref[...])
            y = rm[0:seg, :] * v_ref[pl.ds(gbase, 1), :]
            for j in range(1, seg):
                y = y + rm[j * seg:(j + 1) * seg, :] * v_ref[pl.ds(gbase + j, 1), :]
            first = pl.ds(base, 1)
            for h in range(GLA_H):
                kc = slice(h * GLA_DK, (h + 1) * GLA_DK)
                vc = slice(h * GLA_DV, (h + 1) * GLA_DV)
                ht = hst[h] if carry else h0_ref[u, h].T
                y_sc[sr, vc] = y[:, vc] + _dot_nt(qe_sc[sr, kc], ht)
                new = ht * etot_sc[first, kc] + _dot_tn(v_ref[pl.ds(gbase, seg), vc], ke_sc[sr, kc])
                if carry:
                    hst[h] = new
                else:
                    hout_ref[u, h] = new.T

        _for_segments(nsub, sub_step)
        for h in range(GLA_H):
            vc = slice(h * GLA_DV, (h + 1) * GLA_DV)
            y = y_sc[:, vc]
            yn = y * lax.rsqrt(jnp.mean(y * y, axis=-1, keepdims=True) + 1e-5) * ng_ref[:, vc]
            o_ref[rws, vc] = (yn * _silu(g_ref[rws, vc])).astype(o_ref.dtype)

    if carry:
        @pl.when(pl.program_id(1) == pl.num_programs(1) - 1)
        def _():
            for h in range(GLA_H):
                hout_ref[0, h] = hst[h].T


def _rwkv_kernel(p_ref, sp_ref, h0_ref, mu_ref, w0_ref, wup_ref, a0_ref, aup_ref, gup_ref, kk_ref, ka_ref,
                 rk_ref, lng_ref, lnb_ref, o_ref, hout_ref,
                 hst, last_sc, prev_sc, w_sc, y_sc, rt_sc, bt_sc, kt_sc, v_sc, u_sc, o0_sc, etot_sc, out_sc,
                 pb_sc, pk_sc, *, rows, seg, sub, carry):
    c = CHUNK
    nsub = c // sub
    n_iter = int(math.log2(sub)) - 1
    if carry:
        @pl.when(pl.program_id(1) == 0)
        def _():
            hst[...] = h0_ref[0]
            last_sc[...] = jnp.broadcast_to(sp_ref[0], (8, RW_IN))
    else:
        for s in range(rows // seg):
            prev_sc[s * seg:(s + 1) * seg, :] = jnp.broadcast_to(sp_ref[s], (seg, RW_IN))
    incl, strict, same, r_i, c_i = _seg_masks(c, sub)
    lt_bf = jnp.where(incl, 1.0, 0.0).astype(BF16)
    same_bf = jnp.where(same, 1.0, 0.0).astype(BF16)
    eye = jnp.where(r_i == c_i, 1.0, 0.0)
    row_w = lax.broadcasted_iota(jnp.int32, (c, RW_IN), 0)
    first_row = (row_w & ((c if carry else seg) - 1)) == 0

    for ci in range(rows // c):
        rws = slice(ci * c, (ci + 1) * c)
        p = p_ref[rws, :]
        if carry:
            if ci == 0:
                prev_row = jnp.broadcast_to(last_sc[0:1, :], (c, RW_IN))
            else:
                prev_row = jnp.broadcast_to(p_ref[ci * c - 1:ci * c, :], (c, RW_IN))
        else:
            prev_row = prev_sc[...]
        prev = jnp.where(first_row, prev_row, pltpu.roll(p, 1, 0))
        xs = p + (prev - p) * mu_ref[...]
        r = xs[:, 0:GROUP]
        k = xs[:, GROUP:2 * GROUP]
        v = xs[:, 2 * GROUP:3 * GROUP]
        wa = xs[:, 3 * GROUP:3 * GROUP + 128]
        gd = xs[:, 3 * GROUP + 128:RW_IN]
        w = -_softplus(-(w0_ref[...] + _dot(jnp.tanh(wa), wup_ref[...]))) - 0.5
        ld = -jnp.exp(w)
        a = _sigmoid(a0_ref[...] + _dot(wa, aup_ref[...]))
        gate = _dot(_sigmoid(gd), gup_ref[...])
        kkr = k * kk_ref[...]
        k2 = k * (1.0 + (a - 1.0) * ka_ref[...])
        cum = _split_dot(lt_bf, ld)
        tot = _split_dot(same_bf, ld)
        ecum = jnp.exp(cum)
        einv = jnp.exp(-cum)
        eprev = jnp.exp(cum - ld)
        rkb = r * k2 * rk_ref[...]
        etot_sc[...] = jnp.exp(tot)
        v_sc[...] = v
        rt_sc[...] = r * ecum
        kt_sc[...] = k2 * einv
        braw = kkr * a * einv
        kraw = kkr * eprev
        u_sc[...] = jnp.zeros((c, GROUP), F32)

        for h in range(RW_H):
            cols = slice(h * RW_HD, (h + 1) * RW_HD)
            kkh = kkr[:, cols]
            rn = lax.rsqrt(jnp.maximum(jnp.sum(kkh * kkh, axis=-1, keepdims=True), 1e-24))
            kap = kraw[:, cols] * rn
            bt = braw[:, cols] * rn
            bt_sc[:, cols] = bt
            m1 = _dot_nt(jnp.concatenate([kap, rt_sc[:, cols]], axis=0),
                         jnp.concatenate([bt, kt_sc[:, cols]], axis=0))
            x = jnp.where(strict, -m1[0:c, 0:c], 0.0)
            ak = jnp.where(strict, m1[0:c, c:2 * c], 0.0)
            pb_sc[h] = jnp.where(incl, m1[c:2 * c, 0:c], 0.0)
            pk_sc[h] = jnp.where(incl, m1[c:2 * c, c:2 * c], 0.0)
            tmat = eye + x
            pw = x
            for _ in range(n_iter):
                pw = _dot(pw, pw)
                tmat = tmat + _dot(tmat, pw)
            akv = _dot(ak, v[:, cols])
            wy = _dot(tmat, jnp.concatenate([kap, akv], axis=1))
            w_sc[:, cols] = wy[:, 0:RW_HD]
            y_sc[:, cols] = wy[:, RW_HD:2 * RW_HD]

        def sub_step(u):
            base = pl.multiple_of(u * sub, sub)
            sr = pl.ds(base, sub)
            first = pl.ds(base, 1)
            for h in range(RW_H):
                cols = slice(h * RW_HD, (h + 1) * RW_HD)
                st = hst[h] if carry else h0_ref[u, h]
                wr = _dot_nt(jnp.concatenate([w_sc[sr, cols], rt_sc[sr, cols]], axis=0), st)
                uu = -wr[0:sub, :] - y_sc[sr, cols]
                u_sc[sr, cols] = uu
                o0_sc[sr, cols] = wr[sub:2 * sub, :]
                upd = _dot_tn(jnp.concatenate([uu, v_sc[sr, cols]], axis=0),
                              jnp.concatenate([bt_sc[sr, cols], kt_sc[sr, cols]], axis=0))
                new = (st + upd) * etot_sc[first, cols]
                if carry:
                    hst[h] = new
                else:
                    hout_ref[u, h] = new

        _for_segments(nsub, sub_step)

        for h in range(RW_H):
            cols = slice(h * RW_HD, (h + 1) * RW_HD)
            vh = v[:, cols]
            oh = o0_sc[:, cols] + _dot(pb_sc[h], u_sc[:, cols]) + _dot(pk_sc[h], vh)
            oc = oh - jnp.mean(oh, axis=-1, keepdims=True)
            var = jnp.mean(oc * oc, axis=-1, keepdims=True)
            on = oc * lax.rsqrt(var + RW_LN_EPS) * lng_ref[:, cols] + lnb_ref[:, cols]
            out_sc[:, cols] = on + jnp.sum(rkb[:, cols], axis=-1, keepdims=True) * vh
        o_ref[rws, :] = (out_sc[...] * gate).astype(o_ref.dtype)

    if carry:
        last_sc[...] = jnp.broadcast_to(p_ref[rows - 1:rows, :], (8, RW_IN))

        @pl.when(pl.program_id(1) == pl.num_programs(1) - 1)
        def _():
            hout_ref[0] = hst[...]


class _Group:
    def __init__(self, bsz, t, rows):
        self.bsz, self.t, self.rows = bsz, t, rows
        self.carry = t >= CHUNK
        self.nb = 1 if self.carry else rows // t
        self.nt = t // rows if self.carry else 1
        self.grid = (bsz // self.nb, self.nt)
        self.seg = CHUNK if self.carry else t

    def rows_spec(self, width, col_block):
        nt = self.nt
        return pl.BlockSpec((self.rows, width), lambda i, j: (i * nt + j, col_block))

    def state_spec(self, shape):
        zeros = (0,) * len(shape)
        return pl.BlockSpec((self.nb,) + tuple(shape), lambda i, j: (i,) + zeros)

    def out_rows(self, width, dtype=BF16):
        nt = self.nt
        return (pl.BlockSpec((self.rows, width), lambda i, j: (i * nt + j, 0)),
                jax.ShapeDtypeStruct((self.bsz * self.t, width), dtype))


def _vec_spec(shape):
    zeros = (0,) * len(shape)
    return pl.BlockSpec(tuple(shape), lambda i, j: zeros)


def _retention(grp, p, cos_t, sin_t, h0, gn):
    c = CHUNK
    in_specs = [grp.rows_spec(GROUP, COL_RET // GROUP + n) for n in range(4)]
    tab = pl.BlockSpec((grp.rows, RET_HD), (lambda i, j: (j, 0)) if grp.carry else (lambda i, j: (0, 0)))
    in_specs += [tab, tab, _vec_spec((1, GROUP)), grp.state_spec((RET_H, RET_HD, RET_HD))]
    o_spec, o_shape = grp.out_rows(GROUP)
    return pl.pallas_call(
        functools.partial(_ret_kernel, rows=grp.rows, seg=grp.seg, carry=grp.carry),
        grid=grp.grid,
        in_specs=in_specs,
        out_specs=[o_spec, grp.state_spec((RET_H, RET_HD, RET_HD))],
        out_shape=[o_shape, jax.ShapeDtypeStruct(h0.shape, F32)],
        scratch_shapes=[pltpu.VMEM((RET_H, RET_HD, RET_HD), F32), pltpu.VMEM((c, RET_HD), F32),
                        pltpu.VMEM((c, RET_HD), F32), pltpu.VMEM((c, RET_HD), F32)],
        compiler_params=_cparams(2),
    )(p, p, p, p, cos_t, sin_t, gn.reshape(1, GROUP), h0)


def _mamba(grp, p, cprev, h0, lp):
    c = CHUNK
    pad8 = lambda a: jnp.pad(a.reshape(1, MB_H), ((0, 0), (0, 128 - MB_H)))
    in_specs = [grp.rows_spec(GROUP, COL_MB_Z // GROUP), grp.rows_spec(MB_CONV_DIM, COL_MB_XBC // MB_CONV_DIM),
                grp.rows_spec(128, COL_MB_DT // 128),
                grp.state_spec((MB_CONV - 1, MB_CONV_DIM)), grp.state_spec((MB_H, MB_N, MB_HD)),
                _vec_spec((MB_CONV, MB_CONV_DIM)), _vec_spec((1, MB_CONV_DIM)), _vec_spec((1, 128)),
                _vec_spec((1, 128)), _vec_spec((1, 128)), _vec_spec((1, GROUP))]
    o_spec, o_shape = grp.out_rows(GROUP)
    return pl.pallas_call(
        functools.partial(_mamba_kernel, rows=grp.rows, seg=grp.seg, carry=grp.carry),
        grid=grp.grid,
        in_specs=in_specs,
        out_specs=[o_spec, grp.state_spec((MB_CONV - 1, MB_CONV_DIM)), grp.state_spec((MB_H, MB_N, MB_HD))],
        out_shape=[o_shape, jax.ShapeDtypeStruct(cprev.shape, F32), jax.ShapeDtypeStruct(h0.shape, F32)],
        scratch_shapes=[pltpu.VMEM((MB_H, MB_N, MB_HD), F32),
                        pltpu.VMEM(((grp.rows if grp.carry else 0) + 8, MB_CONV_DIM), F32),
                        pltpu.VMEM((grp.rows, MB_CONV_DIM), F32),
                        pltpu.VMEM((c, MB_N), F32), pltpu.VMEM((c, MB_N), F32), pltpu.VMEM((c, MB_HD), F32),
                        pltpu.VMEM((c, MB_HD), F32), pltpu.VMEM((c, GROUP), F32),
                        pltpu.VMEM((c, 128), F32), pltpu.VMEM((c, 128), F32)],
        compiler_params=_cparams(2),
    )(p, p, p, cprev, h0, lp['mb_conv_w'], lp['mb_conv_b'].reshape(1, MB_CONV_DIM), pad8(lp['mb_dt_bias']),
      pad8(lp['mb_a_log']), pad8(lp['mb_d']), lp['mb_norm_g'].reshape(1, GROUP))


def _gla(grp, p, h0, lp, expand):
    c = CHUNK
    sub = 16 if grp.carry else grp.seg
    gk_up = jnp.pad(lp['gla_gk_up'], ((0, 128 - GLA_LORA), (0, 0)))
    in_specs = [grp.rows_spec(GLA_QK, COL_GLA_Q // GLA_QK), grp.rows_spec(GLA_QK, COL_GLA_K // GLA_QK),
                grp.rows_spec(GROUP, COL_GLA_V // GROUP), grp.rows_spec(128, COL_GLA_GK // 128),
                grp.rows_spec(GROUP, COL_GLA_G // GROUP),
                grp.state_spec((GLA_H, GLA_DK, GLA_DV)),
                _vec_spec((128, GLA_QK)), _vec_spec((1, GLA_QK)), _vec_spec((1, GROUP)), _vec_spec((GLA_QK, GROUP))]
    o_spec, o_shape = grp.out_rows(GROUP)
    return pl.pallas_call(
        functools.partial(_gla_kernel, rows=grp.rows, seg=sub, carry=grp.carry),
        grid=grp.grid,
        in_specs=in_specs,
        out_specs=[o_spec, grp.state_spec((GLA_H, GLA_DK, GLA_DV))],
        out_shape=[o_shape, jax.ShapeDtypeStruct(h0.shape, F32)],
        scratch_shapes=[pltpu.VMEM((GLA_H, GLA_DV, GLA_DK), F32)]
        + [pltpu.VMEM((c, GLA_QK), F32) for _ in range(5)]
        + [pltpu.VMEM((sub * sub, GLA_QK), F32), pltpu.VMEM((c, GROUP), F32)],
        compiler_params=_cparams(2),
    )(p, p, p, p, p, h0, gk_up, lp['gla_gk_b'].reshape(1, GLA_QK), lp['gla_norm_g'].reshape(1, GROUP), expand)


def _rwkv(grp, p, shift_prev, h0, lp):
    c = CHUNK
    sub = 16 if grp.carry else grp.seg
    row = lambda a: a.reshape(1, -1)
    w_up = jnp.pad(lp['rw_w_up'], ((0, 64), (0, 0)))
    a_up = jnp.pad(lp['rw_a_up'], ((64, 0), (0, 0)))
    in_specs = [grp.rows_spec(RW_IN, 0), grp.state_spec((1, RW_IN)), grp.state_spec((RW_H, RW_HD, RW_HD)),
                _vec_spec((1, RW_IN)), _vec_spec((1, GROUP)), _vec_spec((128, GROUP)), _vec_spec((1, GROUP)),
                _vec_spec((128, GROUP)), _vec_spec((128, GROUP))] + [_vec_spec((1, GROUP))] * 5
    o_spec, o_shape = grp.out_rows(GROUP)
    wide = lambda: pltpu.VMEM((c, GROUP), F32)
    return pl.pallas_call(
        functools.partial(_rwkv_kernel, rows=grp.rows, seg=grp.seg, sub=sub, carry=grp.carry),
        grid=grp.grid,
        in_specs=in_specs,
        out_specs=[o_spec, grp.state_spec((RW_H, RW_HD, RW_HD))],
        out_shape=[o_shape, jax.ShapeDtypeStruct(h0.shape, F32)],
        scratch_shapes=[pltpu.VMEM((RW_H, RW_HD, RW_HD), F32), pltpu.VMEM((8, RW_IN), F32),
                        pltpu.VMEM((c, RW_IN), F32)] + [wide() for _ in range(10)]
        + [pltpu.VMEM((RW_H, c, c), F32), pltpu.VMEM((RW_H, c, c), F32)],
        compiler_params=_cparams(2),
    )(p, shift_prev.reshape(-1, 1, RW_IN), h0, row(lp['rw_mu']), row(lp['rw_w0']), w_up, row(lp['rw_a0']), a_up,
      lp['rw_g_up'], row(lp['rw_k_k']), row(lp['rw_k_a']), row(lp['rw_r_k']), row(lp['rw_ln_g']),
      row(lp['rw_ln_b']))


def _rope_tables(pos0, t):
    half = RET_HD // 2
    inv = 1.0 / (ROPE_BASE ** jnp.linspace(0.0, 1.0, half, dtype=F32))
    pos = pos0 + jnp.arange(t, dtype=F32)
    ang = pos[:, None] * inv[None, :]
    cos, sin = jnp.cos(ang), jnp.sin(ang)
    cos_t = jnp.stack([cos, cos], axis=-1).reshape(t, RET_HD)
    sin_t = jnp.stack([-sin, sin], axis=-1).reshape(t, RET_HD)
    return cos_t, sin_t


def _pad_w_in(w):
    z = lambda n: jnp.zeros((D_MODEL, n), w.dtype)
    return jnp.concatenate([
        w[:, 0:1792], w[:, 3328:3336], z(120), w[:, 4360:4376], z(112), w[:, 2304:3328], w[:, 1792:2304],
        w[:, 3848:4360], w[:, 4376:4888], w[:, 3336:3592], w[:, 3592:3848], w[:, 4888:6936]], axis=1)


def _layer(grp, x, x_bf, pos_tabs, states, mem_k, mem_v, lp, w_in_pad, expand, tm):
    rw_shift, rw_state, mb_conv, mb_state, gla_state, ret_state = states
    bsz, t = grp.bsz, grp.t
    p = _matmul([x_bf], w_in_pad, F32, tm, 512)
    o_rw, rw_new = _rwkv(grp, p, rw_shift, rw_state, lp)
    o_mb, conv_new, mb_new = _mamba(grp, p, mb_conv, mb_state, lp)
    o_gl, gla_new = _gla(grp, p, gla_state, lp, expand)
    o_rt, ret_new = _retention(grp, p, pos_tabs[0], pos_tabs[1], ret_state, lp['ret_norm_g'])
    shift_new = p.reshape(bsz, t, N_PAD)[:, t - 1, 0:RW_IN]
    mix = _matmul([o_rw, o_mb, o_gl, o_rt], lp['w_out'], F32, tm, 512)
    x, x_bf = _res_ln(x, mix, lp['ln1_g'], lp['ln1_b'])
    q = _matmul([x_bf], lp['xa_wq'], BF16 if grp.carry else F32, tm, 512)
    if grp.carry:
        att = _cross_attn(q, mem_k, mem_v, bsz, t, 1, 512, BF16)
    else:
        att = _cross_attn(q, mem_k, mem_v, bsz, t, 2, t, F32)
    ao = _matmul([att], lp['xa_wo'], F32, tm, 512)
    x, x_bf = _res_ln(x, ao, lp['ln2_g'], lp['ln2_b'])
    hid = _ffn_gate_up(x_bf, lp['ffn_w_gate'], lp['ffn_w_up'], tm, 512)
    down = _matmul([hid], lp['ffn_w_down'], F32, 512, 512)
    x, x_bf = _res_ln(x, down, lp['ln3_g'], lp['ln3_b'])
    return x, x_bf, (shift_new, rw_new, conv_new, mb_new, gla_new, ret_new)


def kernel(x_prompt, x_sample, state_rwkv_shift, state_rwkv_wkv, state_mamba_conv, state_mamba_ssm, state_gla,
           state_ret, cache_mem_k, cache_mem_v, mem_prompt, w_in, w_out, ln1_g, ln1_b, rw_mu, rw_w0, rw_w_up,
           rw_a0, rw_a_up, rw_g_up, rw_k_k, rw_k_a, rw_r_k, rw_ln_g, rw_ln_b, mb_conv_w, mb_conv_b, mb_dt_bias,
           mb_a_log, mb_d, mb_norm_g, gla_gk_up, gla_gk_b, gla_norm_g, ret_norm_g, ln2_g, ln2_b, xa_wq, xa_wk,
           xa_wv, xa_wo, ln3_g, ln3_b, ffn_w_gate, ffn_w_up, ffn_w_down):
    weights = dict(
        w_out=w_out, ln1_g=ln1_g, ln1_b=ln1_b, rw_mu=rw_mu, rw_w0=rw_w0, rw_w_up=rw_w_up, rw_a0=rw_a0,
        rw_a_up=rw_a_up, rw_g_up=rw_g_up, rw_k_k=rw_k_k, rw_k_a=rw_k_a, rw_r_k=rw_r_k, rw_ln_g=rw_ln_g,
        rw_ln_b=rw_ln_b, mb_conv_w=mb_conv_w, mb_conv_b=mb_conv_b, mb_dt_bias=mb_dt_bias, mb_a_log=mb_a_log,
        mb_d=mb_d, mb_norm_g=mb_norm_g, gla_gk_up=gla_gk_up, gla_gk_b=gla_gk_b, gla_norm_g=gla_norm_g,
        ret_norm_g=ret_norm_g, ln2_g=ln2_g, ln2_b=ln2_b, xa_wq=xa_wq, xa_wo=xa_wo, ln3_g=ln3_g, ln3_b=ln3_b,
        ffn_w_gate=ffn_w_gate, ffn_w_up=ffn_w_up, ffn_w_down=ffn_w_down)
    bp, tp, _ = x_prompt.shape
    bs, ts, _ = x_sample.shape
    gp = _Group(bp, tp, 512)
    gs = _Group(bs, ts, CHUNK)
    expand = (jnp.arange(GLA_QK)[:, None] // GLA_DK == jnp.arange(GROUP)[None, :] // GLA_DV).astype(BF16)
    tabs_p = _rope_tables(0.0, tp)
    tabs_s = tuple(jnp.tile(tb, (gs.nb, 1)) for tb in _rope_tables(float(PAST_LEN), ts))
    zeros_p = (jnp.zeros((bp, RW_IN), F32), jnp.zeros((bp, RW_H, RW_HD, RW_HD), F32),
               jnp.zeros((bp, MB_CONV - 1, MB_CONV_DIM), F32), jnp.zeros((bp, MB_H, MB_N, MB_HD), F32),
               jnp.zeros((bp, GLA_H, GLA_DK, GLA_DV), F32), jnp.zeros((bp, RET_H, RET_HD, RET_HD), F32))

    yp = x_prompt.reshape(bp * tp, D_MODEL)
    ys = x_sample.reshape(bs * ts, D_MODEL)
    yp_bf, ys_bf = yp.astype(BF16), ys.astype(BF16)
    mem_bf = mem_prompt.reshape(bp * N_MEM, D_MODEL).astype(BF16)
    outs_p = [[] for _ in range(8)]
    outs_s = [[] for _ in range(6)]
    for i in range(DEPTH):
        lp = {name: val[i] for name, val in weights.items()}
        w_in_pad = _pad_w_in(w_in[i])
        mk = _matmul([mem_bf], xa_wk[i], F32, 1024, 512)
        mv = _matmul([mem_bf], xa_wv[i], F32, 1024, 512)
        yp, yp_bf, st_p = _layer(gp, yp, yp_bf, tabs_p, zeros_p, mk.reshape(bp, N_MEM, D_MODEL),
                                 mv.reshape(bp, N_MEM, D_MODEL), lp, w_in_pad, expand, 1024)
        for lst, val in zip(outs_p, st_p + (mk.reshape(bp, N_MEM, XA_H, XA_HD), mv.reshape(bp, N_MEM, XA_H, XA_HD))):
            lst.append(val)
        st_in = (state_rwkv_shift[i], state_rwkv_wkv[i], state_mamba_conv[i], state_mamba_ssm[i], state_gla[i],
                 state_ret[i])
        ys, ys_bf, st_s = _layer(gs, ys, ys_bf, tabs_s, st_in, cache_mem_k[i].reshape(bs, N_MEM, D_MODEL),
                                 cache_mem_v[i].reshape(bs, N_MEM, D_MODEL), lp, w_in_pad, expand, 1024)
        for lst, val in zip(outs_s, st_s):
            lst.append(val)
    return (yp.reshape(bp, tp, D_MODEL), ys.reshape(bs, ts, D_MODEL),
            *[jnp.stack(v) for v in outs_p], *[jnp.stack(v) for v in outs_s])
```

```python
import functools
import math

import jax
import jax.numpy as jnp
from jax import lax
from jax.experimental import pallas as pl
from jax.experimental.pallas import tpu as pltpu

F32 = jnp.float32
BF16 = jnp.bfloat16

D_MODEL = 2048
DEPTH = 2
PAST_LEN = 16384
GROUP = 512
RW_H, RW_HD = 8, 64
RW_IN = 1792
RW_LN_EPS = 64e-5
MB_H, MB_HD, MB_N, MB_G = 8, 64, 128, 2
MB_CONV = 4
MB_CONV_DIM = 1024
GLA_H, GLA_DK, GLA_DV = 4, 64, 128
GLA_QK = 256
GLA_LORA = 16
GLA_TAU = 16.0
RET_H, RET_HD = 4, 128
ROPE_BASE = 10000.0
N_MEM = 256
XA_H, XA_HD = 4, 512
D_FF = 5632
DN_ALPHA = (2 * DEPTH) ** 0.25
RET_LOG_GAMMA = tuple(math.log1p(-(2.0 ** (-5.0 - h))) for h in range(RET_H))

N_PAD = 7168
COL_MB_DT = 1792
COL_GLA_GK = 1920
COL_MB_XBC = 2048
COL_MB_Z = 3072
COL_GLA_V = 3584
COL_GLA_G = 4096
COL_GLA_Q = 4608
COL_GLA_K = 4864
COL_RET = 5120

CHUNK = 128
VMEM_LIMIT = 56 * 1024 * 1024


def _cparams(n_axes, vmem=VMEM_LIMIT):
    return pltpu.CompilerParams(dimension_semantics=("arbitrary",) * n_axes, vmem_limit_bytes=vmem)


def _dot(a, b):
    return jnp.dot(a.astype(BF16), b.astype(BF16), preferred_element_type=F32)


def _dot_nt(a, b):
    return lax.dot_general(a.astype(BF16), b.astype(BF16), (((1,), (1,)), ((), ())), preferred_element_type=F32)


def _dot_tn(a, b):
    return lax.dot_general(a.astype(BF16), b.astype(BF16), (((0,), (0,)), ((), ())), preferred_element_type=F32)


def _sigmoid(x):
    return 1.0 / (1.0 + jnp.exp(-x))


def _silu(x):
    return x * _sigmoid(x)


def _softplus(x):
    return jnp.maximum(x, 0.0) + jnp.log(1.0 + jnp.exp(-jnp.abs(x)))


def _split_dot(m_bf16, x):
    hi = x.astype(BF16)
    r1 = x - hi.astype(F32)
    mid = r1.astype(BF16)
    lo = (r1 - mid.astype(F32)).astype(BF16)
    return (jnp.dot(m_bf16, hi, preferred_element_type=F32)
            + jnp.dot(m_bf16, mid, preferred_element_type=F32)
            + jnp.dot(m_bf16, lo, preferred_element_type=F32))


def _mo(x, m):
    return x if isinstance(x, int) else pl.multiple_of(x, m)


def _seg_masks(c, seg):
    sh = jnp.int32(int(math.log2(seg)))
    r = lax.broadcasted_iota(jnp.int32, (c, c), 0)
    q = lax.broadcasted_iota(jnp.int32, (c, c), 1)
    same = lax.shift_right_arithmetic(r, sh) == lax.shift_right_arithmetic(q, sh)
    incl = jnp.logical_and(same, r >= q)
    strict = jnp.logical_and(same, r > q)
    return incl, strict, same, r, q


def _mm_kernel(*refs, k_sizes):
    n_x = len(k_sizes)
    x_refs, w_ref, o_ref, wbf = refs[:n_x], refs[n_x], refs[n_x + 1], refs[n_x + 2]

    @pl.when(pl.program_id(1) == 0)
    def _():
        wbf[...] = w_ref[...].astype(BF16)

    acc = None
    off = 0
    for xr, ks in zip(x_refs, k_sizes):
        part = jnp.dot(xr[...].astype(BF16), wbf[off:off + ks, :], preferred_element_type=F32)
        acc = part if acc is None else acc + part
        off += ks
    o_ref[...] = acc.astype(o_ref.dtype)


def _w_spec(w, tn, layer):
    k = w.shape[-2]
    if w.ndim == 2:
        return pl.BlockSpec((k, tn), lambda j, i: (0, j))
    return pl.BlockSpec((None, k, tn), lambda j, i: (layer, 0, j))


def _matmul(xs, w, out_dtype, tm, tn, layer=None):
    m = xs[0].shape[0]
    k, n = w.shape[-2:]
    k_sizes = tuple(x.shape[1] for x in xs)
    assert sum(k_sizes) == k and m % tm == 0 and n % tn == 0
    in_specs = [pl.BlockSpec((tm, ks), lambda j, i: (i, 0)) for ks in k_sizes]
    in_specs.append(_w_spec(w, tn, layer))
    return pl.pallas_call(
        functools.partial(_mm_kernel, k_sizes=k_sizes),
        grid=(n // tn, m // tm),
        in_specs=in_specs,
        out_specs=pl.BlockSpec((tm, tn), lambda j, i: (i, j)),
        out_shape=jax.ShapeDtypeStruct((m, n), out_dtype),
        scratch_shapes=[pltpu.VMEM((k, tn), BF16)],
        compiler_params=_cparams(2),
    )(*xs, w)


def _ffn_gu_kernel(x_ref, wg_ref, wu_ref, o_ref, wg_bf, wu_bf):
    @pl.when(pl.program_id(1) == 0)
    def _():
        wg_bf[...] = wg_ref[...].astype(BF16)
        wu_bf[...] = wu_ref[...].astype(BF16)

    x = x_ref[...]
    gate = jnp.dot(x, wg_bf[...], preferred_element_type=F32)
    up = jnp.dot(x, wu_bf[...], preferred_element_type=F32)
    o_ref[...] = (_silu(gate) * up).astype(o_ref.dtype)


def _ffn_gate_up(x_bf, wg, wu, tm, tn, layer):
    m, k = x_bf.shape
    n = wg.shape[-1]
    return pl.pallas_call(
        _ffn_gu_kernel,
        grid=(n // tn, m // tm),
        in_specs=[pl.BlockSpec((tm, k), lambda j, i: (i, 0)), _w_spec(wg, tn, layer), _w_spec(wu, tn, layer)],
        out_specs=pl.BlockSpec((tm, tn), lambda j, i: (i, j)),
        out_shape=jax.ShapeDtypeStruct((m, n), BF16),
        scratch_shapes=[pltpu.VMEM((k, tn), BF16), pltpu.VMEM((k, tn), BF16)],
        compiler_params=_cparams(2),
    )(x_bf, wg, wu)


def _ln_kernel(x_ref, y_ref, g_ref, b_ref, of_ref, ob_ref):
    z = DN_ALPHA * x_ref[...] + y_ref[...]
    zc = z - jnp.mean(z, axis=-1, keepdims=True)
    var = jnp.mean(zc * zc, axis=-1, keepdims=True)
    out = zc * lax.rsqrt(var + 1e-5) * g_ref[...] + b_ref[...]
    of_ref[...] = out
    ob_ref[...] = out.astype(BF16)


def _res_ln(x, y, g, b, tm=256):
    m, d = x.shape
    row = pl.BlockSpec((tm, d), lambda i: (i, 0))
    vec = pl.BlockSpec((1, d), lambda i: (0, 0))
    return pl.pallas_call(
        _ln_kernel,
        grid=(m // tm,),
        in_specs=[row, row, vec, vec],
        out_specs=[row, row],
        out_shape=[jax.ShapeDtypeStruct((m, d), F32), jax.ShapeDtypeStruct((m, d), BF16)],
        compiler_params=_cparams(1),
    )(x, y, g.reshape(1, d), b.reshape(1, d))


def _xattn_kernel(q_ref, k_ref, v_ref, o_ref, *, nb, tq):
    for j in range(nb):
        rows = slice(j * tq, (j + 1) * tq)
        for h in range(XA_H):
            cols = slice(h * XA_HD, (h + 1) * XA_HD)
            s = _dot_nt(q_ref[rows, cols], k_ref[j, :, cols]) * (XA_HD ** -0.5)
            e = jnp.exp(s - jnp.max(s, axis=-1, keepdims=True))
            pr = e / jnp.sum(e, axis=-1, keepdims=True)
            o_ref[rows, cols] = _dot(pr, v_ref[j, :, cols]).astype(o_ref.dtype)


def _xattn_cache_kernel(q_ref, k_ref, v_ref, o_ref, *, nb, tq):
    nr = N_MEM * XA_H
    r = lax.broadcasted_iota(jnp.int32, (nr, XA_H * tq), 0)
    q = lax.broadcasted_iota(jnp.int32, (nr, XA_H * tq), 1)
    own = (r & (XA_H - 1)) == lax.shift_right_arithmetic(q, jnp.int32(int(math.log2(tq))))
    for j in range(nb):
        rows = slice(j * tq, (j + 1) * tq)
        kf = k_ref[j].reshape(nr, XA_HD)
        vf = v_ref[j].reshape(nr, XA_HD)
        qcat = jnp.concatenate([q_ref[rows, h * XA_HD:(h + 1) * XA_HD] for h in range(XA_H)], axis=0)
        s = jnp.where(own, _dot_nt(kf, qcat) * (XA_HD ** -0.5), -jnp.inf)
        e = jnp.exp(s - jnp.max(s, axis=0, keepdims=True))
        pr = e / jnp.sum(e, axis=0, keepdims=True)
        o = _dot_tn(pr, vf)
        for h in range(XA_H):
            o_ref[rows, h * XA_HD:(h + 1) * XA_HD] = o[h * tq:(h + 1) * tq, :].astype(o_ref.dtype)


def _cross_attn(q, mem_k, mem_v, bsz, t, nb, tq, out_dtype, layer=None):
    nt = t // tq
    rows = nb * tq
    if layer is None:
        body = _xattn_kernel
        kv_spec = pl.BlockSpec((nb, N_MEM, D_MODEL), lambda i, j: (i, 0, 0))
    else:
        body = _xattn_cache_kernel
        kv_spec = pl.BlockSpec((None, nb, N_MEM, XA_H, XA_HD), lambda i, j: (layer, i, 0, 0, 0))
    return pl.pallas_call(
        functools.partial(body, nb=nb, tq=tq),
        grid=(bsz // nb, nt),
        in_specs=[pl.BlockSpec((rows, D_MODEL), lambda i, j: (i * nt + j, 0)), kv_spec, kv_spec],
        out_specs=pl.BlockSpec((rows, D_MODEL), lambda i, j: (i * nt + j, 0)),
        out_shape=jax.ShapeDtypeStruct((bsz * t, D_MODEL), out_dtype),
        compiler_params=_cparams(2),
    )(q, mem_k, mem_v)


def _state_io(carry, h0_ref, hout_ref, hst):
    if carry:
        return (lambda s, h: hst[h]), (lambda s, h, val: hst.__setitem__(h, val))
    return (lambda s, h: h0_ref[s, h]), (lambda s, h, val: hout_ref.__setitem__((s, h), val))


def _for_segments(nseg, fn, unroll=True):
    if nseg == 1:
        fn(0)
    else:
        def body(s, c):
            fn(s)
            return c
        lax.fori_loop(0, nseg, body, 0, unroll=unroll)


def _ret_kernel(q_ref, k_ref, v_ref, g_ref, cos_ref, sin_ref, gn_ref, h0_ref, o_ref, hout_ref,
                hst, qs_sc, ks_sc, y_sc, *, rows, seg, carry):
    c = CHUNK
    nseg = c // seg
    if carry:
        @pl.when(pl.program_id(1) == 0)
        def _():
            hst[...] = h0_ref[0]
    get_h, set_h = _state_io(carry, h0_ref, hout_ref, hst)
    incl, _, _, r_i, c_i = _seg_masks(c, seg)
    dpos = (r_i - c_i).astype(F32)
    tau = (lax.broadcasted_iota(jnp.int32, (c, RET_HD), 0) & (seg - 1)).astype(F32)
    even = (lax.broadcasted_iota(jnp.int32, (c, GROUP), 1) & 1) == 0

    for ci in range(rows // c):
        rws = slice(ci * c, (ci + 1) * c)
        cosb = jnp.concatenate([cos_ref[rws, :]] * RET_H, axis=1)
        sinb = jnp.concatenate([sin_ref[rws, :]] * RET_H, axis=1)

        def rot(x):
            swapped = jnp.where(even, pltpu.roll(x, GROUP - 1, 1), pltpu.roll(x, 1, 1))
            return x * cosb + swapped * sinb

        qr = rot(q_ref[rws, :])
        kr = rot(k_ref[rws, :]) * (RET_HD ** -0.5)
        for h in range(RET_H):
            lgam = RET_LOG_GAMMA[h]
            cols = slice(h * RET_HD, (h + 1) * RET_HD)
            qh, kh = qr[:, cols], kr[:, cols]
            dm = jnp.where(incl, jnp.exp(dpos * lgam), 0.0)
            y_sc[...] = _dot(_dot_nt(qh, kh) * dm, v_ref[rws, cols])
            qs_sc[...] = qh * jnp.exp((tau + 1.0) * lgam)
            ks_sc[...] = kh * jnp.exp((seg - 1.0 - tau) * lgam)
            cd = math.exp(seg * lgam)

            def seg_step(s, h=h, cols=cols, cd=cd, ci=ci):
                sr = pl.ds(_mo(s * seg, seg), seg)
                vr = pl.ds(_mo(ci * c + s * seg, seg), seg)
                hs = get_h(s, h)
                y_sc[sr, :] += _dot(qs_sc[sr, :], hs)
                set_h(s, h, cd * hs + _dot_tn(ks_sc[sr, :], v_ref[vr, cols]))

            _for_segments(nseg, seg_step)
            y = y_sc[...]
            yn = y * lax.rsqrt(jnp.mean(y * y, axis=-1, keepdims=True) + 1e-5) * gn_ref[:, cols]
            o_ref[rws, cols] = (yn * _silu(g_ref[rws, cols])).astype(o_ref.dtype)

    if carry:
        @pl.when(pl.program_id(1) == pl.num_programs(1) - 1)
        def _():
            hout_ref[0] = hst[...]


def _mamba_kernel(z_ref, xbc_ref, dt_ref, cprev_ref, h0_ref, cw_ref, cb_ref, dtb_ref, alog_ref, dd_ref,
                  ng_ref, o_ref, cout_ref, hout_ref,
                  hst, xpad, act_sc, qs_sc, ks_sc, xs_sc, y_sc, yf_sc, ecum_sc, etot_sc, *, rows, seg, carry):
    c = CHUNK
    nseg = c // seg
    get_h, set_h = _state_io(carry, h0_ref, hout_ref, hst)

    def conv_act(window, n):
        acc = cb_ref[...] + window[5:5 + n] * cw_ref[0:1, :]
        for i in range(1, MB_CONV):
            acc = acc + window[5 + i:5 + i + n] * cw_ref[i:i + 1, :]
        return _silu(acc)

    if carry:
        @pl.when(pl.program_id(1) == 0)
        def _():
            hst[...] = h0_ref[0]
            xpad[0:8, :] = jnp.zeros((8, MB_CONV_DIM), F32)
            xpad[5:8, :] = cprev_ref[0]

        xpad[8:8 + rows, :] = xbc_ref[...]
        for ci in range(rows // c):
            acc = cb_ref[...] + xpad[ci * c + 5:ci * c + 5 + c, :] * cw_ref[0:1, :]
            for i in range(1, MB_CONV):
                acc = acc + xpad[ci * c + 5 + i:ci * c + 5 + i + c, :] * cw_ref[i:i + 1, :]
            act_sc[ci * c:(ci + 1) * c, :] = _silu(acc)
        xpad[0:8, :] = xpad[rows:rows + 8, :]

        @pl.when(pl.program_id(1) == pl.num_programs(1) - 1)
        def _():
            cout_ref[0] = xpad[5:8, :]
    else:
        def conv_seq(s, carry_):
            sr = pl.ds(pl.multiple_of(s * seg, seg), seg)
            xpad[5:8, :] = cprev_ref[s]
            xs = xbc_ref[sr, :]
            window = jnp.concatenate([xpad[0:8, :], xs], axis=0)
            act_sc[sr, :] = conv_act(window, seg)
            cout_ref[s] = xs[seg - 3:seg]
            return carry_

        xpad[0:8, :] = jnp.zeros((8, MB_CONV_DIM), F32)
        lax.fori_loop(0, rows // seg, conv_seq, 0)

    incl, _, same, _, _ = _seg_masks(c, seg)
    lt_bf = jnp.where(incl, 1.0, 0.0).astype(BF16)
    same_bf = jnp.where(same, 1.0, 0.0).astype(BF16)
    a_neg = -jnp.exp(alog_ref[...])

    for ci in range(rows // c):
        rws = slice(ci * c, (ci + 1) * c)
        dtv = _softplus(dt_ref[rws, :] + dtb_ref[...])
        la = dtv * a_neg
        cum = _split_dot(lt_bf, la)
        tot = _split_dot(same_bf, la)
        cum_t = cum.T
        ecum_sc[...] = jnp.exp(cum)
        etot_sc[...] = jnp.exp(tot)
        edec = jnp.exp(tot - cum)
        for g in range(MB_G):
            cg = act_sc[rws, 768 + g * MB_N:768 + (g + 1) * MB_N]
            bg = act_sc[rws, 512 + g * MB_N:512 + (g + 1) * MB_N]
            gmat = _dot_nt(cg, bg)
            qs_sc[...] = cg
            ks_sc[...] = bg
            for hh in range(MB_H // MB_G):
                h = g * (MB_H // MB_G) + hh
                cols = slice(h * MB_HD, (h + 1) * MB_HD)
                lmat = jnp.exp(jnp.where(incl, cum[:, h:h + 1] - cum_t[h:h + 1, :], -jnp.inf))
                xh = act_sc[rws, cols]
                xdt = xh * dtv[:, h:h + 1]
                y_sc[...] = _dot(gmat * lmat, xdt)
                xs_sc[...] = xdt * edec[:, h:h + 1]

                def seg_step(s, h=h):
                    sr = pl.ds(_mo(s * seg, seg), seg)
                    first = pl.ds(_mo(s * seg, seg), 1)
                    hs = get_h(s, h)
                    y_sc[sr, :] += _dot(qs_sc[sr, :], hs) * ecum_sc[sr, h:h + 1]
                    set_h(s, h, etot_sc[first, h:h + 1] * hs + _dot_tn(ks_sc[sr, :], xs_sc[sr, :]))

                _for_segments(nseg, seg_step)
                yf_sc[:, cols] = y_sc[...] + dd_ref[:, h:h + 1] * xh
        yz = yf_sc[...] * _silu(z_ref[rws, :])
        gw = GROUP // MB_G
        for g in range(MB_G):
            cols = slice(g * gw, (g + 1) * gw)
            part = yz[:, cols]
            nrm = part * lax.rsqrt(jnp.mean(part * part, axis=-1, keepdims=True) + 1e-5) * ng_ref[:, cols]
            o_ref[rws, cols] = nrm.astype(o_ref.dtype)

    if carry:
        @pl.when(pl.program_id(1) == pl.num_programs(1) - 1)
        def _():
            hout_ref[0] = hst[...]


def _gla_kernel(q_ref, k_ref, v_ref, gk_ref, g_ref, h0_ref, gkup_ref, gkb_ref, ng_ref, e_ref,
                o_ref, hout_ref,
                hst, cum_sc, q_sc, qe_sc, ke_sc, etot_sc, p_sc, y_sc, *, rows, seg, carry):
    c = CHUNK
    nsub = c // seg
    if carry:
        @pl.when(pl.program_id(1) == 0)
        def _():
            for h in range(GLA_H):
                hst[h] = h0_ref[0, h].T
    incl, _, same, _, _ = _seg_masks(c, seg)
    lt_bf = jnp.where(incl, 1.0, 0.0).astype(BF16)
    same_bf = jnp.where(same, 1.0, 0.0).astype(BF16)
    row_i = lax.broadcasted_iota(jnp.int32, (seg, GLA_QK), 0)

    for ci in range(rows // c):
        rws = slice(ci * c, (ci + 1) * c)
        pre = _dot(gk_ref[rws, :], gkup_ref[...]) + gkb_ref[...]
        la = -_softplus(-pre) * (1.0 / GLA_TAU)
        cum = _split_dot(lt_bf, la)
        tot = _split_dot(same_bf, la)
        qv = q_ref[rws, :] * (GLA_DK ** -0.5)
        cum_sc[...] = cum
        q_sc[...] = qv
        qe_sc[...] = qv * jnp.exp(cum)
        ke_sc[...] = k_ref[rws, :] * jnp.exp(tot - cum)
        etot_sc[...] = jnp.exp(tot)

        def sub_step(u, ci=ci):
            base = pl.multiple_of(u * seg, seg)
            sr = pl.ds(base, seg)
            gbase = ci * c + base
            cu = cum_sc[sr, :]
            qu = q_sc[sr, :]
            for j in range(seg):
                cj = cum_sc[pl.ds(base + j, 1), :]
                kj = k_ref[pl.ds(gbase + j, 1), :]
                pj = qu * jnp.exp(jnp.where(row_i >= j, cu - cj, -jnp.inf)) * kj
                p_sc[j * seg:(j + 1) * seg, :] = pj
            rm = _dot(p_sc[...], e_ref[...])
            y = rm[0:seg, :] * v_ref[pl.ds(gbase, 1), :]
            for j in range(1, seg):
                y = y + rm[j * seg:(j + 1) * seg, :] * v_ref[pl.ds(gbase + j, 1), :]
            first = pl.ds(base, 1)
            for h in range(GLA_H):
                kc = slice(h * GLA_DK, (h + 1) * GLA_DK)
                vc = slice(h * GLA_DV, (h + 1) * GLA_DV)
                ht = hst[h] if carry else h0_ref[u, h].T
                y_sc[sr, vc] = y[:, vc] + _dot_nt(qe_sc[sr, kc], ht)
                new = ht * etot_sc[first, kc] + _dot_tn(v_ref[pl.ds(gbase, seg), vc], ke_sc[sr, kc])
                if carry:
                    hst[h] = new
                else:
                    hout_ref[u, h] = new.T

        _for_segments(nsub, sub_step, unroll=False)
        for h in range(GLA_H):
            vc = slice(h * GLA_DV, (h + 1) * GLA_DV)
            y = y_sc[:, vc]
            yn = y * lax.rsqrt(jnp.mean(y * y, axis=-1, keepdims=True) + 1e-5) * ng_ref[:, vc]
            o_ref[rws, vc] = (yn * _silu(g_ref[rws, vc])).astype(o_ref.dtype)

    if carry:
        @pl.when(pl.program_id(1) == pl.num_programs(1) - 1)
        def _():
            for h in range(GLA_H):
                hout_ref[0, h] = hst[h].T


def _rwkv_kernel(p_ref, sp_ref, h0_ref, mu_ref, w0_ref, wup_ref, a0_ref, aup_ref, gup_ref, kk_ref, ka_ref,
                 rk_ref, lng_ref, lnb_ref, o_ref, hout_ref,
                 hst, last_sc, prev_sc, w_sc, y_sc, bt_sc, kt_sc, v_sc, qh_sc, z_sc, rkb_sc, etot_sc, out_sc, kap_sc,
                 pw_sc, t_sc, ak_sc, pb_sc, pk_sc, *, rows, seg, sub, carry):
    c = CHUNK
    assert rows == c
    nsub = c // sub
    n_iter = int(math.log2(sub)) - 1
    if carry:
        @pl.when(pl.program_id(1) == 0)
        def _():
            hst[...] = h0_ref[0]
            last_sc[...] = jnp.broadcast_to(sp_ref[0], (8, RW_IN))

        prev_row = jnp.broadcast_to(last_sc[0:1, :], (c, RW_IN))
    else:
        for s in range(rows // seg):
            prev_sc[s * seg:(s + 1) * seg, :] = jnp.broadcast_to(sp_ref[s], (seg, RW_IN))
        prev_row = prev_sc[...]
    incl, strict, same, r_i, c_i = _seg_masks(c, sub)
    lt_bf = jnp.where(incl, 1.0, 0.0).astype(BF16)
    same_bf = jnp.where(same, 1.0, 0.0).astype(BF16)
    eye = jnp.where(r_i == c_i, 1.0, 0.0)
    row_w = lax.broadcasted_iota(jnp.int32, (c, RW_IN), 0)
    first_row = (row_w & ((c if carry else seg) - 1)) == 0

    p = p_ref[...]
    prev = jnp.where(first_row, prev_row, pltpu.roll(p, 1, 0))
    xs = p + (prev - p) * mu_ref[...]
    r = xs[:, 0:GROUP]
    k = xs[:, GROUP:2 * GROUP]
    v = xs[:, 2 * GROUP:3 * GROUP]
    wa = xs[:, 3 * GROUP:3 * GROUP + 128]
    gd = xs[:, 3 * GROUP + 128:RW_IN]
    w = -_softplus(-(w0_ref[...] + _dot(jnp.tanh(wa), wup_ref[...]))) - 0.5
    ld = -jnp.exp(w)
    a = _sigmoid(a0_ref[...] + _dot(wa, aup_ref[...]))
    gate = _dot(_sigmoid(gd), gup_ref[...])
    kkr = k * kk_ref[...]
    k2 = k * (1.0 + (a - 1.0) * ka_ref[...])
    cum = _split_dot(lt_bf, ld)
    tot = _split_dot(same_bf, ld)
    ecum = jnp.exp(cum)
    einv = jnp.exp(-cum)
    eprev = jnp.exp(cum - ld)
    rkb_sc[...] = r * k2 * rk_ref[...]
    etot_sc[...] = jnp.exp(tot)
    v_sc[...] = v
    rtil = r * ecum
    kt_sc[...] = k2 * einv
    braw = kkr * a * einv
    kraw = kkr * eprev

    heads = [slice(h * RW_HD, (h + 1) * RW_HD) for h in range(RW_H)]
    for h, cols in enumerate(heads):
        kkh = kkr[:, cols]
        rn = lax.rsqrt(jnp.maximum(jnp.sum(kkh * kkh, axis=-1, keepdims=True), 1e-24))
        kap = kraw[:, cols] * rn
        bt = braw[:, cols] * rn
        kap_sc[:, cols] = kap
        bt_sc[:, cols] = bt
        m1 = _dot_nt(jnp.concatenate([kap, rtil[:, cols]], axis=0), jnp.concatenate([bt, kt_sc[:, cols]], axis=0))
        x = jnp.where(strict, -m1[0:c, 0:c], 0.0)
        pw_sc[0, h] = x
        t_sc[h] = eye + x
        ak_sc[h] = jnp.where(strict, m1[0:c, c:2 * c], 0.0)
        pb_sc[h] = jnp.where(incl, m1[c:2 * c, 0:c], 0.0)
        pk_sc[h] = jnp.where(incl, m1[c:2 * c, c:2 * c], 0.0)
    for h, cols in enumerate(heads):
        vh = v_sc[:, cols]
        y_sc[:, cols] = _dot(ak_sc[h], vh)
        z_sc[:, cols] = _dot(pk_sc[h], vh)
    for it in range(n_iter):
        src, dst = it % 2, (it + 1) % 2
        for h in range(RW_H):
            pw = pw_sc[src, h]
            pw_sc[dst, h] = _dot(pw, pw)
        for h in range(RW_H):
            tm = t_sc[h]
            t_sc[h] = tm + _dot(tm, pw_sc[dst, h])
    for h, cols in enumerate(heads):
        wy = _dot(t_sc[h], jnp.concatenate([kap_sc[:, cols], y_sc[:, cols]], axis=1))
        w_sc[:, cols] = wy[:, 0:RW_HD]
        y_sc[:, cols] = wy[:, RW_HD:2 * RW_HD]
    for h, cols in enumerate(heads):
        pbwy = _dot(pb_sc[h], jnp.concatenate([w_sc[:, cols], y_sc[:, cols]], axis=1))
        qh_sc[:, cols] = rtil[:, cols] - pbwy[:, 0:RW_HD]
        z_sc[:, cols] = z_sc[:, cols] - pbwy[:, RW_HD:2 * RW_HD]

    states = [hst[h] for h in range(RW_H)] if carry else None
    for s in range(nsub):
        sr = slice(s * sub, (s + 1) * sub)
        for h, cols in enumerate(heads):
            gb = _dot_tn(jnp.concatenate([w_sc[sr, cols], y_sc[sr, cols], v_sc[sr, cols]], axis=1),
                         jnp.concatenate([bt_sc[sr, cols], kt_sc[sr, cols]], axis=1))
            wtb = gb[0:RW_HD, 0:RW_HD]
            nmat = gb[2 * RW_HD:3 * RW_HD, RW_HD:2 * RW_HD] - gb[RW_HD:2 * RW_HD, 0:RW_HD]
            st = states[h] if carry else h0_ref[s, h]
            out_sc[sr, cols] = _dot_nt(qh_sc[sr, cols], st) + z_sc[sr, cols]
            new = (st - _dot(st, wtb) + nmat) * etot_sc[s * sub:s * sub + 1, cols]
            if carry:
                states[h] = new
            else:
                hout_ref[s, h] = new
    if carry:
        for h in range(RW_H):
            hst[h] = states[h]

    for h, cols in enumerate(heads):
        oh = out_sc[:, cols]
        oc = oh - jnp.mean(oh, axis=-1, keepdims=True)
        var = jnp.mean(oc * oc, axis=-1, keepdims=True)
        on = oc * lax.rsqrt(var + RW_LN_EPS) * lng_ref[:, cols] + lnb_ref[:, cols]
        out_sc[:, cols] = on + jnp.sum(rkb_sc[:, cols], axis=-1, keepdims=True) * v_sc[:, cols]
    o_ref[...] = (out_sc[...] * gate).astype(o_ref.dtype)

    if carry:
        last_sc[...] = jnp.broadcast_to(p_ref[rows - 1:rows, :], (8, RW_IN))

        @pl.when(pl.program_id(1) == pl.num_programs(1) - 1)
        def _():
            hout_ref[0] = hst[...]


class _Group:
    def __init__(self, bsz, t, rows):
        self.bsz, self.t, self.rows = bsz, t, rows
        self.carry = t >= CHUNK
        self.nb = 1 if self.carry else rows // t
        self.nt = t // rows if self.carry else 1
        self.grid = (bsz // self.nb, self.nt)
        self.seg = CHUNK if self.carry else t

    def rows_spec(self, width, col_block):
        nt = self.nt
        return pl.BlockSpec((self.rows, width), lambda i, j: (i * nt + j, col_block))

    def state_spec(self, shape, layer=None):
        zeros = (0,) * len(shape)
        if layer is None:
            return pl.BlockSpec((self.nb,) + tuple(shape), lambda i, j: (i,) + zeros)
        return pl.BlockSpec((None, self.nb) + tuple(shape), lambda i, j: (layer, i) + zeros)

    def out_rows(self, width, dtype=BF16):
        nt = self.nt
        return (pl.BlockSpec((self.rows, width), lambda i, j: (i * nt + j, 0)),
                jax.ShapeDtypeStruct((self.bsz * self.t, width), dtype))


def _vec_spec(shape):
    zeros = (0,) * len(shape)
    return pl.BlockSpec(tuple(shape), lambda i, j: zeros)


def _retention(grp, p, cos_t, sin_t, h0, sl, gn):
    c = CHUNK
    shape = (RET_H, RET_HD, RET_HD)
    in_specs = [grp.rows_spec(GROUP, COL_RET // GROUP + n) for n in range(4)]
    tab = pl.BlockSpec((grp.rows, RET_HD), (lambda i, j: (j, 0)) if grp.carry else (lambda i, j: (0, 0)))
    in_specs += [tab, tab, _vec_spec((1, GROUP)), grp.state_spec(shape, sl)]
    o_spec, o_shape = grp.out_rows(GROUP)
    return pl.pallas_call(
        functools.partial(_ret_kernel, rows=grp.rows, seg=grp.seg, carry=grp.carry),
        grid=grp.grid,
        in_specs=in_specs,
        out_specs=[o_spec, grp.state_spec(shape)],
        out_shape=[o_shape, jax.ShapeDtypeStruct(h0.shape[1:], F32)],
        scratch_shapes=[pltpu.VMEM(shape, F32), pltpu.VMEM((c, RET_HD), F32),
                        pltpu.VMEM((c, RET_HD), F32), pltpu.VMEM((c, RET_HD), F32)],
        compiler_params=_cparams(2),
    )(p, p, p, p, cos_t, sin_t, gn.reshape(1, GROUP), h0)


def _mamba(grp, p, cprev, h0, sl, lp):
    c = CHUNK
    cshape, hshape = (MB_CONV - 1, MB_CONV_DIM), (MB_H, MB_N, MB_HD)
    pad8 = lambda a: jnp.pad(a.reshape(1, MB_H), ((0, 0), (0, 128 - MB_H)))
    in_specs = [grp.rows_spec(GROUP, COL_MB_Z // GROUP), grp.rows_spec(MB_CONV_DIM, COL_MB_XBC // MB_CONV_DIM),
                grp.rows_spec(128, COL_MB_DT // 128),
                grp.state_spec(cshape, sl), grp.state_spec(hshape, sl),
                _vec_spec((MB_CONV, MB_CONV_DIM)), _vec_spec((1, MB_CONV_DIM)), _vec_spec((1, 128)),
                _vec_spec((1, 128)), _vec_spec((1, 128)), _vec_spec((1, GROUP))]
    o_spec, o_shape = grp.out_rows(GROUP)
    return pl.pallas_call(
        functools.partial(_mamba_kernel, rows=grp.rows, seg=grp.seg, carry=grp.carry),
        grid=grp.grid,
        in_specs=in_specs,
        out_specs=[o_spec, grp.state_spec(cshape), grp.state_spec(hshape)],
        out_shape=[o_shape, jax.ShapeDtypeStruct(cprev.shape[1:], F32), jax.ShapeDtypeStruct(h0.shape[1:], F32)],
        scratch_shapes=[pltpu.VMEM(hshape, F32),
                        pltpu.VMEM(((grp.rows if grp.carry else 0) + 8, MB_CONV_DIM), F32),
                        pltpu.VMEM((grp.rows, MB_CONV_DIM), F32),
                        pltpu.VMEM((c, MB_N), F32), pltpu.VMEM((c, MB_N), F32), pltpu.VMEM((c, MB_HD), F32),
                        pltpu.VMEM((c, MB_HD), F32), pltpu.VMEM((c, GROUP), F32),
                        pltpu.VMEM((c, 128), F32), pltpu.VMEM((c, 128), F32)],
        compiler_params=_cparams(2),
    )(p, p, p, cprev, h0, lp['mb_conv_w'], lp['mb_conv_b'].reshape(1, MB_CONV_DIM), pad8(lp['mb_dt_bias']),
      pad8(lp['mb_a_log']), pad8(lp['mb_d']), lp['mb_norm_g'].reshape(1, GROUP))


def _gla(grp, p, h0, sl, lp, expand):
    c = CHUNK
    shape = (GLA_H, GLA_DK, GLA_DV)
    sub = 16 if grp.carry else grp.seg
    gk_up = jnp.pad(lp['gla_gk_up'], ((0, 128 - GLA_LORA), (0, 0)))
    in_specs = [grp.rows_spec(GLA_QK, COL_GLA_Q // GLA_QK), grp.rows_spec(GLA_QK, COL_GLA_K // GLA_QK),
                grp.rows_spec(GROUP, COL_GLA_V // GROUP), grp.rows_spec(128, COL_GLA_GK // 128),
                grp.rows_spec(GROUP, COL_GLA_G // GROUP), grp.state_spec(shape, sl),
                _vec_spec((128, GLA_QK)), _vec_spec((1, GLA_QK)), _vec_spec((1, GROUP)), _vec_spec((GLA_QK, GROUP))]
    o_spec, o_shape = grp.out_rows(GROUP)
    return pl.pallas_call(
        functools.partial(_gla_kernel, rows=grp.rows, seg=sub, carry=grp.carry),
        grid=grp.grid,
        in_specs=in_specs,
        out_specs=[o_spec, grp.state_spec(shape)],
        out_shape=[o_shape, jax.ShapeDtypeStruct(h0.shape[1:], F32)],
        scratch_shapes=[pltpu.VMEM((GLA_H, GLA_DV, GLA_DK), F32)]
        + [pltpu.VMEM((c, GLA_QK), F32) for _ in range(5)]
        + [pltpu.VMEM((sub * sub, GLA_QK), F32), pltpu.VMEM((c, GROUP), F32)],
        compiler_params=_cparams(2),
    )(p, p, p, p, p, h0, gk_up, lp['gla_gk_b'].reshape(1, GLA_QK), lp['gla_norm_g'].reshape(1, GROUP), expand)


def _rwkv(grp, p, shift_prev, h0, sl, lp):
    c = CHUNK
    shape = (RW_H, RW_HD, RW_HD)
    sub = 64 if grp.carry else grp.seg
    row = lambda a: a.reshape(1, -1)
    w_up = jnp.pad(lp['rw_w_up'], ((0, 64), (0, 0)))
    a_up = jnp.pad(lp['rw_a_up'], ((64, 0), (0, 0)))
    in_specs = [grp.rows_spec(RW_IN, 0), grp.state_spec((1, RW_IN), sl), grp.state_spec(shape, sl),
                _vec_spec((1, RW_IN)), _vec_spec((1, GROUP)), _vec_spec((128, GROUP)), _vec_spec((1, GROUP)),
                _vec_spec((128, GROUP)), _vec_spec((128, GROUP))] + [_vec_spec((1, GROUP))] * 5
    o_spec, o_shape = grp.out_rows(GROUP)
    wide = lambda: pltpu.VMEM((c, GROUP), F32)
    return pl.pallas_call(
        functools.partial(_rwkv_kernel, rows=grp.rows, seg=grp.seg, sub=sub, carry=grp.carry),
        grid=grp.grid,
        in_specs=in_specs,
        out_specs=[o_spec, grp.state_spec(shape)],
        out_shape=[o_shape, jax.ShapeDtypeStruct(h0.shape[1:], F32)],
        scratch_shapes=[pltpu.VMEM(shape, F32), pltpu.VMEM((8, RW_IN), F32),
                        pltpu.VMEM((c, RW_IN), F32)] + [wide() for _ in range(11)]
        + [pltpu.VMEM((2, RW_H, c, c), F32)] + [pltpu.VMEM((RW_H, c, c), F32) for _ in range(4)],
        compiler_params=_cparams(2),
    )(p, shift_prev.reshape(shift_prev.shape[0], -1, 1, RW_IN), h0, row(lp['rw_mu']), row(lp['rw_w0']), w_up,
      row(lp['rw_a0']), a_up, lp['rw_g_up'], row(lp['rw_k_k']), row(lp['rw_k_a']), row(lp['rw_r_k']),
      row(lp['rw_ln_g']), row(lp['rw_ln_b']))


def _rope_tables(pos0, t):
    half = RET_HD // 2
    inv = 1.0 / (ROPE_BASE ** jnp.linspace(0.0, 1.0, half, dtype=F32))
    pos = pos0 + jnp.arange(t, dtype=F32)
    ang = pos[:, None] * inv[None, :]
    cos, sin = jnp.cos(ang), jnp.sin(ang)
    cos_t = jnp.stack([cos, cos], axis=-1).reshape(t, RET_HD)
    sin_t = jnp.stack([-sin, sin], axis=-1).reshape(t, RET_HD)
    return cos_t, sin_t


def _pad_w_in(w):
    z = lambda n: jnp.zeros(w.shape[:-1] + (n,), w.dtype)
    return jnp.concatenate([
        w[..., 0:1792], w[..., 3328:3336], z(120), w[..., 4360:4376], z(112), w[..., 2304:3328],
        w[..., 1792:2304], w[..., 3848:4360], w[..., 4376:4888], w[..., 3336:3592], w[..., 3592:3848],
        w[..., 4888:6936]], axis=-1)


def _layer(grp, grp_rw, x, x_bf, pos_tabs, states, sl, mem_k, mem_v, cache_layer, lp, big, wl, expand, tm):
    rw_shift, rw_state, mb_conv, mb_state, gla_state, ret_state = states
    bsz, t = grp.bsz, grp.t
    p = _matmul([x_bf], big['w_in'], F32, tm, 1024, wl)
    o_rw, rw_new = _rwkv(grp_rw, p, rw_shift, rw_state, sl, lp)
    o_mb, conv_new, mb_new = _mamba(grp, p, mb_conv, mb_state, sl, lp)
    o_gl, gla_new = _gla(grp, p, gla_state, sl, lp, expand)
    o_rt, ret_new = _retention(grp, p, pos_tabs[0], pos_tabs[1], ret_state, sl, lp['ret_norm_g'])
    shift_new = p.reshape(bsz, t, N_PAD)[:, t - 1, 0:RW_IN]
    mix = _matmul([o_rw, o_mb, o_gl, o_rt], big['w_out'], F32, tm, 1024, wl)
    x, x_bf = _res_ln(x, mix, lp['ln1_g'], lp['ln1_b'])
    q = _matmul([x_bf], big['xa_wq'], BF16 if grp.carry else F32, tm, 1024, wl)
    if grp.carry:
        att = _cross_attn(q, mem_k, mem_v, bsz, t, 1, 512, BF16)
    else:
        att = _cross_attn(q, mem_k, mem_v, bsz, t, 2, t, F32, layer=cache_layer)
    ao = _matmul([att], big['xa_wo'], F32, tm, 1024, wl)
    x, x_bf = _res_ln(x, ao, lp['ln2_g'], lp['ln2_b'])
    hid = _ffn_gate_up(x_bf, big['ffn_w_gate'], big['ffn_w_up'], tm, 512, wl)
    down = _matmul([hid], big['ffn_w_down'], F32, 512, 512, wl)
    x, x_bf = _res_ln(x, down, lp['ln3_g'], lp['ln3_b'])
    return x, x_bf, (shift_new, rw_new, conv_new, mb_new, gla_new, ret_new)


def kernel(x_prompt, x_sample, state_rwkv_shift, state_rwkv_wkv, state_mamba_conv, state_mamba_ssm, state_gla,
           state_ret, cache_mem_k, cache_mem_v, mem_prompt, w_in, w_out, ln1_g, ln1_b, rw_mu, rw_w0, rw_w_up,
           rw_a0, rw_a_up, rw_g_up, rw_k_k, rw_k_a, rw_r_k, rw_ln_g, rw_ln_b, mb_conv_w, mb_conv_b, mb_dt_bias,
           mb_a_log, mb_d, mb_norm_g, gla_gk_up, gla_gk_b, gla_norm_g, ret_norm_g, ln2_g, ln2_b, xa_wq, xa_wk,
           xa_wv, xa_wo, ln3_g, ln3_b, ffn_w_gate, ffn_w_up, ffn_w_down):
    small = dict(
        ln1_g=ln1_g, ln1_b=ln1_b, rw_mu=rw_mu, rw_w0=rw_w0, rw_w_up=rw_w_up, rw_a0=rw_a0,
        rw_a_up=rw_a_up, rw_g_up=rw_g_up, rw_k_k=rw_k_k, rw_k_a=rw_k_a, rw_r_k=rw_r_k, rw_ln_g=rw_ln_g,
        rw_ln_b=rw_ln_b, mb_conv_w=mb_conv_w, mb_conv_b=mb_conv_b, mb_dt_bias=mb_dt_bias, mb_a_log=mb_a_log,
        mb_d=mb_d, mb_norm_g=mb_norm_g, gla_gk_up=gla_gk_up, gla_gk_b=gla_gk_b, gla_norm_g=gla_norm_g,
        ret_norm_g=ret_norm_g, ln2_g=ln2_g, ln2_b=ln2_b, ln3_g=ln3_g, ln3_b=ln3_b)
    big = dict(w_in=_pad_w_in(w_in), w_out=w_out, xa_wq=xa_wq, xa_wo=xa_wo, ffn_w_gate=ffn_w_gate,
               ffn_w_up=ffn_w_up, ffn_w_down=ffn_w_down)
    bp, tp, _ = x_prompt.shape
    bs, ts, _ = x_sample.shape
    gp = _Group(bp, tp, 512)
    gp_rw = _Group(bp, tp, CHUNK)
    gs = _Group(bs, ts, CHUNK)
    expand = (jnp.arange(GLA_QK)[:, None] // GLA_DK == jnp.arange(GROUP)[None, :] // GLA_DV).astype(BF16)
    tabs_p = _rope_tables(0.0, tp)
    tabs_s = tuple(jnp.tile(tb, (gs.nb, 1)) for tb in _rope_tables(float(PAST_LEN), ts))
    zeros_p = (jnp.zeros((1, bp, RW_IN), F32), jnp.zeros((1, bp, RW_H, RW_HD, RW_HD), F32),
               jnp.zeros((1, bp, MB_CONV - 1, MB_CONV_DIM), F32), jnp.zeros((1, bp, MB_H, MB_N, MB_HD), F32),
               jnp.zeros((1, bp, GLA_H, GLA_DK, GLA_DV), F32), jnp.zeros((1, bp, RET_H, RET_HD, RET_HD), F32))
    st_s_in = (state_rwkv_shift, state_rwkv_wkv, state_mamba_conv, state_mamba_ssm, state_gla, state_ret)

    yp = x_prompt.reshape(bp * tp, D_MODEL)
    ys = x_sample.reshape(bs * ts, D_MODEL)
    yp_bf, ys_bf = yp.astype(BF16), ys.astype(BF16)
    mem_bf = mem_prompt.reshape(bp * N_MEM, D_MODEL).astype(BF16)
    outs_p = [[] for _ in range(8)]
    outs_s = [[] for _ in range(6)]
    for i in range(DEPTH):
        lp = {name: val[i] for name, val in small.items()}
        mk = _matmul([mem_bf], xa_wk, F32, 1024, 512, i)
        mv = _matmul([mem_bf], xa_wv, F32, 1024, 512, i)
        yp, yp_bf, st_p = _layer(gp, gp_rw, yp, yp_bf, tabs_p, zeros_p, 0, mk.reshape(bp, N_MEM, D_MODEL),
                                 mv.reshape(bp, N_MEM, D_MODEL), None, lp, big, i, expand, 1024)
        for lst, val in zip(outs_p, st_p + (mk.reshape(bp, N_MEM, XA_H, XA_HD), mv.reshape(bp, N_MEM, XA_H, XA_HD))):
            lst.append(val)
        ys, ys_bf, st_s = _layer(gs, gs, ys, ys_bf, tabs_s, st_s_in, i, cache_mem_k, cache_mem_v, i, lp, big, i,
                                 expand, 1024)
        for lst, val in zip(outs_s, st_s):
            lst.append(val)
    return (yp.reshape(bp, tp, D_MODEL), ys.reshape(bs, ts, D_MODEL),
            *[jnp.stack(v) for v in outs_p], *[jnp.stack(v) for v in outs_s])
```

```python
import functools
import math

import jax
import jax.numpy as jnp
from jax import lax
from jax.experimental import pallas as pl
from jax.experimental.pallas import tpu as pltpu

F32 = jnp.float32
BF16 = jnp.bfloat16

D_MODEL = 2048
DEPTH = 2
PAST_LEN = 16384
GROUP = 512
RW_H, RW_HD = 8, 64
RW_IN = 1792
RW_LN_EPS = 64e-5
MB_H, MB_HD, MB_N, MB_G = 8, 64, 128, 2
MB_CONV = 4
MB_CONV_DIM = 1024
GLA_H, GLA_DK, GLA_DV = 4, 64, 128
GLA_QK = 256
GLA_LORA = 16
GLA_TAU = 16.0
RET_H, RET_HD = 4, 128
ROPE_BASE = 10000.0
N_MEM = 256
XA_H, XA_HD = 4, 512
D_FF = 5632
DN_ALPHA = (2 * DEPTH) ** 0.25
RET_LOG_GAMMA = tuple(math.log1p(-(2.0 ** (-5.0 - h))) for h in range(RET_H))

N_PAD = 7168
COL_MB_DT = 1792
COL_GLA_GK = 1920
COL_MB_XBC = 2048
COL_MB_Z = 3072
COL_GLA_V = 3584
COL_GLA_G = 4096
COL_GLA_Q = 4608
COL_GLA_K = 4864
COL_RET = 5120

CHUNK = 128
VMEM_LIMIT = 56 * 1024 * 1024


def _cparams(n_axes, vmem=VMEM_LIMIT):
    return pltpu.CompilerParams(dimension_semantics=("arbitrary",) * n_axes, vmem_limit_bytes=vmem)


def _dot(a, b):
    return jnp.dot(a.astype(BF16), b.astype(BF16), preferred_element_type=F32)


def _dot_nt(a, b):
    return lax.dot_general(a.astype(BF16), b.astype(BF16), (((1,), (1,)), ((), ())), preferred_element_type=F32)


def _dot_tn(a, b):
    return lax.dot_general(a.astype(BF16), b.astype(BF16), (((0,), (0,)), ((), ())), preferred_element_type=F32)


def _sigmoid(x):
    return 1.0 / (1.0 + jnp.exp(-x))


def _silu(x):
    return x * _sigmoid(x)


def _softplus(x):
    return jnp.maximum(x, 0.0) + jnp.log(1.0 + jnp.exp(-jnp.abs(x)))


def _split_dot(m_bf16, x):
    hi = x.astype(BF16)
    r1 = x - hi.astype(F32)
    mid = r1.astype(BF16)
    lo = (r1 - mid.astype(F32)).astype(BF16)
    return (jnp.dot(m_bf16, hi, preferred_element_type=F32)
            + jnp.dot(m_bf16, mid, preferred_element_type=F32)
            + jnp.dot(m_bf16, lo, preferred_element_type=F32))


def _head_sums(x, ones_bf16):
    hi = x.astype(BF16)
    lo = (x - hi.astype(F32)).astype(BF16)
    return (jnp.dot(hi, ones_bf16, preferred_element_type=F32)
            + jnp.dot(lo, ones_bf16, preferred_element_type=F32))


def _mo(x, m):
    return x if isinstance(x, int) else pl.multiple_of(x, m)


def _seg_masks(c, seg):
    sh = jnp.int32(int(math.log2(seg)))
    r = lax.broadcasted_iota(jnp.int32, (c, c), 0)
    q = lax.broadcasted_iota(jnp.int32, (c, c), 1)
    same = lax.shift_right_arithmetic(r, sh) == lax.shift_right_arithmetic(q, sh)
    incl = jnp.logical_and(same, r >= q)
    strict = jnp.logical_and(same, r > q)
    return incl, strict, same, r, q


def _mm_kernel(*refs, k_sizes):
    n_x = len(k_sizes)
    x_refs, w_ref, o_ref, wbf = refs[:n_x], refs[n_x], refs[n_x + 1], refs[n_x + 2]

    @pl.when(pl.program_id(1) == 0)
    def _():
        wbf[...] = w_ref[...].astype(BF16)

    acc = None
    off = 0
    for xr, ks in zip(x_refs, k_sizes):
        part = jnp.dot(xr[...].astype(BF16), wbf[off:off + ks, :], preferred_element_type=F32)
        acc = part if acc is None else acc + part
        off += ks
    o_ref[...] = acc.astype(o_ref.dtype)


def _w_spec(w, tn, layer):
    k = w.shape[-2]
    if w.ndim == 2:
        return pl.BlockSpec((k, tn), lambda j, i: (0, j))
    return pl.BlockSpec((None, k, tn), lambda j, i: (layer, 0, j))


def _matmul(xs, w, out_dtype, tm, tn, layer=None):
    m = xs[0].shape[0]
    k, n = w.shape[-2:]
    k_sizes = tuple(x.shape[1] for x in xs)
    assert sum(k_sizes) == k and m % tm == 0 and n % tn == 0
    in_specs = [pl.BlockSpec((tm, ks), lambda j, i: (i, 0)) for ks in k_sizes]
    in_specs.append(_w_spec(w, tn, layer))
    return pl.pallas_call(
        functools.partial(_mm_kernel, k_sizes=k_sizes),
        grid=(n // tn, m // tm),
        in_specs=in_specs,
        out_specs=pl.BlockSpec((tm, tn), lambda j, i: (i, j)),
        out_shape=jax.ShapeDtypeStruct((m, n), out_dtype),
        scratch_shapes=[pltpu.VMEM((k, tn), BF16)],
        compiler_params=_cparams(2),
    )(*xs, w)


def _ffn_gu_kernel(x_ref, wg_ref, wu_ref, o_ref, wg_bf, wu_bf):
    @pl.when(pl.program_id(1) == 0)
    def _():
        wg_bf[...] = wg_ref[...].astype(BF16)
        wu_bf[...] = wu_ref[...].astype(BF16)

    x = x_ref[...]
    gate = jnp.dot(x, wg_bf[...], preferred_element_type=F32)
    up = jnp.dot(x, wu_bf[...], preferred_element_type=F32)
    o_ref[...] = (_silu(gate) * up).astype(o_ref.dtype)


def _ffn_gate_up(x_bf, wg, wu, tm, tn, layer):
    m, k = x_bf.shape
    n = wg.shape[-1]
    return pl.pallas_call(
        _ffn_gu_kernel,
        grid=(n // tn, m // tm),
        in_specs=[pl.BlockSpec((tm, k), lambda j, i: (i, 0)), _w_spec(wg, tn, layer), _w_spec(wu, tn, layer)],
        out_specs=pl.BlockSpec((tm, tn), lambda j, i: (i, j)),
        out_shape=jax.ShapeDtypeStruct((m, n), BF16),
        scratch_shapes=[pltpu.VMEM((k, tn), BF16), pltpu.VMEM((k, tn), BF16)],
        compiler_params=_cparams(2),
    )(x_bf, wg, wu)


def _ln_kernel(x_ref, y_ref, g_ref, b_ref, of_ref, ob_ref):
    z = DN_ALPHA * x_ref[...] + y_ref[...]
    zc = z - jnp.mean(z, axis=-1, keepdims=True)
    var = jnp.mean(zc * zc, axis=-1, keepdims=True)
    out = zc * lax.rsqrt(var + 1e-5) * g_ref[...] + b_ref[...]
    of_ref[...] = out
    ob_ref[...] = out.astype(BF16)


def _res_ln(x, y, g, b, tm=256):
    m, d = x.shape
    row = pl.BlockSpec((tm, d), lambda i: (i, 0))
    vec = pl.BlockSpec((1, d), lambda i: (0, 0))
    return pl.pallas_call(
        _ln_kernel,
        grid=(m // tm,),
        in_specs=[row, row, vec, vec],
        out_specs=[row, row],
        out_shape=[jax.ShapeDtypeStruct((m, d), F32), jax.ShapeDtypeStruct((m, d), BF16)],
        compiler_params=_cparams(1),
    )(x, y, g.reshape(1, d), b.reshape(1, d))


def _xattn_kernel(q_ref, k_ref, v_ref, o_ref, *, nb, tq):
    for j in range(nb):
        rows = slice(j * tq, (j + 1) * tq)
        for h in range(XA_H):
            cols = slice(h * XA_HD, (h + 1) * XA_HD)
            s = _dot_nt(q_ref[rows, cols], k_ref[j, :, cols]) * (XA_HD ** -0.5)
            e = jnp.exp(s - jnp.max(s, axis=-1, keepdims=True))
            pr = e / jnp.sum(e, axis=-1, keepdims=True)
            o_ref[rows, cols] = _dot(pr, v_ref[j, :, cols]).astype(o_ref.dtype)


def _xattn_cache_kernel(q_ref, k_ref, v_ref, o_ref, *, nb, tq):
    nr = N_MEM * XA_H
    r = lax.broadcasted_iota(jnp.int32, (nr, XA_H * tq), 0)
    q = lax.broadcasted_iota(jnp.int32, (nr, XA_H * tq), 1)
    own = (r & (XA_H - 1)) == lax.shift_right_arithmetic(q, jnp.int32(int(math.log2(tq))))
    for j in range(nb):
        rows = slice(j * tq, (j + 1) * tq)
        kf = k_ref[j].reshape(nr, XA_HD)
        vf = v_ref[j].reshape(nr, XA_HD)
        qcat = jnp.concatenate([q_ref[rows, h * XA_HD:(h + 1) * XA_HD] for h in range(XA_H)], axis=0)
        s = jnp.where(own, _dot_nt(kf, qcat) * (XA_HD ** -0.5), -jnp.inf)
        e = jnp.exp(s - jnp.max(s, axis=0, keepdims=True))
        pr = e / jnp.sum(e, axis=0, keepdims=True)
        o = _dot_tn(pr, vf)
        for h in range(XA_H):
            o_ref[rows, h * XA_HD:(h + 1) * XA_HD] = o[h * tq:(h + 1) * tq, :].astype(o_ref.dtype)


def _cross_attn(q, mem_k, mem_v, bsz, t, nb, tq, out_dtype, layer=None):
    nt = t // tq
    rows = nb * tq
    if layer is None:
        body = _xattn_kernel
        kv_spec = pl.BlockSpec((nb, N_MEM, D_MODEL), lambda i, j: (i, 0, 0))
    else:
        body = _xattn_cache_kernel
        kv_spec = pl.BlockSpec((None, nb, N_MEM, XA_H, XA_HD), lambda i, j: (layer, i, 0, 0, 0))
    return pl.pallas_call(
        functools.partial(body, nb=nb, tq=tq),
        grid=(bsz // nb, nt),
        in_specs=[pl.BlockSpec((rows, D_MODEL), lambda i, j: (i * nt + j, 0)), kv_spec, kv_spec],
        out_specs=pl.BlockSpec((rows, D_MODEL), lambda i, j: (i * nt + j, 0)),
        out_shape=jax.ShapeDtypeStruct((bsz * t, D_MODEL), out_dtype),
        compiler_params=_cparams(2),
    )(q, mem_k, mem_v)


def _state_io(carry, h0_ref, hout_ref, hst):
    if carry:
        return (lambda s, h: hst[h]), (lambda s, h, val: hst.__setitem__(h, val))
    return (lambda s, h: h0_ref[s, h]), (lambda s, h, val: hout_ref.__setitem__((s, h), val))


def _for_segments(nseg, fn, unroll=True):
    if nseg == 1:
        fn(0)
    else:
        def body(s, c):
            fn(s)
            return c
        lax.fori_loop(0, nseg, body, 0, unroll=unroll)


def _ret_kernel(q_ref, k_ref, v_ref, g_ref, cos_ref, sin_ref, gn_ref, h0_ref, o_ref, hout_ref,
                hst, qs_sc, ks_sc, y_sc, *, rows, seg, carry):
    c = CHUNK
    nseg = c // seg
    if carry:
        @pl.when(pl.program_id(1) == 0)
        def _():
            hst[...] = h0_ref[0]
    get_h, set_h = _state_io(carry, h0_ref, hout_ref, hst)
    incl, _, _, r_i, c_i = _seg_masks(c, seg)
    dpos = (r_i - c_i).astype(F32)
    tau = (lax.broadcasted_iota(jnp.int32, (c, RET_HD), 0) & (seg - 1)).astype(F32)
    even = (lax.broadcasted_iota(jnp.int32, (c, GROUP), 1) & 1) == 0

    for ci in range(rows // c):
        rws = slice(ci * c, (ci + 1) * c)
        cosb = jnp.concatenate([cos_ref[rws, :]] * RET_H, axis=1)
        sinb = jnp.concatenate([sin_ref[rws, :]] * RET_H, axis=1)

        def rot(x):
            swapped = jnp.where(even, pltpu.roll(x, GROUP - 1, 1), pltpu.roll(x, 1, 1))
            return x * cosb + swapped * sinb

        qr = rot(q_ref[rws, :])
        kr = rot(k_ref[rws, :]) * (RET_HD ** -0.5)
        for h in range(RET_H):
            lgam = RET_LOG_GAMMA[h]
            cols = slice(h * RET_HD, (h + 1) * RET_HD)
            qh, kh = qr[:, cols], kr[:, cols]
            dm = jnp.where(incl, jnp.exp(dpos * lgam), 0.0)
            y_sc[...] = _dot(_dot_nt(qh, kh) * dm, v_ref[rws, cols])
            qs_sc[...] = qh * jnp.exp((tau + 1.0) * lgam)
            ks_sc[...] = kh * jnp.exp((seg - 1.0 - tau) * lgam)
            cd = math.exp(seg * lgam)

            def seg_step(s, h=h, cols=cols, cd=cd, ci=ci):
                sr = pl.ds(_mo(s * seg, seg), seg)
                vr = pl.ds(_mo(ci * c + s * seg, seg), seg)
                hs = get_h(s, h)
                y_sc[sr, :] += _dot(qs_sc[sr, :], hs)
                set_h(s, h, cd * hs + _dot_tn(ks_sc[sr, :], v_ref[vr, cols]))

            _for_segments(nseg, seg_step)
            y = y_sc[...]
            yn = y * lax.rsqrt(jnp.mean(y * y, axis=-1, keepdims=True) + 1e-5) * gn_ref[:, cols]
            o_ref[rws, cols] = (yn * _silu(g_ref[rws, cols])).astype(o_ref.dtype)

    if carry:
        @pl.when(pl.program_id(1) == pl.num_programs(1) - 1)
        def _():
            hout_ref[0] = hst[...]


def _mamba_kernel(z_ref, xbc_ref, dt_ref, cprev_ref, h0_ref, cw_ref, cb_ref, dtb_ref, alog_ref, dd_ref,
                  ng_ref, o_ref, cout_ref, hout_ref,
                  hst, xpad, act_sc, qs_sc, ks_sc, xs_sc, y_sc, yf_sc, ecum_sc, etot_sc, *, rows, seg, carry):
    c = CHUNK
    nseg = c // seg
    get_h, set_h = _state_io(carry, h0_ref, hout_ref, hst)

    def conv_act(window, n):
        acc = cb_ref[...] + window[5:5 + n] * cw_ref[0:1, :]
        for i in range(1, MB_CONV):
            acc = acc + window[5 + i:5 + i + n] * cw_ref[i:i + 1, :]
        return _silu(acc)

    if carry:
        @pl.when(pl.program_id(1) == 0)
        def _():
            hst[...] = h0_ref[0]
            xpad[0:8, :] = jnp.zeros((8, MB_CONV_DIM), F32)
            xpad[5:8, :] = cprev_ref[0]

        xpad[8:8 + rows, :] = xbc_ref[...]
        for ci in range(rows // c):
            acc = cb_ref[...] + xpad[ci * c + 5:ci * c + 5 + c, :] * cw_ref[0:1, :]
            for i in range(1, MB_CONV):
                acc = acc + xpad[ci * c + 5 + i:ci * c + 5 + i + c, :] * cw_ref[i:i + 1, :]
            act_sc[ci * c:(ci + 1) * c, :] = _silu(acc)
        xpad[0:8, :] = xpad[rows:rows + 8, :]

        @pl.when(pl.program_id(1) == pl.num_programs(1) - 1)
        def _():
            cout_ref[0] = xpad[5:8, :]
    else:
        def conv_seq(s, carry_):
            sr = pl.ds(pl.multiple_of(s * seg, seg), seg)
            xpad[5:8, :] = cprev_ref[s]
            xs = xbc_ref[sr, :]
            window = jnp.concatenate([xpad[0:8, :], xs], axis=0)
            act_sc[sr, :] = conv_act(window, seg)
            cout_ref[s] = xs[seg - 3:seg]
            return carry_

        xpad[0:8, :] = jnp.zeros((8, MB_CONV_DIM), F32)
        lax.fori_loop(0, rows // seg, conv_seq, 0)

    incl, _, same, _, _ = _seg_masks(c, seg)
    lt_bf = jnp.where(incl, 1.0, 0.0).astype(BF16)
    same_bf = jnp.where(same, 1.0, 0.0).astype(BF16)
    a_neg = -jnp.exp(alog_ref[...])

    for ci in range(rows // c):
        rws = slice(ci * c, (ci + 1) * c)
        dtv = _softplus(dt_ref[rws, :] + dtb_ref[...])
        la = dtv * a_neg
        cum = _split_dot(lt_bf, la)
        tot = _split_dot(same_bf, la)
        cum_t = cum.T
        ecum_sc[...] = jnp.exp(cum)
        etot_sc[...] = jnp.exp(tot)
        edec = jnp.exp(tot - cum)
        for g in range(MB_G):
            cg = act_sc[rws, 768 + g * MB_N:768 + (g + 1) * MB_N]
            bg = act_sc[rws, 512 + g * MB_N:512 + (g + 1) * MB_N]
            gmat = _dot_nt(cg, bg)
            qs_sc[...] = cg
            ks_sc[...] = bg
            for hh in range(MB_H // MB_G):
                h = g * (MB_H // MB_G) + hh
                cols = slice(h * MB_HD, (h + 1) * MB_HD)
                lmat = jnp.exp(jnp.where(incl, cum[:, h:h + 1] - cum_t[h:h + 1, :], -jnp.inf))
                xh = act_sc[rws, cols]
                xdt = xh * dtv[:, h:h + 1]
                y_sc[...] = _dot(gmat * lmat, xdt)
                xs_sc[...] = xdt * edec[:, h:h + 1]

                def seg_step(s, h=h):
                    sr = pl.ds(_mo(s * seg, seg), seg)
                    first = pl.ds(_mo(s * seg, seg), 1)
                    hs = get_h(s, h)
                    y_sc[sr, :] += _dot(qs_sc[sr, :], hs) * ecum_sc[sr, h:h + 1]
                    set_h(s, h, etot_sc[first, h:h + 1] * hs + _dot_tn(ks_sc[sr, :], xs_sc[sr, :]))

                _for_segments(nseg, seg_step)
                yf_sc[:, cols] = y_sc[...] + dd_ref[:, h:h + 1] * xh
        yz = yf_sc[...] * _silu(z_ref[rws, :])
        gw = GROUP // MB_G
        for g in range(MB_G):
            cols = slice(g * gw, (g + 1) * gw)
            part = yz[:, cols]
            nrm = part * lax.rsqrt(jnp.mean(part * part, axis=-1, keepdims=True) + 1e-5) * ng_ref[:, cols]
            o_ref[rws, cols] = nrm.astype(o_ref.dtype)

    if carry:
        @pl.when(pl.program_id(1) == pl.num_programs(1) - 1)
        def _():
            hout_ref[0] = hst[...]


def _gla_kernel(q_ref, k_ref, v_ref, gk_ref, g_ref, h0_ref, gkup_ref, gkb_ref, ng_ref, e_ref,
                o_ref, hout_ref,
                hst, qe_sc, ke_sc, etot_sc, p_sc, y_sc, *, rows, seg, carry):
    c = CHUNK
    nsub = c // seg
    if carry:
        @pl.when(pl.program_id(1) == 0)
        def _():
            for h in range(GLA_H):
                hst[h] = h0_ref[0, h].T
    incl, _, same, _, _ = _seg_masks(c, seg)
    lt_bf = jnp.where(incl, 1.0, 0.0).astype(BF16)
    same_bf = jnp.where(same, 1.0, 0.0).astype(BF16)
    row_i = lax.broadcasted_iota(jnp.int32, (seg, GLA_QK), 0)
    kcs = [slice(h * GLA_DK, (h + 1) * GLA_DK) for h in range(GLA_H)]
    vcs = [slice(h * GLA_DV, (h + 1) * GLA_DV) for h in range(GLA_H)]

    for ci in range(rows // c):
        rws = slice(ci * c, (ci + 1) * c)
        pre = _dot(gk_ref[rws, :], gkup_ref[...]) + gkb_ref[...]
        la = -_softplus(-pre) * (1.0 / GLA_TAU)
        cum = _split_dot(lt_bf, la)
        tot = _split_dot(same_bf, la)
        qv = q_ref[rws, :] * (GLA_DK ** -0.5)
        kv = k_ref[rws, :]
        qe_sc[...] = qv * jnp.exp(cum)
        ke_sc[...] = kv * jnp.exp(tot - cum)
        etot_sc[...] = jnp.exp(tot)

        for u in range(nsub):
            sr = slice(u * seg, (u + 1) * seg)
            cu, qu = cum[sr], qv[sr]
            for j in range(seg):
                r0 = u * seg + j
                pj = qu * jnp.exp(jnp.where(row_i >= j, cu - cum[r0:r0 + 1], -jnp.inf)) * kv[r0:r0 + 1]
                p_sc[u, j * seg:(j + 1) * seg, :] = pj
        for u in range(nsub):
            rm = _dot(p_sc[u], e_ref[...])
            y = rm[0:seg, :] * v_ref[ci * c + u * seg:ci * c + u * seg + 1, :]
            for j in range(1, seg):
                r0 = ci * c + u * seg + j
                y = y + rm[j * seg:(j + 1) * seg, :] * v_ref[r0:r0 + 1, :]
            y_sc[u * seg:(u + 1) * seg, :] = y
        states = [hst[h] for h in range(GLA_H)] if carry else None
        for u in range(nsub):
            sr = slice(u * seg, (u + 1) * seg)
            vr = slice(ci * c + u * seg, ci * c + (u + 1) * seg)
            hts = [states[h] if carry else h0_ref[u, h].T for h in range(GLA_H)]
            for h in range(GLA_H):
                y_sc[sr, vcs[h]] += _dot_nt(qe_sc[sr, kcs[h]], hts[h])
            upds = [_dot_tn(v_ref[vr, vcs[h]], ke_sc[sr, kcs[h]]) for h in range(GLA_H)]
            for h in range(GLA_H):
                new = hts[h] * etot_sc[u * seg:u * seg + 1, kcs[h]] + upds[h]
                if carry:
                    states[h] = new
                else:
                    hout_ref[u, h] = new.T
        if carry:
            for h in range(GLA_H):
                hst[h] = states[h]

        for h in range(GLA_H):
            y = y_sc[:, vcs[h]]
            yn = y * lax.rsqrt(jnp.mean(y * y, axis=-1, keepdims=True) + 1e-5) * ng_ref[:, vcs[h]]
            o_ref[rws, vcs[h]] = (yn * _silu(g_ref[rws, vcs[h]])).astype(o_ref.dtype)

    if carry:
        @pl.when(pl.program_id(1) == pl.num_programs(1) - 1)
        def _():
            for h in range(GLA_H):
                hout_ref[0, h] = hst[h].T


def _rwkv_kernel(p_ref, sp_ref, h0_ref, mu_ref, w0_ref, wup_ref, a0_ref, aup_ref, gup_ref, kk_ref, ka_ref,
                 rk_ref, lng_ref, lnb_ref, ones_ref, o_ref, hout_ref,
                 hst, last_sc, prev_sc, w_sc, y_sc, bt_sc, kt_sc, v_sc, rt_sc, u_sc, rkb_sc, etot_sc, out_sc, kap_sc,
                 pw_sc, t_sc, ak_sc, pb_sc, pk_sc, *, rows, seg, sub, carry):
    c = CHUNK
    assert rows == c
    nsub = c // sub
    n_iter = int(math.log2(sub)) - 1
    npair = RW_H // 2
    pw_ = 2 * RW_HD
    pr_i = lax.broadcasted_iota(jnp.int32, (pw_, pw_), 0)
    pc_i = lax.broadcasted_iota(jnp.int32, (pw_, pw_), 1)
    bd_mask = (pr_i >= RW_HD) == (pc_i >= RW_HD)
    low_rows = lax.broadcasted_iota(jnp.int32, (pw_, RW_HD), 0) < RW_HD

    def to_bd(stack):
        return jnp.where(bd_mask, jnp.concatenate([stack, stack], axis=1), 0.0)

    def from_bd(bd):
        return jnp.where(low_rows, bd[:, 0:RW_HD], bd[:, RW_HD:pw_])

    if carry:
        @pl.when(pl.program_id(1) == 0)
        def _():
            for pi in range(npair):
                hst[pi] = to_bd(h0_ref[0, pi])
            last_sc[...] = jnp.broadcast_to(sp_ref[0], (8, RW_IN))

        prev_row = jnp.broadcast_to(last_sc[0:1, :], (c, RW_IN))
    else:
        for s in range(rows // seg):
            prev_sc[s * seg:(s + 1) * seg, :] = jnp.broadcast_to(sp_ref[s], (seg, RW_IN))
        prev_row = prev_sc[...]
    incl, strict, same, r_i, c_i = _seg_masks(c, sub)
    lt_bf = jnp.where(incl, 1.0, 0.0).astype(BF16)
    same_bf = jnp.where(same, 1.0, 0.0).astype(BF16)
    eye = jnp.where(r_i == c_i, 1.0, 0.0)
    row_w = lax.broadcasted_iota(jnp.int32, (c, RW_IN), 0)
    first_row = (row_w & ((c if carry else seg) - 1)) == 0

    p = p_ref[...]
    prev = jnp.where(first_row, prev_row, pltpu.roll(p, 1, 0))
    xs = p + (prev - p) * mu_ref[...]
    r = xs[:, 0:GROUP]
    k = xs[:, GROUP:2 * GROUP]
    v = xs[:, 2 * GROUP:3 * GROUP]
    wa = xs[:, 3 * GROUP:3 * GROUP + 128]
    gd = xs[:, 3 * GROUP + 128:RW_IN]
    w = -_softplus(-(w0_ref[...] + _dot(jnp.tanh(wa), wup_ref[...]))) - 0.5
    ld = -jnp.exp(w)
    a = _sigmoid(a0_ref[...] + _dot(wa, aup_ref[...]))
    gate = _dot(_sigmoid(gd), gup_ref[...])
    kkr = k * kk_ref[...]
    k2 = k * (1.0 + (a - 1.0) * ka_ref[...])
    cum = _split_dot(lt_bf, ld)
    tot = _split_dot(same_bf, ld)
    ecum = jnp.exp(cum)
    einv = jnp.exp(-cum)
    eprev = jnp.exp(cum - ld)
    rkb_sc[...] = r * k2 * rk_ref[...]
    etot_sc[...] = jnp.exp(tot)
    v_sc[...] = v
    rtil = r * ecum
    rt_sc[...] = rtil
    kt_sc[...] = k2 * einv
    rn = lax.rsqrt(jnp.maximum(_head_sums(kkr * kkr, ones_ref[...]), 1e-24))
    braw = kkr * a * einv * rn
    kraw = kkr * eprev * rn

    heads = [slice(h * RW_HD, (h + 1) * RW_HD) for h in range(RW_H)]
    for h, cols in enumerate(heads):
        kap = kraw[:, cols]
        bt = braw[:, cols]
        kap_sc[:, cols] = kap
        bt_sc[:, cols] = bt
        m1 = _dot_nt(jnp.concatenate([kap, rtil[:, cols]], axis=0), jnp.concatenate([bt, kt_sc[:, cols]], axis=0))
        x = jnp.where(strict, -m1[0:c, 0:c], 0.0)
        pw_sc[0, h] = x
        t_sc[h] = eye + x
        ak_sc[h] = jnp.where(strict, m1[0:c, c:2 * c], 0.0)
        pb_sc[h] = jnp.where(incl, m1[c:2 * c, 0:c], 0.0)
        pk_sc[h] = jnp.where(incl, m1[c:2 * c, c:2 * c], 0.0)
    for h, cols in enumerate(heads):
        y_sc[:, cols] = _dot(ak_sc[h], v_sc[:, cols])
    for it in range(n_iter):
        src, dst = it % 2, (it + 1) % 2
        for h in range(RW_H):
            pw = pw_sc[src, h]
            pw_sc[dst, h] = _dot(pw, pw)
        for h in range(RW_H):
            tm = t_sc[h]
            t_sc[h] = tm + _dot(tm, pw_sc[dst, h])
    for h, cols in enumerate(heads):
        wy = _dot(t_sc[h], jnp.concatenate([kap_sc[:, cols], y_sc[:, cols]], axis=1))
        w_sc[:, cols] = wy[:, 0:RW_HD]
        y_sc[:, cols] = wy[:, RW_HD:2 * RW_HD]

    states = [hst[pi] for pi in range(npair)] if carry else None
    units = [(s, pi) for s in range(nsub) for pi in range(npair)]
    group = npair if carry else 2 * npair
    for g0 in range(0, len(units), group):
        grp_units = units[g0:g0 + group]
        sts, wrs, uus, upds = [], [], [], []
        for s, pi in grp_units:
            sts.append(states[pi] if carry else to_bd(h0_ref[s, pi]))
        for (s, pi), st in zip(grp_units, sts):
            sr, pc = slice(s * sub, (s + 1) * sub), slice(pi * pw_, (pi + 1) * pw_)
            wrs.append(_dot_nt(jnp.concatenate([w_sc[sr, pc], rt_sc[sr, pc]], axis=0), st))
        for (s, pi), wr in zip(grp_units, wrs):
            sr, pc = slice(s * sub, (s + 1) * sub), slice(pi * pw_, (pi + 1) * pw_)
            uu = -wr[0:sub, :] - y_sc[sr, pc]
            u_sc[sr, pc] = uu
            out_sc[sr, pc] = wr[sub:2 * sub, :]
            uus.append(uu)
        for (s, pi), uu in zip(grp_units, uus):
            sr, pc = slice(s * sub, (s + 1) * sub), slice(pi * pw_, (pi + 1) * pw_)
            upds.append(_dot_tn(jnp.concatenate([uu, v_sc[sr, pc]], axis=0),
                                jnp.concatenate([bt_sc[sr, pc], kt_sc[sr, pc]], axis=0)))
        for (s, pi), st, upd in zip(grp_units, sts, upds):
            pc = slice(pi * pw_, (pi + 1) * pw_)
            new = (st + jnp.where(bd_mask, upd, 0.0)) * etot_sc[s * sub:s * sub + 1, pc]
            if carry:
                states[pi] = new
            else:
                hout_ref[s, pi] = from_bd(new)
    if carry:
        for pi in range(npair):
            hst[pi] = states[pi]

    for h, cols in enumerate(heads):
        out_sc[:, cols] += _dot(jnp.concatenate([pb_sc[h], pk_sc[h]], axis=1),
                                jnp.concatenate([u_sc[:, cols], v_sc[:, cols]], axis=0))
    ones = ones_ref[...]
    o_all = out_sc[...]
    oc = o_all - _head_sums(o_all, ones) * (1.0 / RW_HD)
    var = _head_sums(oc * oc, ones) * (1.0 / RW_HD)
    on = oc * lax.rsqrt(var + RW_LN_EPS) * lng_ref[...] + lnb_ref[...]
    out_sc[...] = on + _head_sums(rkb_sc[...], ones) * v_sc[...]
    o_ref[...] = (out_sc[...] * gate).astype(o_ref.dtype)

    if carry:
        last_sc[...] = jnp.broadcast_to(p_ref[rows - 1:rows, :], (8, RW_IN))

        @pl.when(pl.program_id(1) == pl.num_programs(1) - 1)
        def _():
            for pi in range(npair):
                hout_ref[0, pi] = from_bd(hst[pi])


class _Group:
    def __init__(self, bsz, t, rows):
        self.bsz, self.t, self.rows = bsz, t, rows
        self.carry = t >= CHUNK
        self.nb = 1 if self.carry else rows // t
        self.nt = t // rows if self.carry else 1
        self.grid = (bsz // self.nb, self.nt)
        self.seg = CHUNK if self.carry else t

    def rows_spec(self, width, col_block):
        nt = self.nt
        return pl.BlockSpec((self.rows, width), lambda i, j: (i * nt + j, col_block))

    def state_spec(self, shape, layer=None):
        zeros = (0,) * len(shape)
        if layer is None:
            return pl.BlockSpec((self.nb,) + tuple(shape), lambda i, j: (i,) + zeros)
        return pl.BlockSpec((None, self.nb) + tuple(shape), lambda i, j: (layer, i) + zeros)

    def out_rows(self, width, dtype=BF16):
        nt = self.nt
        return (pl.BlockSpec((self.rows, width), lambda i, j: (i * nt + j, 0)),
                jax.ShapeDtypeStruct((self.bsz * self.t, width), dtype))


def _vec_spec(shape):
    zeros = (0,) * len(shape)
    return pl.BlockSpec(tuple(shape), lambda i, j: zeros)


def _retention(grp, p, cos_t, sin_t, h0, sl, gn):
    c = CHUNK
    shape = (RET_H, RET_HD, RET_HD)
    in_specs = [grp.rows_spec(GROUP, COL_RET // GROUP + n) for n in range(4)]
    tab = pl.BlockSpec((grp.rows, RET_HD), (lambda i, j: (j, 0)) if grp.carry else (lambda i, j: (0, 0)))
    in_specs += [tab, tab, _vec_spec((1, GROUP)), grp.state_spec(shape, sl)]
    o_spec, o_shape = grp.out_rows(GROUP)
    return pl.pallas_call(
        functools.partial(_ret_kernel, rows=grp.rows, seg=grp.seg, carry=grp.carry),
        grid=grp.grid,
        in_specs=in_specs,
        out_specs=[o_spec, grp.state_spec(shape)],
        out_shape=[o_shape, jax.ShapeDtypeStruct(h0.shape[1:], F32)],
        scratch_shapes=[pltpu.VMEM(shape, F32), pltpu.VMEM((c, RET_HD), F32),
                        pltpu.VMEM((c, RET_HD), F32), pltpu.VMEM((c, RET_HD), F32)],
        compiler_params=_cparams(2),
    )(p, p, p, p, cos_t, sin_t, gn.reshape(1, GROUP), h0)


def _mamba(grp, p, cprev, h0, sl, lp):
    c = CHUNK
    cshape, hshape = (MB_CONV - 1, MB_CONV_DIM), (MB_H, MB_N, MB_HD)
    pad8 = lambda a: jnp.pad(a.reshape(1, MB_H), ((0, 0), (0, 128 - MB_H)))
    in_specs = [grp.rows_spec(GROUP, COL_MB_Z // GROUP), grp.rows_spec(MB_CONV_DIM, COL_MB_XBC // MB_CONV_DIM),
                grp.rows_spec(128, COL_MB_DT // 128),
                grp.state_spec(cshape, sl), grp.state_spec(hshape, sl),
                _vec_spec((MB_CONV, MB_CONV_DIM)), _vec_spec((1, MB_CONV_DIM)), _vec_spec((1, 128)),
                _vec_spec((1, 128)), _vec_spec((1, 128)), _vec_spec((1, GROUP))]
    o_spec, o_shape = grp.out_rows(GROUP)
    return pl.pallas_call(
        functools.partial(_mamba_kernel, rows=grp.rows, seg=grp.seg, carry=grp.carry),
        grid=grp.grid,
        in_specs=in_specs,
        out_specs=[o_spec, grp.state_spec(cshape), grp.state_spec(hshape)],
        out_shape=[o_shape, jax.ShapeDtypeStruct(cprev.shape[1:], F32), jax.ShapeDtypeStruct(h0.shape[1:], F32)],
        scratch_shapes=[pltpu.VMEM(hshape, F32),
                        pltpu.VMEM(((grp.rows if grp.carry else 0) + 8, MB_CONV_DIM), F32),
                        pltpu.VMEM((grp.rows, MB_CONV_DIM), F32),
                        pltpu.VMEM((c, MB_N), F32), pltpu.VMEM((c, MB_N), F32), pltpu.VMEM((c, MB_HD), F32),
                        pltpu.VMEM((c, MB_HD), F32), pltpu.VMEM((c, GROUP), F32),
                        pltpu.VMEM((c, 128), F32), pltpu.VMEM((c, 128), F32)],
        compiler_params=_cparams(2),
    )(p, p, p, cprev, h0, lp['mb_conv_w'], lp['mb_conv_b'].reshape(1, MB_CONV_DIM), pad8(lp['mb_dt_bias']),
      pad8(lp['mb_a_log']), pad8(lp['mb_d']), lp['mb_norm_g'].reshape(1, GROUP))


def _gla(grp, p, h0, sl, lp, expand):
    c = CHUNK
    shape = (GLA_H, GLA_DK, GLA_DV)
    sub = 16 if grp.carry else grp.seg
    gk_up = jnp.pad(lp['gla_gk_up'], ((0, 128 - GLA_LORA), (0, 0)))
    in_specs = [grp.rows_spec(GLA_QK, COL_GLA_Q // GLA_QK), grp.rows_spec(GLA_QK, COL_GLA_K // GLA_QK),
                grp.rows_spec(GROUP, COL_GLA_V // GROUP), grp.rows_spec(128, COL_GLA_GK // 128),
                grp.rows_spec(GROUP, COL_GLA_G // GROUP), grp.state_spec(shape, sl),
                _vec_spec((128, GLA_QK)), _vec_spec((1, GLA_QK)), _vec_spec((1, GROUP)), _vec_spec((GLA_QK, GROUP))]
    o_spec, o_shape = grp.out_rows(GROUP)
    return pl.pallas_call(
        functools.partial(_gla_kernel, rows=grp.rows, seg=sub, carry=grp.carry),
        grid=grp.grid,
        in_specs=in_specs,
        out_specs=[o_spec, grp.state_spec(shape)],
        out_shape=[o_shape, jax.ShapeDtypeStruct(h0.shape[1:], F32)],
        scratch_shapes=[pltpu.VMEM((GLA_H, GLA_DV, GLA_DK), F32)]
        + [pltpu.VMEM((c, GLA_QK), F32) for _ in range(3)]
        + [pltpu.VMEM((c // sub, sub * sub, GLA_QK), F32), pltpu.VMEM((c, GROUP), F32)],
        compiler_params=_cparams(2),
    )(p, p, p, p, p, h0, gk_up, lp['gla_gk_b'].reshape(1, GLA_QK), lp['gla_norm_g'].reshape(1, GROUP), expand)


def _rwkv(grp, p, shift_prev, h0, sl, lp):
    c = CHUNK
    head_ones = (jnp.arange(GROUP)[:, None] // RW_HD == jnp.arange(GROUP)[None, :] // RW_HD).astype(BF16)
    shape = (RW_H // 2, 2 * RW_HD, RW_HD)
    h0 = h0.reshape(h0.shape[:2] + shape)
    sub = 64 if grp.carry else grp.seg
    row = lambda a: a.reshape(1, -1)
    w_up = jnp.pad(lp['rw_w_up'], ((0, 64), (0, 0)))
    a_up = jnp.pad(lp['rw_a_up'], ((64, 0), (0, 0)))
    in_specs = [grp.rows_spec(RW_IN, 0), grp.state_spec((1, RW_IN), sl), grp.state_spec(shape, sl),
                _vec_spec((1, RW_IN)), _vec_spec((1, GROUP)), _vec_spec((128, GROUP)), _vec_spec((1, GROUP)),
                _vec_spec((128, GROUP)), _vec_spec((128, GROUP))] + [_vec_spec((1, GROUP))] * 5 + [_vec_spec((GROUP, GROUP))]
    o_spec, o_shape = grp.out_rows(GROUP)
    wide = lambda: pltpu.VMEM((c, GROUP), F32)
    return pl.pallas_call(
        functools.partial(_rwkv_kernel, rows=grp.rows, seg=grp.seg, sub=sub, carry=grp.carry),
        grid=grp.grid,
        in_specs=in_specs,
        out_specs=[o_spec, grp.state_spec(shape)],
        out_shape=[o_shape, jax.ShapeDtypeStruct(h0.shape[1:], F32)],
        scratch_shapes=[pltpu.VMEM((RW_H // 2, 2 * RW_HD, 2 * RW_HD), F32), pltpu.VMEM((8, RW_IN), F32),
                        pltpu.VMEM((c, RW_IN), F32)] + [wide() for _ in range(11)]
        + [pltpu.VMEM((2, RW_H, c, c), F32)] + [pltpu.VMEM((RW_H, c, c), F32) for _ in range(4)],
        compiler_params=_cparams(2),
    )(p, shift_prev.reshape(shift_prev.shape[0], -1, 1, RW_IN), h0, row(lp['rw_mu']), row(lp['rw_w0']), w_up,
      row(lp['rw_a0']), a_up, lp['rw_g_up'], row(lp['rw_k_k']), row(lp['rw_k_a']), row(lp['rw_r_k']),
      row(lp['rw_ln_g']), row(lp['rw_ln_b']), head_ones)


def _rope_tables(pos0, t):
    half = RET_HD // 2
    inv = 1.0 / (ROPE_BASE ** jnp.linspace(0.0, 1.0, half, dtype=F32))
    pos = pos0 + jnp.arange(t, dtype=F32)
    ang = pos[:, None] * inv[None, :]
    cos, sin = jnp.cos(ang), jnp.sin(ang)
    cos_t = jnp.stack([cos, cos], axis=-1).reshape(t, RET_HD)
    sin_t = jnp.stack([-sin, sin], axis=-1).reshape(t, RET_HD)
    return cos_t, sin_t


def _pad_w_in(w):
    z = lambda n: jnp.zeros(w.shape[:-1] + (n,), w.dtype)
    return jnp.concatenate([
        w[..., 0:1792], w[..., 3328:3336], z(120), w[..., 4360:4376], z(112), w[..., 2304:3328],
        w[..., 1792:2304], w[..., 3848:4360], w[..., 4376:4888], w[..., 3336:3592], w[..., 3592:3848],
        w[..., 4888:6936]], axis=-1)


def _layer(grp, grp_rw, x, x_bf, pos_tabs, states, sl, mem_k, mem_v, cache_layer, lp, big, wl, expand, tm):
    rw_shift, rw_state, mb_conv, mb_state, gla_state, ret_state = states
    bsz, t = grp.bsz, grp.t
    p = _matmul([x_bf], big['w_in'], F32, tm, 1024, wl)
    o_rw, rw_new = _rwkv(grp_rw, p, rw_shift, rw_state, sl, lp)
    rw_new = rw_new.reshape(bsz, RW_H, RW_HD, RW_HD)
    o_mb, conv_new, mb_new = _mamba(grp, p, mb_conv, mb_state, sl, lp)
    o_gl, gla_new = _gla(grp, p, gla_state, sl, lp, expand)
    o_rt, ret_new = _retention(grp, p, pos_tabs[0], pos_tabs[1], ret_state, sl, lp['ret_norm_g'])
    shift_new = p.reshape(bsz, t, N_PAD)[:, t - 1, 0:RW_IN]
    mix = _matmul([o_rw, o_mb, o_gl, o_rt], big['w_out'], F32, tm, 1024, wl)
    x, x_bf = _res_ln(x, mix, lp['ln1_g'], lp['ln1_b'])
    q = _matmul([x_bf], big['xa_wq'], BF16 if grp.carry else F32, tm, 1024, wl)
    if grp.carry:
        att = _cross_attn(q, mem_k, mem_v, bsz, t, 1, 512, BF16)
    else:
        att = _cross_attn(q, mem_k, mem_v, bsz, t, 4, t, F32, layer=cache_layer)
    ao = _matmul([att], big['xa_wo'], F32, tm, 1024, wl)
    x, x_bf = _res_ln(x, ao, lp['ln2_g'], lp['ln2_b'])
    hid = _ffn_gate_up(x_bf, big['ffn_w_gate'], big['ffn_w_up'], tm, 512, wl)
    down = _matmul([hid], big['ffn_w_down'], F32, 512, 512, wl)
    x, x_bf = _res_ln(x, down, lp['ln3_g'], lp['ln3_b'])
    return x, x_bf, (shift_new, rw_new, conv_new, mb_new, gla_new, ret_new)


def kernel(x_prompt, x_sample, state_rwkv_shift, state_rwkv_wkv, state_mamba_conv, state_mamba_ssm, state_gla,
           state_ret, cache_mem_k, cache_mem_v, mem_prompt, w_in, w_out, ln1_g, ln1_b, rw_mu, rw_w0, rw_w_up,
           rw_a0, rw_a_up, rw_g_up, rw_k_k, rw_k_a, rw_r_k, rw_ln_g, rw_ln_b, mb_conv_w, mb_conv_b, mb_dt_bias,
           mb_a_log, mb_d, mb_norm_g, gla_gk_up, gla_gk_b, gla_norm_g, ret_norm_g, ln2_g, ln2_b, xa_wq, xa_wk,
           xa_wv, xa_wo, ln3_g, ln3_b, ffn_w_gate, ffn_w_up, ffn_w_down):
    small = dict(
        ln1_g=ln1_g, ln1_b=ln1_b, rw_mu=rw_mu, rw_w0=rw_w0, rw_w_up=rw_w_up, rw_a0=rw_a0,
        rw_a_up=rw_a_up, rw_g_up=rw_g_up, rw_k_k=rw_k_k, rw_k_a=rw_k_a, rw_r_k=rw_r_k, rw_ln_g=rw_ln_g,
        rw_ln_b=rw_ln_b, mb_conv_w=mb_conv_w, mb_conv_b=mb_conv_b, mb_dt_bias=mb_dt_bias, mb_a_log=mb_a_log,
        mb_d=mb_d, mb_norm_g=mb_norm_g, gla_gk_up=gla_gk_up, gla_gk_b=gla_gk_b, gla_norm_g=gla_norm_g,
        ret_norm_g=ret_norm_g, ln2_g=ln2_g, ln2_b=ln2_b, ln3_g=ln3_g, ln3_b=ln3_b)
    big = dict(w_in=_pad_w_in(w_in), w_out=w_out, xa_wq=xa_wq, xa_wo=xa_wo, ffn_w_gate=ffn_w_gate,
               ffn_w_up=ffn_w_up, ffn_w_down=ffn_w_down)
    bp, tp, _ = x_prompt.shape
    bs, ts, _ = x_sample.shape
    gp = _Group(bp, tp, 512)
    gp_rw = _Group(bp, tp, CHUNK)
    gs = _Group(bs, ts, CHUNK)
    expand = (jnp.arange(GLA_QK)[:, None] // GLA_DK == jnp.arange(GROUP)[None, :] // GLA_DV).astype(BF16)
    tabs_p = _rope_tables(0.0, tp)
    tabs_s = tuple(jnp.tile(tb, (gs.nb, 1)) for tb in _rope_tables(float(PAST_LEN), ts))
    zeros_p = (jnp.zeros((1, bp, RW_IN), F32), jnp.zeros((1, bp, RW_H, RW_HD, RW_HD), F32),
               jnp.zeros((1, bp, MB_CONV - 1, MB_CONV_DIM), F32), jnp.zeros((1, bp, MB_H, MB_N, MB_HD), F32),
               jnp.zeros((1, bp, GLA_H, GLA_DK, GLA_DV), F32), jnp.zeros((1, bp, RET_H, RET_HD, RET_HD), F32))
    st_s_in = (state_rwkv_shift, state_rwkv_wkv, state_mamba_conv, state_mamba_ssm, state_gla, state_ret)

    yp = x_prompt.reshape(bp * tp, D_MODEL)
    ys = x_sample.reshape(bs * ts, D_MODEL)
    yp_bf, ys_bf = yp.astype(BF16), ys.astype(BF16)
    mem_bf = mem_prompt.reshape(bp * N_MEM, D_MODEL).astype(BF16)
    outs_p = [[] for _ in range(8)]
    outs_s = [[] for _ in range(6)]
    for i in range(DEPTH):
        lp = {name: val[i] for name, val in small.items()}
        mk = _matmul([mem_bf], xa_wk, F32, 1024, 512, i)
        mv = _matmul([mem_bf], xa_wv, F32, 1024, 512, i)
        yp, yp_bf, st_p = _layer(gp, gp_rw, yp, yp_bf, tabs_p, zeros_p, 0, mk.reshape(bp, N_MEM, D_MODEL),
                                 mv.reshape(bp, N_MEM, D_MODEL), None, lp, big, i, expand, 1024)
        for lst, val in zip(outs_p, st_p + (mk.reshape(bp, N_MEM, XA_H, XA_HD), mv.reshape(bp, N_MEM, XA_H, XA_HD))):
            lst.append(val)
        ys, ys_bf, st_s = _layer(gs, gs, ys, ys_bf, tabs_s, st_s_in, i, cache_mem_k, cache_mem_v, i, lp, big, i,
                                 expand, 1024)
        for lst, val in zip(outs_s, st_s):
            lst.append(val)
    return (yp.reshape(bp, tp, D_MODEL), ys.reshape(bs, ts, D_MODEL),
            *[jnp.stack(v) for v in outs_p], *[jnp.stack(v) for v in outs_s])
```

```python
import functools
import math

import jax
import jax.numpy as jnp
from jax import lax
from jax.experimental import pallas as pl
from jax.experimental.pallas import tpu as pltpu

F32 = jnp.float32
BF16 = jnp.bfloat16

D_MODEL = 2048
DEPTH = 2
PAST_LEN = 16384
GROUP = 512
RW_H, RW_HD = 8, 64
RW_IN = 1792
RW_LN_EPS = 64e-5
MB_H, MB_HD, MB_N, MB_G = 8, 64, 128, 2
MB_CONV = 4
MB_CONV_DIM = 1024
GLA_H, GLA_DK, GLA_DV = 4, 64, 128
GLA_QK = 256
GLA_LORA = 16
GLA_TAU = 16.0
RET_H, RET_HD = 4, 128
ROPE_BASE = 10000.0
N_MEM = 256
XA_H, XA_HD = 4, 512
D_FF = 5632
DN_ALPHA = (2 * DEPTH) ** 0.25
RET_LOG_GAMMA = tuple(math.log1p(-(2.0 ** (-5.0 - h))) for h in range(RET_H))

N_PAD = 7168
COL_MB_DT = 1792
COL_GLA_GK = 1920
COL_MB_XBC = 2048
COL_MB_Z = 3072
COL_GLA_V = 3584
COL_GLA_G = 4096
COL_GLA_Q = 4608
COL_GLA_K = 4864
COL_RET = 5120

CHUNK = 128
VMEM_LIMIT = 56 * 1024 * 1024


def _cparams(n_axes, vmem=VMEM_LIMIT):
    return pltpu.CompilerParams(dimension_semantics=("arbitrary",) * n_axes, vmem_limit_bytes=vmem)


def _dot(a, b):
    return jnp.dot(a.astype(BF16), b.astype(BF16), preferred_element_type=F32)


def _dot_nt(a, b):
    return lax.dot_general(a.astype(BF16), b.astype(BF16), (((1,), (1,)), ((), ())), preferred_element_type=F32)


def _dot_tn(a, b):
    return lax.dot_general(a.astype(BF16), b.astype(BF16), (((0,), (0,)), ((), ())), preferred_element_type=F32)


def _sigmoid(x):
    return 1.0 / (1.0 + jnp.exp(-x))


def _silu(x):
    return x * _sigmoid(x)


def _softplus(x):
    return jnp.maximum(x, 0.0) + jnp.log(1.0 + jnp.exp(-jnp.abs(x)))


def _split_dot(m_bf16, x):
    hi = x.astype(BF16)
    r1 = x - hi.astype(F32)
    mid = r1.astype(BF16)
    lo = (r1 - mid.astype(F32)).astype(BF16)
    return (jnp.dot(m_bf16, hi, preferred_element_type=F32)
            + jnp.dot(m_bf16, mid, preferred_element_type=F32)
            + jnp.dot(m_bf16, lo, preferred_element_type=F32))


def _head_sums(x, ones_bf16):
    hi = x.astype(BF16)
    lo = (x - hi.astype(F32)).astype(BF16)
    return (jnp.dot(hi, ones_bf16, preferred_element_type=F32)
            + jnp.dot(lo, ones_bf16, preferred_element_type=F32))


def _mo(x, m):
    return x if isinstance(x, int) else pl.multiple_of(x, m)


def _seg_masks(c, seg):
    sh = jnp.int32(int(math.log2(seg)))
    r = lax.broadcasted_iota(jnp.int32, (c, c), 0)
    q = lax.broadcasted_iota(jnp.int32, (c, c), 1)
    same = lax.shift_right_arithmetic(r, sh) == lax.shift_right_arithmetic(q, sh)
    incl = jnp.logical_and(same, r >= q)
    strict = jnp.logical_and(same, r > q)
    return incl, strict, same, r, q


def _mm_kernel(*refs, k_sizes):
    n_x = len(k_sizes)
    x_refs, w_ref, o_ref, wbf = refs[:n_x], refs[n_x], refs[n_x + 1], refs[n_x + 2]

    @pl.when(pl.program_id(1) == 0)
    def _():
        wbf[...] = w_ref[...].astype(BF16)

    acc = None
    off = 0
    for xr, ks in zip(x_refs, k_sizes):
        part = jnp.dot(xr[...].astype(BF16), wbf[off:off + ks, :], preferred_element_type=F32)
        acc = part if acc is None else acc + part
        off += ks
    o_ref[...] = acc.astype(o_ref.dtype)


def _w_spec(w, tn, layer):
    k = w.shape[-2]
    if w.ndim == 2:
        return pl.BlockSpec((k, tn), lambda j, i: (0, j))
    return pl.BlockSpec((None, k, tn), lambda j, i: (layer, 0, j))


def _matmul(xs, w, out_dtype, tm, tn, layer=None):
    m = xs[0].shape[0]
    k, n = w.shape[-2:]
    k_sizes = tuple(x.shape[1] for x in xs)
    assert sum(k_sizes) == k and m % tm == 0 and n % tn == 0
    in_specs = [pl.BlockSpec((tm, ks), lambda j, i: (i, 0)) for ks in k_sizes]
    in_specs.append(_w_spec(w, tn, layer))
    return pl.pallas_call(
        functools.partial(_mm_kernel, k_sizes=k_sizes),
        grid=(n // tn, m // tm),
        in_specs=in_specs,
        out_specs=pl.BlockSpec((tm, tn), lambda j, i: (i, j)),
        out_shape=jax.ShapeDtypeStruct((m, n), out_dtype),
        scratch_shapes=[pltpu.VMEM((k, tn), BF16)],
        compiler_params=_cparams(2),
    )(*xs, w)


def _ffn_gu_kernel(x_ref, wg_ref, wu_ref, o_ref, wg_bf, wu_bf):
    @pl.when(pl.program_id(1) == 0)
    def _():
        wg_bf[...] = wg_ref[...].astype(BF16)
        wu_bf[...] = wu_ref[...].astype(BF16)

    x = x_ref[...]
    gate = jnp.dot(x, wg_bf[...], preferred_element_type=F32)
    up = jnp.dot(x, wu_bf[...], preferred_element_type=F32)
    o_ref[...] = (_silu(gate) * up).astype(o_ref.dtype)


def _ffn_gate_up(x_bf, wg, wu, tm, tn, layer):
    m, k = x_bf.shape
    n = wg.shape[-1]
    return pl.pallas_call(
        _ffn_gu_kernel,
        grid=(n // tn, m // tm),
        in_specs=[pl.BlockSpec((tm, k), lambda j, i: (i, 0)), _w_spec(wg, tn, layer), _w_spec(wu, tn, layer)],
        out_specs=pl.BlockSpec((tm, tn), lambda j, i: (i, j)),
        out_shape=jax.ShapeDtypeStruct((m, n), BF16),
        scratch_shapes=[pltpu.VMEM((k, tn), BF16), pltpu.VMEM((k, tn), BF16)],
        compiler_params=_cparams(2),
    )(x_bf, wg, wu)


def _mm_ln_kernel(*refs, k_sizes, cast_w):
    n_x = len(k_sizes)
    x_refs = refs[:n_x]
    w_ref, r_ref, g_ref, b_ref, of_ref, ob_ref = refs[n_x:n_x + 6]
    if cast_w:
        wbf = refs[n_x + 6]

        @pl.when(pl.program_id(0) == 0)
        def _():
            wbf[...] = w_ref[...].astype(BF16)
    else:
        wbf = w_ref
    acc = None
    off = 0
    for xr, ks in zip(x_refs, k_sizes):
        part = jnp.dot(xr[...].astype(BF16), wbf[off:off + ks, :], preferred_element_type=F32)
        acc = part if acc is None else acc + part
        off += ks
    z = DN_ALPHA * r_ref[...] + acc
    zc = z - jnp.mean(z, axis=-1, keepdims=True)
    var = jnp.mean(zc * zc, axis=-1, keepdims=True)
    out = zc * lax.rsqrt(var + 1e-5) * g_ref[...] + b_ref[...]
    of_ref[...] = out
    ob_ref[...] = out.astype(BF16)


def _matmul_res_ln(xs, w, layer, resid, g, b, tm):
    m, n = resid.shape
    k = w.shape[-2]
    k_sizes = tuple(x.shape[1] for x in xs)
    cast_w = w.dtype != BF16
    assert sum(k_sizes) == k and m % tm == 0 and w.shape[-1] == n
    row = pl.BlockSpec((tm, n), lambda i: (i, 0))
    vec = pl.BlockSpec((1, n), lambda i: (0, 0))
    in_specs = [pl.BlockSpec((tm, ks), lambda i: (i, 0)) for ks in k_sizes]
    in_specs.append(pl.BlockSpec((None, k, n), lambda i: (layer, 0, 0), pipeline_mode=pl.Buffered(1)))
    in_specs += [row, vec, vec]
    return pl.pallas_call(
        functools.partial(_mm_ln_kernel, k_sizes=k_sizes, cast_w=cast_w),
        grid=(m // tm,),
        in_specs=in_specs,
        out_specs=[row, row],
        out_shape=[jax.ShapeDtypeStruct((m, n), F32), jax.ShapeDtypeStruct((m, n), BF16)],
        scratch_shapes=[pltpu.VMEM((k, n), BF16)] if cast_w else [],
        compiler_params=_cparams(1),
    )(*xs, w, resid, g.reshape(1, n), b.reshape(1, n))


def _xattn_kernel(q_ref, k_ref, v_ref, o_ref, *, nb, tq):
    for j in range(nb):
        rows = slice(j * tq, (j + 1) * tq)
        for h in range(XA_H):
            cols = slice(h * XA_HD, (h + 1) * XA_HD)
            s = _dot_nt(q_ref[rows, cols], k_ref[j, :, cols]) * (XA_HD ** -0.5)
            e = jnp.exp(s - jnp.max(s, axis=-1, keepdims=True))
            pr = e / jnp.sum(e, axis=-1, keepdims=True)
            o_ref[rows, cols] = _dot(pr, v_ref[j, :, cols]).astype(o_ref.dtype)


def _xattn_cache_kernel(q_ref, k_ref, v_ref, o_ref, *, nb, tq):
    nr = N_MEM * XA_H
    r = lax.broadcasted_iota(jnp.int32, (nr, XA_H * tq), 0)
    q = lax.broadcasted_iota(jnp.int32, (nr, XA_H * tq), 1)
    own = (r & (XA_H - 1)) == lax.shift_right_arithmetic(q, jnp.int32(int(math.log2(tq))))
    for j in range(nb):
        rows = slice(j * tq, (j + 1) * tq)
        kf = k_ref[j].reshape(nr, XA_HD)
        vf = v_ref[j].reshape(nr, XA_HD)
        qcat = jnp.concatenate([q_ref[rows, h * XA_HD:(h + 1) * XA_HD] for h in range(XA_H)], axis=0)
        s = jnp.where(own, _dot_nt(kf, qcat) * (XA_HD ** -0.5), -jnp.inf)
        e = jnp.exp(s - jnp.max(s, axis=0, keepdims=True))
        pr = e / jnp.sum(e, axis=0, keepdims=True)
        o = _dot_tn(pr, vf)
        for h in range(XA_H):
            o_ref[rows, h * XA_HD:(h + 1) * XA_HD] = o[h * tq:(h + 1) * tq, :].astype(o_ref.dtype)


def _cross_attn(q, mem_k, mem_v, bsz, t, nb, tq, out_dtype, layer=None):
    nt = t // tq
    rows = nb * tq
    if layer is None:
        body = _xattn_kernel
        kv_spec = pl.BlockSpec((nb, N_MEM, D_MODEL), lambda i, j: (i, 0, 0))
    else:
        body = _xattn_cache_kernel
        kv_spec = pl.BlockSpec((None, nb, N_MEM, XA_H, XA_HD), lambda i, j: (layer, i, 0, 0, 0))
    return pl.pallas_call(
        functools.partial(body, nb=nb, tq=tq),
        grid=(bsz // nb, nt),
        in_specs=[pl.BlockSpec((rows, D_MODEL), lambda i, j: (i * nt + j, 0)), kv_spec, kv_spec],
        out_specs=pl.BlockSpec((rows, D_MODEL), lambda i, j: (i * nt + j, 0)),
        out_shape=jax.ShapeDtypeStruct((bsz * t, D_MODEL), out_dtype),
        compiler_params=_cparams(2),
    )(q, mem_k, mem_v)


def _state_io(carry, h0_ref, hout_ref, hst):
    if carry:
        return (lambda s, h: hst[h]), (lambda s, h, val: hst.__setitem__(h, val))
    return (lambda s, h: h0_ref[s, h]), (lambda s, h, val: hout_ref.__setitem__((s, h), val))


def _for_segments(nseg, fn, unroll=True):
    if nseg == 1:
        fn(0)
    else:
        def body(s, c):
            fn(s)
            return c
        lax.fori_loop(0, nseg, body, 0, unroll=unroll)


def _ret_kernel(q_ref, k_ref, v_ref, g_ref, cos_ref, sin_ref, gn_ref, h0_ref, o_ref, hout_ref,
                hst, qs_sc, ks_sc, y_sc, *, rows, seg, carry):
    c = CHUNK
    nseg = c // seg
    if carry:
        @pl.when(pl.program_id(1) == 0)
        def _():
            hst[...] = h0_ref[0]
    get_h, set_h = _state_io(carry, h0_ref, hout_ref, hst)
    incl, _, _, r_i, c_i = _seg_masks(c, seg)
    dpos = (r_i - c_i).astype(F32)
    tau = (lax.broadcasted_iota(jnp.int32, (c, RET_HD), 0) & (seg - 1)).astype(F32)
    even = (lax.broadcasted_iota(jnp.int32, (c, GROUP), 1) & 1) == 0

    for ci in range(rows // c):
        rws = slice(ci * c, (ci + 1) * c)
        cosb = jnp.concatenate([cos_ref[rws, :]] * RET_H, axis=1)
        sinb = jnp.concatenate([sin_ref[rws, :]] * RET_H, axis=1)

        def rot(x):
            swapped = jnp.where(even, pltpu.roll(x, GROUP - 1, 1), pltpu.roll(x, 1, 1))
            return x * cosb + swapped * sinb

        qr = rot(q_ref[rws, :])
        kr = rot(k_ref[rws, :]) * (RET_HD ** -0.5)
        for h in range(RET_H):
            lgam = RET_LOG_GAMMA[h]
            cols = slice(h * RET_HD, (h + 1) * RET_HD)
            qh, kh = qr[:, cols], kr[:, cols]
            dm = jnp.where(incl, jnp.exp(dpos * lgam), 0.0)
            y_sc[...] = _dot(_dot_nt(qh, kh) * dm, v_ref[rws, cols])
            qs_sc[...] = qh * jnp.exp((tau + 1.0) * lgam)
            ks_sc[...] = kh * jnp.exp((seg - 1.0 - tau) * lgam)
            cd = math.exp(seg * lgam)

            def seg_step(s, h=h, cols=cols, cd=cd, ci=ci):
                sr = pl.ds(_mo(s * seg, seg), seg)
                vr = pl.ds(_mo(ci * c + s * seg, seg), seg)
                hs = get_h(s, h)
                y_sc[sr, :] += _dot(qs_sc[sr, :], hs)
                set_h(s, h, cd * hs + _dot_tn(ks_sc[sr, :], v_ref[vr, cols]))

            _for_segments(nseg, seg_step)
            y = y_sc[...]
            yn = y * lax.rsqrt(jnp.mean(y * y, axis=-1, keepdims=True) + 1e-5) * gn_ref[:, cols]
            o_ref[rws, cols] = (yn * _silu(g_ref[rws, cols])).astype(o_ref.dtype)

    if carry:
        @pl.when(pl.program_id(1) == pl.num_programs(1) - 1)
        def _():
            hout_ref[0] = hst[...]


def _mamba_kernel(z_ref, xbc_ref, dt_ref, cprev_ref, h0_ref, cw_ref, cb_ref, dtb_ref, alog_ref, dd_ref,
                  ng_ref, o_ref, cout_ref, hout_ref,
                  hst, xpad, act_sc, qs_sc, ks_sc, xs_sc, y_sc, yf_sc, ecum_sc, etot_sc, *, rows, seg, carry):
    c = CHUNK
    nseg = c // seg
    get_h, set_h = _state_io(carry, h0_ref, hout_ref, hst)

    def conv_act(window, n):
        acc = cb_ref[...] + window[5:5 + n] * cw_ref[0:1, :]
        for i in range(1, MB_CONV):
            acc = acc + window[5 + i:5 + i + n] * cw_ref[i:i + 1, :]
        return _silu(acc)

    if carry:
        @pl.when(pl.program_id(1) == 0)
        def _():
            hst[...] = h0_ref[0]
            xpad[0:8, :] = jnp.zeros((8, MB_CONV_DIM), F32)
            xpad[5:8, :] = cprev_ref[0]

        xpad[8:8 + rows, :] = xbc_ref[...]
        for ci in range(rows // c):
            acc = cb_ref[...] + xpad[ci * c + 5:ci * c + 5 + c, :] * cw_ref[0:1, :]
            for i in range(1, MB_CONV):
                acc = acc + xpad[ci * c + 5 + i:ci * c + 5 + i + c, :] * cw_ref[i:i + 1, :]
            act_sc[ci * c:(ci + 1) * c, :] = _silu(acc)
        xpad[0:8, :] = xpad[rows:rows + 8, :]

        @pl.when(pl.program_id(1) == pl.num_programs(1) - 1)
        def _():
            cout_ref[0] = xpad[5:8, :]
    else:
        def conv_seq(s, carry_):
            sr = pl.ds(pl.multiple_of(s * seg, seg), seg)
            xpad[5:8, :] = cprev_ref[s]
            xs = xbc_ref[sr, :]
            window = jnp.concatenate([xpad[0:8, :], xs], axis=0)
            act_sc[sr, :] = conv_act(window, seg)
            cout_ref[s] = xs[seg - 3:seg]
            return carry_

        xpad[0:8, :] = jnp.zeros((8, MB_CONV_DIM), F32)
        lax.fori_loop(0, rows // seg, conv_seq, 0)

    incl, _, same, _, _ = _seg_masks(c, seg)
    lt_bf = jnp.where(incl, 1.0, 0.0).astype(BF16)
    same_bf = jnp.where(same, 1.0, 0.0).astype(BF16)
    a_neg = -jnp.exp(alog_ref[...])

    for ci in range(rows // c):
        rws = slice(ci * c, (ci + 1) * c)
        dtv = _softplus(dt_ref[rws, :] + dtb_ref[...])
        la = dtv * a_neg
        cum = _split_dot(lt_bf, la)
        tot = _split_dot(same_bf, la)
        cum_t = cum.T
        ecum_sc[...] = jnp.exp(cum)
        etot_sc[...] = jnp.exp(tot)
        edec = jnp.exp(tot - cum)
        for g in range(MB_G):
            cg = act_sc[rws, 768 + g * MB_N:768 + (g + 1) * MB_N]
            bg = act_sc[rws, 512 + g * MB_N:512 + (g + 1) * MB_N]
            gmat = _dot_nt(cg, bg)
            qs_sc[...] = cg
            ks_sc[...] = bg
            for hh in range(MB_H // MB_G):
                h = g * (MB_H // MB_G) + hh
                cols = slice(h * MB_HD, (h + 1) * MB_HD)
                lmat = jnp.exp(jnp.where(incl, cum[:, h:h + 1] - cum_t[h:h + 1, :], -jnp.inf))
                xh = act_sc[rws, cols]
                xdt = xh * dtv[:, h:h + 1]
                y_sc[...] = _dot(gmat * lmat, xdt)
                xs_sc[...] = xdt * edec[:, h:h + 1]

                def seg_step(s, h=h):
                    sr = pl.ds(_mo(s * seg, seg), seg)
                    first = pl.ds(_mo(s * seg, seg), 1)
                    hs = get_h(s, h)
                    y_sc[sr, :] += _dot_nt(qs_sc[sr, :], hs) * ecum_sc[sr, h:h + 1]
                    set_h(s, h, etot_sc[first, h:h + 1] * hs + _dot_tn(xs_sc[sr, :], ks_sc[sr, :]))

                _for_segments(nseg, seg_step)
                yf_sc[:, cols] = y_sc[...] + dd_ref[:, h:h + 1] * xh
        yz = yf_sc[...] * _silu(z_ref[rws, :])
        gw = GROUP // MB_G
        for g in range(MB_G):
            cols = slice(g * gw, (g + 1) * gw)
            part = yz[:, cols]
            nrm = part * lax.rsqrt(jnp.mean(part * part, axis=-1, keepdims=True) + 1e-5) * ng_ref[:, cols]
            o_ref[rws, cols] = nrm.astype(o_ref.dtype)

    if carry:
        @pl.when(pl.program_id(1) == pl.num_programs(1) - 1)
        def _():
            hout_ref[0] = hst[...]


def _gla_kernel(q_ref, k_ref, v_ref, gk_ref, g_ref, h0_ref, gkup_ref, gkb_ref, ng_ref, e_ref,
                o_ref, hout_ref,
                hst, qe_sc, ke_sc, etot_sc, p_sc, y_sc, *, rows, seg, carry):
    c = CHUNK
    nsub = c // seg
    if carry:
        @pl.when(pl.program_id(1) == 0)
        def _():
            for h in range(GLA_H):
                hst[h] = h0_ref[0, h].T
    incl, _, same, _, _ = _seg_masks(c, seg)
    lt_bf = jnp.where(incl, 1.0, 0.0).astype(BF16)
    same_bf = jnp.where(same, 1.0, 0.0).astype(BF16)
    row_i = lax.broadcasted_iota(jnp.int32, (seg, GLA_QK), 0)
    kcs = [slice(h * GLA_DK, (h + 1) * GLA_DK) for h in range(GLA_H)]
    vcs = [slice(h * GLA_DV, (h + 1) * GLA_DV) for h in range(GLA_H)]

    for ci in range(rows // c):
        rws = slice(ci * c, (ci + 1) * c)
        pre = _dot(gk_ref[rws, :], gkup_ref[...]) + gkb_ref[...]
        la = -_softplus(-pre) * (1.0 / GLA_TAU)
        cum = _split_dot(lt_bf, la)
        tot = _split_dot(same_bf, la)
        qv = q_ref[rws, :] * (GLA_DK ** -0.5)
        kv = k_ref[rws, :]
        qe_sc[...] = qv * jnp.exp(cum)
        ke_sc[...] = kv * jnp.exp(tot - cum)
        etot_sc[...] = jnp.exp(tot)

        for u in range(nsub):
            sr = slice(u * seg, (u + 1) * seg)
            cu, qu = cum[sr], qv[sr]
            for j in range(seg):
                r0 = u * seg + j
                pj = qu * jnp.exp(jnp.where(row_i >= j, cu - cum[r0:r0 + 1], -jnp.inf)) * kv[r0:r0 + 1]
                p_sc[u, j * seg:(j + 1) * seg, :] = pj
        for u in range(nsub):
            rm = _dot(p_sc[u], e_ref[...])
            y = rm[0:seg, :] * v_ref[ci * c + u * seg:ci * c + u * seg + 1, :]
            for j in range(1, seg):
                r0 = ci * c + u * seg + j
                y = y + rm[j * seg:(j + 1) * seg, :] * v_ref[r0:r0 + 1, :]
            y_sc[u * seg:(u + 1) * seg, :] = y
        states = [hst[h] for h in range(GLA_H)] if carry else None
        for u in range(nsub):
            sr = slice(u * seg, (u + 1) * seg)
            vr = slice(ci * c + u * seg, ci * c + (u + 1) * seg)
            hts = [states[h] if carry else h0_ref[u, h].T for h in range(GLA_H)]
            for h in range(GLA_H):
                y_sc[sr, vcs[h]] += _dot_nt(qe_sc[sr, kcs[h]], hts[h])
            upds = [_dot_tn(v_ref[vr, vcs[h]], ke_sc[sr, kcs[h]]) for h in range(GLA_H)]
            for h in range(GLA_H):
                new = hts[h] * etot_sc[u * seg:u * seg + 1, kcs[h]] + upds[h]
                if carry:
                    states[h] = new
                else:
                    hout_ref[u, h] = new.T
        if carry:
            for h in range(GLA_H):
                hst[h] = states[h]

        for h in range(GLA_H):
            y = y_sc[:, vcs[h]]
            yn = y * lax.rsqrt(jnp.mean(y * y, axis=-1, keepdims=True) + 1e-5) * ng_ref[:, vcs[h]]
            o_ref[rws, vcs[h]] = (yn * _silu(g_ref[rws, vcs[h]])).astype(o_ref.dtype)

    if carry:
        @pl.when(pl.program_id(1) == pl.num_programs(1) - 1)
        def _():
            for h in range(GLA_H):
                hout_ref[0, h] = hst[h].T


def _rwkv_kernel(p_ref, sp_ref, h0_ref, mu_ref, w0_ref, wup_ref, a0_ref, aup_ref, gup_ref, kk_ref, ka_ref,
                 rk_ref, lng_ref, lnb_ref, ones_ref, o_ref, hout_ref,
                 hst, last_sc, prev_sc, w_sc, y_sc, bt_sc, kt_sc, v_sc, rt_sc, u_sc, rkb_sc, etot_sc, out_sc, kap_sc,
                 pw_sc, t_sc, ak_sc, pb_sc, pk_sc, *, rows, seg, sub, carry):
    c = CHUNK
    assert rows == c
    nsub = c // sub
    n_iter = int(math.log2(sub)) - 1
    npair = RW_H // 2
    pw_ = 2 * RW_HD
    pr_i = lax.broadcasted_iota(jnp.int32, (pw_, pw_), 0)
    pc_i = lax.broadcasted_iota(jnp.int32, (pw_, pw_), 1)
    bd_mask = (pr_i >= RW_HD) == (pc_i >= RW_HD)
    low_rows = lax.broadcasted_iota(jnp.int32, (pw_, RW_HD), 0) < RW_HD

    def to_bd(stack):
        return jnp.where(bd_mask, jnp.concatenate([stack, stack], axis=1), 0.0)

    def from_bd(bd):
        return jnp.where(low_rows, bd[:, 0:RW_HD], bd[:, RW_HD:pw_])

    if carry:
        @pl.when(pl.program_id(1) == 0)
        def _():
            for pi in range(npair):
                hst[pi] = to_bd(h0_ref[0, pi])
            last_sc[...] = jnp.broadcast_to(sp_ref[0], (8, RW_IN))

        prev_row = jnp.broadcast_to(last_sc[0:1, :], (c, RW_IN))
    else:
        for s in range(rows // seg):
            prev_sc[s * seg:(s + 1) * seg, :] = jnp.broadcast_to(sp_ref[s], (seg, RW_IN))
        prev_row = prev_sc[...]
    incl, strict, same, r_i, c_i = _seg_masks(c, sub)
    lt_bf = jnp.where(incl, 1.0, 0.0).astype(BF16)
    same_bf = jnp.where(same, 1.0, 0.0).astype(BF16)
    eye = jnp.where(r_i == c_i, 1.0, 0.0)
    row_w = lax.broadcasted_iota(jnp.int32, (c, RW_IN), 0)
    first_row = (row_w & ((c if carry else seg) - 1)) == 0

    p = p_ref[...]
    prev = jnp.where(first_row, prev_row, pltpu.roll(p, 1, 0))
    xs = p + (prev - p) * mu_ref[...]
    r = xs[:, 0:GROUP]
    k = xs[:, GROUP:2 * GROUP]
    v = xs[:, 2 * GROUP:3 * GROUP]
    wa = xs[:, 3 * GROUP:3 * GROUP + 128]
    gd = xs[:, 3 * GROUP + 128:RW_IN]
    w = -_softplus(-(w0_ref[...] + _dot(jnp.tanh(wa), wup_ref[...]))) - 0.5
    ld = -jnp.exp(w)
    a = _sigmoid(a0_ref[...] + _dot(wa, aup_ref[...]))
    gate = _dot(_sigmoid(gd), gup_ref[...])
    kkr = k * kk_ref[...]
    k2 = k * (1.0 + (a - 1.0) * ka_ref[...])
    cum = _split_dot(lt_bf, ld)
    tot = _split_dot(same_bf, ld)
    ecum = jnp.exp(cum)
    einv = jnp.exp(-cum)
    eprev = jnp.exp(cum - ld)
    rkb_sc[...] = r * k2 * rk_ref[...]
    etot_sc[...] = jnp.exp(tot)
    v_sc[...] = v
    rtil = r * ecum
    rt_sc[...] = rtil
    kt_sc[...] = k2 * einv
    rn = lax.rsqrt(jnp.maximum(_head_sums(kkr * kkr, ones_ref[...]), 1e-24))
    braw = kkr * a * einv * rn
    kraw = kkr * eprev * rn

    heads = [slice(h * RW_HD, (h + 1) * RW_HD) for h in range(RW_H)]
    for h, cols in enumerate(heads):
        kap = kraw[:, cols]
        bt = braw[:, cols]
        kap_sc[:, cols] = kap
        bt_sc[:, cols] = bt
        m1 = _dot_nt(jnp.concatenate([kap, rtil[:, cols]], axis=0), jnp.concatenate([bt, kt_sc[:, cols]], axis=0))
        x = jnp.where(strict, -m1[0:c, 0:c], 0.0)
        pw_sc[0, h] = x
        t_sc[h] = eye + x
        ak_sc[h] = jnp.where(strict, m1[0:c, c:2 * c], 0.0)
        pb_sc[h] = jnp.where(incl, m1[c:2 * c, 0:c], 0.0)
        pk_sc[h] = jnp.where(incl, m1[c:2 * c, c:2 * c], 0.0)
    for h, cols in enumerate(heads):
        y_sc[:, cols] = _dot(ak_sc[h], v_sc[:, cols])
    for it in range(n_iter):
        src, dst = it % 2, (it + 1) % 2
        for h in range(RW_H):
            pw = pw_sc[src, h]
            pw_sc[dst, h] = _dot(pw, pw)
        for h in range(RW_H):
            tm = t_sc[h]
            t_sc[h] = tm + _dot(tm, pw_sc[dst, h])
    for h, cols in enumerate(heads):
        wy = _dot(t_sc[h], jnp.concatenate([kap_sc[:, cols], y_sc[:, cols]], axis=1))
        w_sc[:, cols] = wy[:, 0:RW_HD]
        y_sc[:, cols] = wy[:, RW_HD:2 * RW_HD]

    states = [hst[pi] for pi in range(npair)] if carry else None
    units = [(s, pi) for s in range(nsub) for pi in range(npair)]
    group = npair if carry else 2 * npair
    for g0 in range(0, len(units), group):
        grp_units = units[g0:g0 + group]
        sts, wrs, uus, upds = [], [], [], []
        for s, pi in grp_units:
            sts.append(states[pi] if carry else to_bd(h0_ref[s, pi]))
        for (s, pi), st in zip(grp_units, sts):
            sr, pc = slice(s * sub, (s + 1) * sub), slice(pi * pw_, (pi + 1) * pw_)
            wrs.append(_dot_nt(jnp.concatenate([w_sc[sr, pc], rt_sc[sr, pc]], axis=0), st))
        for (s, pi), wr in zip(grp_units, wrs):
            sr, pc = slice(s * sub, (s + 1) * sub), slice(pi * pw_, (pi + 1) * pw_)
            uu = -wr[0:sub, :] - y_sc[sr, pc]
            u_sc[sr, pc] = uu
            out_sc[sr, pc] = wr[sub:2 * sub, :]
            uus.append(uu)
        for (s, pi), uu in zip(grp_units, uus):
            sr, pc = slice(s * sub, (s + 1) * sub), slice(pi * pw_, (pi + 1) * pw_)
            upds.append(_dot_tn(jnp.concatenate([uu, v_sc[sr, pc]], axis=0),
                                jnp.concatenate([bt_sc[sr, pc], kt_sc[sr, pc]], axis=0)))
        for (s, pi), st, upd in zip(grp_units, sts, upds):
            pc = slice(pi * pw_, (pi + 1) * pw_)
            new = (st + jnp.where(bd_mask, upd, 0.0)) * etot_sc[s * sub:s * sub + 1, pc]
            if carry:
                states[pi] = new
            else:
                hout_ref[s, pi] = from_bd(new)
    if carry:
        for pi in range(npair):
            hst[pi] = states[pi]

    for h, cols in enumerate(heads):
        out_sc[:, cols] += _dot(jnp.concatenate([pb_sc[h], pk_sc[h]], axis=1),
                                jnp.concatenate([u_sc[:, cols], v_sc[:, cols]], axis=0))
    ones = ones_ref[...]
    o_all = out_sc[...]
    oc = o_all - _head_sums(o_all, ones) * (1.0 / RW_HD)
    var = _head_sums(oc * oc, ones) * (1.0 / RW_HD)
    on = oc * lax.rsqrt(var + RW_LN_EPS) * lng_ref[...] + lnb_ref[...]
    out_sc[...] = on + _head_sums(rkb_sc[...], ones) * v_sc[...]
    o_ref[...] = (out_sc[...] * gate).astype(o_ref.dtype)

    if carry:
        last_sc[...] = jnp.broadcast_to(p_ref[rows - 1:rows, :], (8, RW_IN))

        @pl.when(pl.program_id(1) == pl.num_programs(1) - 1)
        def _():
            for pi in range(npair):
                hout_ref[0, pi] = from_bd(hst[pi])


class _Group:
    def __init__(self, bsz, t, rows):
        self.bsz, self.t, self.rows = bsz, t, rows
        self.carry = t >= CHUNK
        self.nb = 1 if self.carry else rows // t
        self.nt = t // rows if self.carry else 1
        self.grid = (bsz // self.nb, self.nt)
        self.seg = CHUNK if self.carry else t

    def rows_spec(self, width, col_block):
        nt = self.nt
        return pl.BlockSpec((self.rows, width), lambda i, j: (i * nt + j, col_block))

    def state_spec(self, shape, layer=None):
        zeros = (0,) * len(shape)
        if layer is None:
            return pl.BlockSpec((self.nb,) + tuple(shape), lambda i, j: (i,) + zeros)
        return pl.BlockSpec((None, self.nb) + tuple(shape), lambda i, j: (layer, i) + zeros)

    def out_rows(self, width, dtype=BF16):
        nt = self.nt
        return (pl.BlockSpec((self.rows, width), lambda i, j: (i * nt + j, 0)),
                jax.ShapeDtypeStruct((self.bsz * self.t, width), dtype))


def _vec_spec(shape):
    zeros = (0,) * len(shape)
    return pl.BlockSpec(tuple(shape), lambda i, j: zeros)


def _retention(grp, p, cos_t, sin_t, h0, sl, gn):
    c = CHUNK
    shape = (RET_H, RET_HD, RET_HD)
    in_specs = [grp.rows_spec(GROUP, COL_RET // GROUP + n) for n in range(4)]
    tab = pl.BlockSpec((grp.rows, RET_HD), (lambda i, j: (j, 0)) if grp.carry else (lambda i, j: (0, 0)))
    in_specs += [tab, tab, _vec_spec((1, GROUP)), grp.state_spec(shape, sl)]
    o_spec, o_shape = grp.out_rows(GROUP)
    return pl.pallas_call(
        functools.partial(_ret_kernel, rows=grp.rows, seg=grp.seg, carry=grp.carry),
        grid=grp.grid,
        in_specs=in_specs,
        out_specs=[o_spec, grp.state_spec(shape)],
        out_shape=[o_shape, jax.ShapeDtypeStruct(h0.shape[1:], F32)],
        scratch_shapes=[pltpu.VMEM(shape, F32), pltpu.VMEM((c, RET_HD), F32),
                        pltpu.VMEM((c, RET_HD), F32), pltpu.VMEM((c, RET_HD), F32)],
        compiler_params=_cparams(2),
    )(p, p, p, p, cos_t, sin_t, gn.reshape(1, GROUP), h0)


def _mamba(grp, p, cprev, h0, sl, lp):
    c = CHUNK
    cshape, hshape = (MB_CONV - 1, MB_CONV_DIM), (MB_H, MB_HD, MB_N)
    h0 = jnp.swapaxes(h0, -1, -2)
    pad8 = lambda a: jnp.pad(a.reshape(1, MB_H), ((0, 0), (0, 128 - MB_H)))
    in_specs = [grp.rows_spec(GROUP, COL_MB_Z // GROUP), grp.rows_spec(MB_CONV_DIM, COL_MB_XBC // MB_CONV_DIM),
                grp.rows_spec(128, COL_MB_DT // 128),
                grp.state_spec(cshape, sl), grp.state_spec(hshape, sl),
                _vec_spec((MB_CONV, MB_CONV_DIM)), _vec_spec((1, MB_CONV_DIM)), _vec_spec((1, 128)),
                _vec_spec((1, 128)), _vec_spec((1, 128)), _vec_spec((1, GROUP))]
    o_spec, o_shape = grp.out_rows(GROUP)
    return pl.pallas_call(
        functools.partial(_mamba_kernel, rows=grp.rows, seg=grp.seg, carry=grp.carry),
        grid=grp.grid,
        in_specs=in_specs,
        out_specs=[o_spec, grp.state_spec(cshape), grp.state_spec(hshape)],
        out_shape=[o_shape, jax.ShapeDtypeStruct(cprev.shape[1:], F32), jax.ShapeDtypeStruct(h0.shape[1:], F32)],
        scratch_shapes=[pltpu.VMEM(hshape, F32),
                        pltpu.VMEM(((grp.rows if grp.carry else 0) + 8, MB_CONV_DIM), F32),
                        pltpu.VMEM((grp.rows, MB_CONV_DIM), F32),
                        pltpu.VMEM((c, MB_N), F32), pltpu.VMEM((c, MB_N), F32), pltpu.VMEM((c, MB_HD), F32),
                        pltpu.VMEM((c, MB_HD), F32), pltpu.VMEM((c, GROUP), F32),
                        pltpu.VMEM((c, 128), F32), pltpu.VMEM((c, 128), F32)],
        compiler_params=_cparams(2),
    )(p, p, p, cprev, h0, lp['mb_conv_w'], lp['mb_conv_b'].reshape(1, MB_CONV_DIM), pad8(lp['mb_dt_bias']),
      pad8(lp['mb_a_log']), pad8(lp['mb_d']), lp['mb_norm_g'].reshape(1, GROUP))


def _gla(grp, p, h0, sl, lp, expand):
    c = CHUNK
    shape = (GLA_H, GLA_DK, GLA_DV)
    sub = 16 if grp.carry else grp.seg
    gk_up = jnp.pad(lp['gla_gk_up'], ((0, 128 - GLA_LORA), (0, 0)))
    in_specs = [grp.rows_spec(GLA_QK, COL_GLA_Q // GLA_QK), grp.rows_spec(GLA_QK, COL_GLA_K // GLA_QK),
                grp.rows_spec(GROUP, COL_GLA_V // GROUP), grp.rows_spec(128, COL_GLA_GK // 128),
                grp.rows_spec(GROUP, COL_GLA_G // GROUP), grp.state_spec(shape, sl),
                _vec_spec((128, GLA_QK)), _vec_spec((1, GLA_QK)), _vec_spec((1, GROUP)), _vec_spec((GLA_QK, GROUP))]
    o_spec, o_shape = grp.out_rows(GROUP)
    return pl.pallas_call(
        functools.partial(_gla_kernel, rows=grp.rows, seg=sub, carry=grp.carry),
        grid=grp.grid,
        in_specs=in_specs,
        out_specs=[o_spec, grp.state_spec(shape)],
        out_shape=[o_shape, jax.ShapeDtypeStruct(h0.shape[1:], F32)],
        scratch_shapes=[pltpu.VMEM((GLA_H, GLA_DV, GLA_DK), F32)]
        + [pltpu.VMEM((c, GLA_QK), F32) for _ in range(3)]
        + [pltpu.VMEM((c // sub, sub * sub, GLA_QK), F32), pltpu.VMEM((c, GROUP), F32)],
        compiler_params=_cparams(2),
    )(p, p, p, p, p, h0, gk_up, lp['gla_gk_b'].reshape(1, GLA_QK), lp['gla_norm_g'].reshape(1, GROUP), expand)


def _rwkv(grp, p, shift_prev, h0, sl, lp):
    c = CHUNK
    head_ones = (jnp.arange(GROUP)[:, None] // RW_HD == jnp.arange(GROUP)[None, :] // RW_HD).astype(BF16)
    shape = (RW_H // 2, 2 * RW_HD, RW_HD)
    h0 = h0.reshape(h0.shape[:2] + shape)
    sub = 64 if grp.carry else grp.seg
    row = lambda a: a.reshape(1, -1)
    w_up = jnp.pad(lp['rw_w_up'], ((0, 64), (0, 0)))
    a_up = jnp.pad(lp['rw_a_up'], ((64, 0), (0, 0)))
    in_specs = [grp.rows_spec(RW_IN, 0), grp.state_spec((1, RW_IN), sl), grp.state_spec(shape, sl),
                _vec_spec((1, RW_IN)), _vec_spec((1, GROUP)), _vec_spec((128, GROUP)), _vec_spec((1, GROUP)),
                _vec_spec((128, GROUP)), _vec_spec((128, GROUP))] + [_vec_spec((1, GROUP))] * 5 + [_vec_spec((GROUP, GROUP))]
    o_spec, o_shape = grp.out_rows(GROUP)
    wide = lambda: pltpu.VMEM((c, GROUP), F32)
    return pl.pallas_call(
        functools.partial(_rwkv_kernel, rows=grp.rows, seg=grp.seg, sub=sub, carry=grp.carry),
        grid=grp.grid,
        in_specs=in_specs,
        out_specs=[o_spec, grp.state_spec(shape)],
        out_shape=[o_shape, jax.ShapeDtypeStruct(h0.shape[1:], F32)],
        scratch_shapes=[pltpu.VMEM((RW_H // 2, 2 * RW_HD, 2 * RW_HD), F32), pltpu.VMEM((8, RW_IN), F32),
                        pltpu.VMEM((c, RW_IN), F32)] + [wide() for _ in range(11)]
        + [pltpu.VMEM((2, RW_H, c, c), F32)] + [pltpu.VMEM((RW_H, c, c), F32) for _ in range(4)],
        compiler_params=_cparams(2),
    )(p, shift_prev.reshape(shift_prev.shape[0], -1, 1, RW_IN), h0, row(lp['rw_mu']), row(lp['rw_w0']), w_up,
      row(lp['rw_a0']), a_up, lp['rw_g_up'], row(lp['rw_k_k']), row(lp['rw_k_a']), row(lp['rw_r_k']),
      row(lp['rw_ln_g']), row(lp['rw_ln_b']), head_ones)


def _rope_tables(pos0, t):
    half = RET_HD // 2
    inv = 1.0 / (ROPE_BASE ** jnp.linspace(0.0, 1.0, half, dtype=F32))
    pos = pos0 + jnp.arange(t, dtype=F32)
    ang = pos[:, None] * inv[None, :]
    cos, sin = jnp.cos(ang), jnp.sin(ang)
    cos_t = jnp.stack([cos, cos], axis=-1).reshape(t, RET_HD)
    sin_t = jnp.stack([-sin, sin], axis=-1).reshape(t, RET_HD)
    return cos_t, sin_t


def _pad_w_in(w):
    z = lambda n: jnp.zeros(w.shape[:-1] + (n,), w.dtype)
    return jnp.concatenate([
        w[..., 0:1792], w[..., 3328:3336], z(120), w[..., 4360:4376], z(112), w[..., 2304:3328],
        w[..., 1792:2304], w[..., 3848:4360], w[..., 4376:4888], w[..., 3336:3592], w[..., 3592:3848],
        w[..., 4888:6936]], axis=-1)


def _layer(grp, grp_rw, x, x_bf, pos_tabs, states, sl, mem_k, mem_v, cache_layer, lp, big, wl, expand, tm):
    rw_shift, rw_state, mb_conv, mb_state, gla_state, ret_state = states
    bsz, t = grp.bsz, grp.t
    tm_ln = 512 if grp.carry else 256
    p = _matmul([x_bf], big['w_in'], F32, tm, 1024, wl)
    o_rw, rw_new = _rwkv(grp_rw, p, rw_shift, rw_state, sl, lp)
    rw_new = rw_new.reshape(bsz, RW_H, RW_HD, RW_HD)
    o_mb, conv_new, mb_new = _mamba(grp, p, mb_conv, mb_state, sl, lp)
    mb_new = jnp.swapaxes(mb_new, -1, -2)
    o_gl, gla_new = _gla(grp, p, gla_state, sl, lp, expand)
    o_rt, ret_new = _retention(grp, p, pos_tabs[0], pos_tabs[1], ret_state, sl, lp['ret_norm_g'])
    shift_new = p.reshape(bsz, t, N_PAD)[:, t - 1, 0:RW_IN]
    x, x_bf = _matmul_res_ln([o_rw, o_mb, o_gl, o_rt], big['w_out'], wl, x, lp['ln1_g'], lp['ln1_b'], tm_ln)
    q = _matmul([x_bf], big['xa_wq'], BF16 if grp.carry else F32, tm, 1024, wl)
    if grp.carry:
        att = _cross_attn(q, mem_k, mem_v, bsz, t, 1, 512, BF16)
    else:
        att = _cross_attn(q, mem_k, mem_v, bsz, t, 4, t, F32, layer=cache_layer)
    x, x_bf = _matmul_res_ln([att], big['xa_wo'], wl, x, lp['ln2_g'], lp['ln2_b'], tm_ln)
    hid = _ffn_gate_up(x_bf, big['ffn_w_gate'], big['ffn_w_up'], tm, 512, wl)
    x, x_bf = _matmul_res_ln([hid], big['ffn_w_down'], wl, x, lp['ln3_g'], lp['ln3_b'], 256)
    return x, x_bf, (shift_new, rw_new, conv_new, mb_new, gla_new, ret_new)


def kernel(x_prompt, x_sample, state_rwkv_shift, state_rwkv_wkv, state_mamba_conv, state_mamba_ssm, state_gla,
           state_ret, cache_mem_k, cache_mem_v, mem_prompt, w_in, w_out, ln1_g, ln1_b, rw_mu, rw_w0, rw_w_up,
           rw_a0, rw_a_up, rw_g_up, rw_k_k, rw_k_a, rw_r_k, rw_ln_g, rw_ln_b, mb_conv_w, mb_conv_b, mb_dt_bias,
           mb_a_log, mb_d, mb_norm_g, gla_gk_up, gla_gk_b, gla_norm_g, ret_norm_g, ln2_g, ln2_b, xa_wq, xa_wk,
           xa_wv, xa_wo, ln3_g, ln3_b, ffn_w_gate, ffn_w_up, ffn_w_down):
    small = dict(
        ln1_g=ln1_g, ln1_b=ln1_b, rw_mu=rw_mu, rw_w0=rw_w0, rw_w_up=rw_w_up, rw_a0=rw_a0,
        rw_a_up=rw_a_up, rw_g_up=rw_g_up, rw_k_k=rw_k_k, rw_k_a=rw_k_a, rw_r_k=rw_r_k, rw_ln_g=rw_ln_g,
        rw_ln_b=rw_ln_b, mb_conv_w=mb_conv_w, mb_conv_b=mb_conv_b, mb_dt_bias=mb_dt_bias, mb_a_log=mb_a_log,
        mb_d=mb_d, mb_norm_g=mb_norm_g, gla_gk_up=gla_gk_up, gla_gk_b=gla_gk_b, gla_norm_g=gla_norm_g,
        ret_norm_g=ret_norm_g, ln2_g=ln2_g, ln2_b=ln2_b, ln3_g=ln3_g, ln3_b=ln3_b)
    big = dict(w_in=_pad_w_in(w_in), w_out=w_out, xa_wq=xa_wq, xa_wo=xa_wo, ffn_w_gate=ffn_w_gate,
               ffn_w_up=ffn_w_up, ffn_w_down=ffn_w_down.astype(BF16))
    bp, tp, _ = x_prompt.shape
    bs, ts, _ = x_sample.shape
    gp = _Group(bp, tp, 512)
    gp_rw = _Group(bp, tp, CHUNK)
    gs = _Group(bs, ts, CHUNK)
    expand = (jnp.arange(GLA_QK)[:, None] // GLA_DK == jnp.arange(GROUP)[None, :] // GLA_DV).astype(BF16)
    tabs_p = _rope_tables(0.0, tp)
    tabs_s = tuple(jnp.tile(tb, (gs.nb, 1)) for tb in _rope_tables(float(PAST_LEN), ts))
    zeros_p = (jnp.zeros((1, bp, RW_IN), F32), jnp.zeros((1, bp, RW_H, RW_HD, RW_HD), F32),
               jnp.zeros((1, bp, MB_CONV - 1, MB_CONV_DIM), F32), jnp.zeros((1, bp, MB_H, MB_N, MB_HD), F32),
               jnp.zeros((1, bp, GLA_H, GLA_DK, GLA_DV), F32), jnp.zeros((1, bp, RET_H, RET_HD, RET_HD), F32))
    st_s_in = (state_rwkv_shift, state_rwkv_wkv, state_mamba_conv, state_mamba_ssm, state_gla, state_ret)

    yp = x_prompt.reshape(bp * tp, D_MODEL)
    ys = x_sample.reshape(bs * ts, D_MODEL)
    yp_bf, ys_bf = yp.astype(BF16), ys.astype(BF16)
    mem_bf = mem_prompt.reshape(bp * N_MEM, D_MODEL).astype(BF16)
    outs_p = [[] for _ in range(8)]
    outs_s = [[] for _ in range(6)]
    for i in range(DEPTH):
        lp = {name: val[i] for name, val in small.items()}
        mk = _matmul([mem_bf], xa_wk, F32, 1024, 512, i)
        mv = _matmul([mem_bf], xa_wv, F32, 1024, 512, i)
        yp, yp_bf, st_p = _layer(gp, gp_rw, yp, yp_bf, tabs_p, zeros_p, 0, mk.reshape(bp, N_MEM, D_MODEL),
                                 mv.reshape(bp, N_MEM, D_MODEL), None, lp, big, i, expand, 1024)
        for lst, val in zip(outs_p, st_p + (mk.reshape(bp, N_MEM, XA_H, XA_HD), mv.reshape(bp, N_MEM, XA_H, XA_HD))):
            lst.append(val)
        ys, ys_bf, st_s = _layer(gs, gs, ys, ys_bf, tabs_s, st_s_in, i, cache_mem_k, cache_mem_v, i, lp, big, i,
                                 expand, 1024)
        for lst, val in zip(outs_s, st_s):
            lst.append(val)
    return (yp.reshape(bp, tp, D_MODEL), ys.reshape(bs, ts, D_MODEL),
            *[jnp.stack(v) for v in outs_p], *[jnp.stack(v) for v in outs_s])
```

```python
import functools
import math

import jax
import jax.numpy as jnp
from jax import lax
from jax.experimental import pallas as pl
from jax.experimental.pallas import tpu as pltpu

F32 = jnp.float32
BF16 = jnp.bfloat16

D_MODEL = 2048
DEPTH = 2
PAST_LEN = 16384
GROUP = 512
RW_H, RW_HD = 8, 64
RW_IN = 1792
RW_LN_EPS = 64e-5
MB_H, MB_HD, MB_N, MB_G = 8, 64, 128, 2
MB_CONV = 4
MB_CONV_DIM = 1024
GLA_H, GLA_DK, GLA_DV = 4, 64, 128
GLA_QK = 256
GLA_LORA = 16
GLA_TAU = 16.0
GLA_SAFE_EXP = 60.0
RET_H, RET_HD = 4, 128
ROPE_BASE = 10000.0
N_MEM = 256
XA_H, XA_HD = 4, 512
D_FF = 5632
DN_ALPHA = (2 * DEPTH) ** 0.25
RET_LOG_GAMMA = tuple(math.log1p(-(2.0 ** (-5.0 - h))) for h in range(RET_H))

N_PAD = 7168
COL_MB_DT = 1792
COL_GLA_GK = 1920
COL_MB_XBC = 2048
COL_MB_Z = 3072
COL_GLA_V = 3584
COL_GLA_G = 4096
COL_GLA_Q = 4608
COL_GLA_K = 4864
COL_RET = 5120

CHUNK = 128
VMEM_LIMIT = 56 * 1024 * 1024


def _cparams(n_axes, vmem=VMEM_LIMIT):
    return pltpu.CompilerParams(dimension_semantics=("arbitrary",) * n_axes, vmem_limit_bytes=vmem)


def _dot(a, b):
    return jnp.dot(a.astype(BF16), b.astype(BF16), preferred_element_type=F32)


def _dot_nt(a, b):
    return lax.dot_general(a.astype(BF16), b.astype(BF16), (((1,), (1,)), ((), ())), preferred_element_type=F32)


def _dot_tn(a, b):
    return lax.dot_general(a.astype(BF16), b.astype(BF16), (((0,), (0,)), ((), ())), preferred_element_type=F32)


def _sigmoid(x):
    return 1.0 / (1.0 + jnp.exp(-x))


def _silu(x):
    return x * _sigmoid(x)


def _softplus(x):
    return jnp.maximum(x, 0.0) + jnp.log(1.0 + jnp.exp(-jnp.abs(x)))


def _split_dot(m_bf16, x):
    hi = x.astype(BF16)
    r1 = x - hi.astype(F32)
    mid = r1.astype(BF16)
    lo = (r1 - mid.astype(F32)).astype(BF16)
    return (jnp.dot(m_bf16, hi, preferred_element_type=F32)
            + jnp.dot(m_bf16, mid, preferred_element_type=F32)
            + jnp.dot(m_bf16, lo, preferred_element_type=F32))


def _head_sums(x, ones_bf16):
    hi = x.astype(BF16)
    lo = (x - hi.astype(F32)).astype(BF16)
    return (jnp.dot(hi, ones_bf16, preferred_element_type=F32)
            + jnp.dot(lo, ones_bf16, preferred_element_type=F32))


def _mo(x, m):
    return x if isinstance(x, int) else pl.multiple_of(x, m)


def _seg_masks(c, seg):
    sh = jnp.int32(int(math.log2(seg)))
    r = lax.broadcasted_iota(jnp.int32, (c, c), 0)
    q = lax.broadcasted_iota(jnp.int32, (c, c), 1)
    same = lax.shift_right_arithmetic(r, sh) == lax.shift_right_arithmetic(q, sh)
    incl = jnp.logical_and(same, r >= q)
    strict = jnp.logical_and(same, r > q)
    return incl, strict, same, r, q


def _mm_kernel(*refs, k_sizes):
    n_x = len(k_sizes)
    x_refs, w_ref, o_ref, wbf = refs[:n_x], refs[n_x], refs[n_x + 1], refs[n_x + 2]

    @pl.when(pl.program_id(1) == 0)
    def _():
        wbf[...] = w_ref[...].astype(BF16)

    acc = None
    off = 0
    for xr, ks in zip(x_refs, k_sizes):
        part = jnp.dot(xr[...].astype(BF16), wbf[off:off + ks, :], preferred_element_type=F32)
        acc = part if acc is None else acc + part
        off += ks
    o_ref[...] = acc.astype(o_ref.dtype)


def _w_spec(w, tn, layer):
    k = w.shape[-2]
    if w.ndim == 2:
        return pl.BlockSpec((k, tn), lambda j, i: (0, j))
    return pl.BlockSpec((None, k, tn), lambda j, i: (layer, 0, j))


def _matmul(xs, w, out_dtype, tm, tn, layer=None):
    m = xs[0].shape[0]
    k, n = w.shape[-2:]
    k_sizes = tuple(x.shape[1] for x in xs)
    assert sum(k_sizes) == k and m % tm == 0 and n % tn == 0
    in_specs = [pl.BlockSpec((tm, ks), lambda j, i: (i, 0)) for ks in k_sizes]
    in_specs.append(_w_spec(w, tn, layer))
    return pl.pallas_call(
        functools.partial(_mm_kernel, k_sizes=k_sizes),
        grid=(n // tn, m // tm),
        in_specs=in_specs,
        out_specs=pl.BlockSpec((tm, tn), lambda j, i: (i, j)),
        out_shape=jax.ShapeDtypeStruct((m, n), out_dtype),
        scratch_shapes=[pltpu.VMEM((k, tn), BF16)],
        compiler_params=_cparams(2),
    )(*xs, w)


def _ffn_gu_kernel(x_ref, wg_ref, wu_ref, o_ref, wg_bf, wu_bf):
    @pl.when(pl.program_id(1) == 0)
    def _():
        wg_bf[...] = wg_ref[...].astype(BF16)
        wu_bf[...] = wu_ref[...].astype(BF16)

    x = x_ref[...]
    gate = jnp.dot(x, wg_bf[...], preferred_element_type=F32)
    up = jnp.dot(x, wu_bf[...], preferred_element_type=F32)
    o_ref[...] = (_silu(gate) * up).astype(o_ref.dtype)


def _ffn_gate_up(x_bf, wg, wu, tm, tn, layer):
    m, k = x_bf.shape
    n = wg.shape[-1]
    return pl.pallas_call(
        _ffn_gu_kernel,
        grid=(n // tn, m // tm),
        in_specs=[pl.BlockSpec((tm, k), lambda j, i: (i, 0)), _w_spec(wg, tn, layer), _w_spec(wu, tn, layer)],
        out_specs=pl.BlockSpec((tm, tn), lambda j, i: (i, j)),
        out_shape=jax.ShapeDtypeStruct((m, n), BF16),
        scratch_shapes=[pltpu.VMEM((k, tn), BF16), pltpu.VMEM((k, tn), BF16)],
        compiler_params=_cparams(2),
    )(x_bf, wg, wu)


def _mm_ln_kernel(*refs, k_sizes, cast_w):
    n_x = len(k_sizes)
    x_refs = refs[:n_x]
    w_ref, r_ref, g_ref, b_ref, of_ref, ob_ref = refs[n_x:n_x + 6]
    if cast_w:
        wbf = refs[n_x + 6]

        @pl.when(pl.program_id(0) == 0)
        def _():
            wbf[...] = w_ref[...].astype(BF16)
    else:
        wbf = w_ref
    acc = None
    off = 0
    for xr, ks in zip(x_refs, k_sizes):
        part = jnp.dot(xr[...].astype(BF16), wbf[off:off + ks, :], preferred_element_type=F32)
        acc = part if acc is None else acc + part
        off += ks
    z = DN_ALPHA * r_ref[...] + acc
    zc = z - jnp.mean(z, axis=-1, keepdims=True)
    var = jnp.mean(zc * zc, axis=-1, keepdims=True)
    out = zc * lax.rsqrt(var + 1e-5) * g_ref[...] + b_ref[...]
    of_ref[...] = out
    ob_ref[...] = out.astype(BF16)


def _matmul_res_ln(xs, w, layer, resid, g, b, tm):
    m, n = resid.shape
    k = w.shape[-2]
    k_sizes = tuple(x.shape[1] for x in xs)
    cast_w = w.dtype != BF16
    assert sum(k_sizes) == k and m % tm == 0 and w.shape[-1] == n
    row = pl.BlockSpec((tm, n), lambda i: (i, 0))
    vec = pl.BlockSpec((1, n), lambda i: (0, 0))
    in_specs = [pl.BlockSpec((tm, ks), lambda i: (i, 0)) for ks in k_sizes]
    in_specs.append(pl.BlockSpec((None, k, n), lambda i: (layer, 0, 0), pipeline_mode=pl.Buffered(1)))
    in_specs += [row, vec, vec]
    return pl.pallas_call(
        functools.partial(_mm_ln_kernel, k_sizes=k_sizes, cast_w=cast_w),
        grid=(m // tm,),
        in_specs=in_specs,
        out_specs=[row, row],
        out_shape=[jax.ShapeDtypeStruct((m, n), F32), jax.ShapeDtypeStruct((m, n), BF16)],
        scratch_shapes=[pltpu.VMEM((k, n), BF16)] if cast_w else [],
        compiler_params=_cparams(1),
    )(*xs, w, resid, g.reshape(1, n), b.reshape(1, n))


def _xattn_kernel(q_ref, k_ref, v_ref, o_ref, *, nb, tq):
    for j in range(nb):
        rows = slice(j * tq, (j + 1) * tq)
        for h in range(XA_H):
            cols = slice(h * XA_HD, (h + 1) * XA_HD)
            s = _dot_nt(q_ref[rows, cols], k_ref[j, :, cols]) * (XA_HD ** -0.5)
            e = jnp.exp(s - jnp.max(s, axis=-1, keepdims=True))
            pr = e / jnp.sum(e, axis=-1, keepdims=True)
            o_ref[rows, cols] = _dot(pr, v_ref[j, :, cols]).astype(o_ref.dtype)


def _xattn_cache_kernel(q_ref, k_ref, v_ref, o_ref, *, nb, tq):
    nr = N_MEM * XA_H
    r = lax.broadcasted_iota(jnp.int32, (nr, XA_H * tq), 0)
    q = lax.broadcasted_iota(jnp.int32, (nr, XA_H * tq), 1)
    own = (r & (XA_H - 1)) == lax.shift_right_arithmetic(q, jnp.int32(int(math.log2(tq))))
    for j in range(nb):
        rows = slice(j * tq, (j + 1) * tq)
        kf = k_ref[j].reshape(nr, XA_HD)
        vf = v_ref[j].reshape(nr, XA_HD)
        qcat = jnp.concatenate([q_ref[rows, h * XA_HD:(h + 1) * XA_HD] for h in range(XA_H)], axis=0)
        s = jnp.where(own, _dot_nt(kf, qcat) * (XA_HD ** -0.5), -jnp.inf)
        e = jnp.exp(s - jnp.max(s, axis=0, keepdims=True))
        pr = e / jnp.sum(e, axis=0, keepdims=True)
        o = _dot_tn(pr, vf)
        for h in range(XA_H):
            o_ref[rows, h * XA_HD:(h + 1) * XA_HD] = o[h * tq:(h + 1) * tq, :].astype(o_ref.dtype)


def _cross_attn(q, mem_k, mem_v, bsz, t, nb, tq, out_dtype, layer=None):
    nt = t // tq
    rows = nb * tq
    if layer is None:
        body = _xattn_kernel
        kv_spec = pl.BlockSpec((nb, N_MEM, D_MODEL), lambda i, j: (i, 0, 0))
    else:
        body = _xattn_cache_kernel
        kv_spec = pl.BlockSpec((None, nb, N_MEM, XA_H, XA_HD), lambda i, j: (layer, i, 0, 0, 0))
    return pl.pallas_call(
        functools.partial(body, nb=nb, tq=tq),
        grid=(bsz // nb, nt),
        in_specs=[pl.BlockSpec((rows, D_MODEL), lambda i, j: (i * nt + j, 0)), kv_spec, kv_spec],
        out_specs=pl.BlockSpec((rows, D_MODEL), lambda i, j: (i * nt + j, 0)),
        out_shape=jax.ShapeDtypeStruct((bsz * t, D_MODEL), out_dtype),
        compiler_params=_cparams(2),
    )(q, mem_k, mem_v)


def _state_io(carry, h0_ref, hout_ref, hst):
    if carry:
        return (lambda s, h: hst[h]), (lambda s, h, val: hst.__setitem__(h, val))
    return (lambda s, h: h0_ref[s, h]), (lambda s, h, val: hout_ref.__setitem__((s, h), val))


def _for_segments(nseg, fn, unroll=True):
    if nseg == 1:
        fn(0)
    else:
        def body(s, c):
            fn(s)
            return c
        lax.fori_loop(0, nseg, body, 0, unroll=unroll)


def _ret_kernel(q_ref, k_ref, v_ref, g_ref, cos_ref, sin_ref, gn_ref, h0_ref, o_ref, hout_ref,
                hst, qs_sc, ks_sc, y_sc, *, rows, seg, carry):
    c = CHUNK
    nseg = c // seg
    if carry:
        @pl.when(pl.program_id(1) == 0)
        def _():
            hst[...] = h0_ref[0]
    get_h, set_h = _state_io(carry, h0_ref, hout_ref, hst)
    incl, _, _, r_i, c_i = _seg_masks(c, seg)
    dpos = (r_i - c_i).astype(F32)
    tau = (lax.broadcasted_iota(jnp.int32, (c, RET_HD), 0) & (seg - 1)).astype(F32)
    even = (lax.broadcasted_iota(jnp.int32, (c, GROUP), 1) & 1) == 0

    for ci in range(rows // c):
        rws = slice(ci * c, (ci + 1) * c)
        cosb = jnp.concatenate([cos_ref[rws, :]] * RET_H, axis=1)
        sinb = jnp.concatenate([sin_ref[rws, :]] * RET_H, axis=1)

        def rot(x):
            swapped = jnp.where(even, pltpu.roll(x, GROUP - 1, 1), pltpu.roll(x, 1, 1))
            return x * cosb + swapped * sinb

        qr = rot(q_ref[rws, :])
        kr = rot(k_ref[rws, :]) * (RET_HD ** -0.5)
        for h in range(RET_H):
            lgam = RET_LOG_GAMMA[h]
            cols = slice(h * RET_HD, (h + 1) * RET_HD)
            qh, kh = qr[:, cols], kr[:, cols]
            dm = jnp.where(incl, jnp.exp(dpos * lgam), 0.0)
            y_sc[...] = _dot(_dot_nt(qh, kh) * dm, v_ref[rws, cols])
            qs_sc[...] = qh * jnp.exp((tau + 1.0) * lgam)
            ks_sc[...] = kh * jnp.exp((seg - 1.0 - tau) * lgam)
            cd = math.exp(seg * lgam)

            def seg_step(s, h=h, cols=cols, cd=cd, ci=ci):
                sr = pl.ds(_mo(s * seg, seg), seg)
                vr = pl.ds(_mo(ci * c + s * seg, seg), seg)
                hs = get_h(s, h)
                y_sc[sr, :] += _dot(qs_sc[sr, :], hs)
                set_h(s, h, cd * hs + _dot_tn(ks_sc[sr, :], v_ref[vr, cols]))

            _for_segments(nseg, seg_step)
            y = y_sc[...]
            yn = y * lax.rsqrt(jnp.mean(y * y, axis=-1, keepdims=True) + 1e-5) * gn_ref[:, cols]
            o_ref[rws, cols] = (yn * _silu(g_ref[rws, cols])).astype(o_ref.dtype)

    if carry:
        @pl.when(pl.program_id(1) == pl.num_programs(1) - 1)
        def _():
            hout_ref[0] = hst[...]


def _mamba_kernel(z_ref, xbc_ref, dt_ref, cprev_ref, h0_ref, cw_ref, cb_ref, dtb_ref, alog_ref, dd_ref,
                  ng_ref, o_ref, cout_ref, hout_ref,
                  hst, xpad, act_sc, qs_sc, ks_sc, xs_sc, y_sc, yf_sc, ecum_sc, etot_sc, *, rows, seg, carry):
    c = CHUNK
    nseg = c // seg
    get_h, set_h = _state_io(carry, h0_ref, hout_ref, hst)

    def conv_act(window, n):
        acc = cb_ref[...] + window[5:5 + n] * cw_ref[0:1, :]
        for i in range(1, MB_CONV):
            acc = acc + window[5 + i:5 + i + n] * cw_ref[i:i + 1, :]
        return _silu(acc)

    if carry:
        @pl.when(pl.program_id(1) == 0)
        def _():
            hst[...] = h0_ref[0]
            xpad[0:8, :] = jnp.zeros((8, MB_CONV_DIM), F32)
            xpad[5:8, :] = cprev_ref[0]

        xpad[8:8 + rows, :] = xbc_ref[...]
        for ci in range(rows // c):
            acc = cb_ref[...] + xpad[ci * c + 5:ci * c + 5 + c, :] * cw_ref[0:1, :]
            for i in range(1, MB_CONV):
                acc = acc + xpad[ci * c + 5 + i:ci * c + 5 + i + c, :] * cw_ref[i:i + 1, :]
            act_sc[ci * c:(ci + 1) * c, :] = _silu(acc)
        xpad[0:8, :] = xpad[rows:rows + 8, :]

        @pl.when(pl.program_id(1) == pl.num_programs(1) - 1)
        def _():
            cout_ref[0] = xpad[5:8, :]
    else:
        def conv_seq(s, carry_):
            sr = pl.ds(pl.multiple_of(s * seg, seg), seg)
            xpad[5:8, :] = cprev_ref[s]
            xs = xbc_ref[sr, :]
            window = jnp.concatenate([xpad[0:8, :], xs], axis=0)
            act_sc[sr, :] = conv_act(window, seg)
            cout_ref[s] = xs[seg - 3:seg]
            return carry_

        xpad[0:8, :] = jnp.zeros((8, MB_CONV_DIM), F32)
        lax.fori_loop(0, rows // seg, conv_seq, 0)

    incl, _, same, _, _ = _seg_masks(c, seg)
    lt_bf = jnp.where(incl, 1.0, 0.0).astype(BF16)
    same_bf = jnp.where(same, 1.0, 0.0).astype(BF16)
    a_neg = -jnp.exp(alog_ref[...])

    for ci in range(rows // c):
        rws = slice(ci * c, (ci + 1) * c)
        dtv = _softplus(dt_ref[rws, :] + dtb_ref[...])
        la = dtv * a_neg
        cum = _split_dot(lt_bf, la)
        tot = _split_dot(same_bf, la)
        cum_t = cum.T
        ecum_sc[...] = jnp.exp(cum)
        etot_sc[...] = jnp.exp(tot)
        edec = jnp.exp(tot - cum)
        for g in range(MB_G):
            cg = act_sc[rws, 768 + g * MB_N:768 + (g + 1) * MB_N]
            bg = act_sc[rws, 512 + g * MB_N:512 + (g + 1) * MB_N]
            gmat = _dot_nt(cg, bg)
            qs_sc[...] = cg
            ks_sc[...] = bg
            for hh in range(MB_H // MB_G):
                h = g * (MB_H // MB_G) + hh
                cols = slice(h * MB_HD, (h + 1) * MB_HD)
                lmat = jnp.exp(jnp.where(incl, cum[:, h:h + 1] - cum_t[h:h + 1, :], -jnp.inf))
                xh = act_sc[rws, cols]
                xdt = xh * dtv[:, h:h + 1]
                y_sc[...] = _dot(gmat * lmat, xdt)
                xs_sc[...] = xdt * edec[:, h:h + 1]

                def seg_step(s, h=h):
                    sr = pl.ds(_mo(s * seg, seg), seg)
                    first = pl.ds(_mo(s * seg, seg), 1)
                    hs = get_h(s, h)
                    y_sc[sr, :] += _dot_nt(qs_sc[sr, :], hs) * ecum_sc[sr, h:h + 1]
                    set_h(s, h, etot_sc[first, h:h + 1] * hs + _dot_tn(xs_sc[sr, :], ks_sc[sr, :]))

                _for_segments(nseg, seg_step)
                yf_sc[:, cols] = y_sc[...] + dd_ref[:, h:h + 1] * xh
        yz = yf_sc[...] * _silu(z_ref[rws, :])
        gw = GROUP // MB_G
        for g in range(MB_G):
            cols = slice(g * gw, (g + 1) * gw)
            part = yz[:, cols]
            nrm = part * lax.rsqrt(jnp.mean(part * part, axis=-1, keepdims=True) + 1e-5) * ng_ref[:, cols]
            o_ref[rws, cols] = nrm.astype(o_ref.dtype)

    if carry:
        @pl.when(pl.program_id(1) == pl.num_programs(1) - 1)
        def _():
            hout_ref[0] = hst[...]


def _gla_kernel(q_ref, k_ref, v_ref, gk_ref, g_ref, h0_ref, gkup_ref, gkb_ref, ng_ref, e_ref,
                o_ref, hout_ref,
                hst, qe_sc, ke_sc, etot_sc, p_sc, y_sc, *, rows, seg, carry):
    c = CHUNK
    nsub = c // seg
    if carry:
        @pl.when(pl.program_id(1) == 0)
        def _():
            for h in range(GLA_H):
                hst[h] = h0_ref[0, h].T
    incl, _, same, _, _ = _seg_masks(c, seg)
    lt_bf = jnp.where(incl, 1.0, 0.0).astype(BF16)
    same_bf = jnp.where(same, 1.0, 0.0).astype(BF16)
    row_i = lax.broadcasted_iota(jnp.int32, (seg, GLA_QK), 0)
    kcs = [slice(h * GLA_DK, (h + 1) * GLA_DK) for h in range(GLA_H)]
    vcs = [slice(h * GLA_DV, (h + 1) * GLA_DV) for h in range(GLA_H)]

    for ci in range(rows // c):
        rws = slice(ci * c, (ci + 1) * c)
        pre = _dot(gk_ref[rws, :], gkup_ref[...]) + gkb_ref[...]
        la = -_softplus(-pre) * (1.0 / GLA_TAU)
        cum = _split_dot(lt_bf, la)
        tot = _split_dot(same_bf, la)
        qv = q_ref[rws, :] * (GLA_DK ** -0.5)
        kv = k_ref[rws, :]
        qe_sc[...] = qv * jnp.exp(cum)
        ke_sc[...] = kv * jnp.exp(tot - cum)
        etot_sc[...] = jnp.exp(tot)

        safe = (jnp.max(-cum) < GLA_SAFE_EXP) if carry else False

        def factored():
            kinv = kv * jnp.exp(-cum)
            for h in range(GLA_H):
                sc = jnp.where(incl, _dot_nt(qe_sc[:, kcs[h]], kinv[:, kcs[h]]), 0.0)
                y_sc[:, vcs[h]] = _dot(sc, v_ref[rws, vcs[h]])

        def pairwise():
            for u in range(nsub):
                sr = slice(u * seg, (u + 1) * seg)
                cu, qu = cum[sr], qv[sr]
                for j in range(seg):
                    r0 = u * seg + j
                    pj = qu * jnp.exp(jnp.where(row_i >= j, cu - cum[r0:r0 + 1], -jnp.inf)) * kv[r0:r0 + 1]
                    p_sc[u, j * seg:(j + 1) * seg, :] = pj
            for u in range(nsub):
                rm = _dot(p_sc[u], e_ref[...])
                y = rm[0:seg, :] * v_ref[ci * c + u * seg:ci * c + u * seg + 1, :]
                for j in range(1, seg):
                    r0 = ci * c + u * seg + j
                    y = y + rm[j * seg:(j + 1) * seg, :] * v_ref[r0:r0 + 1, :]
                y_sc[u * seg:(u + 1) * seg, :] = y

        if carry:
            pl.when(safe)(factored)
            pl.when(jnp.logical_not(safe))(pairwise)
        else:
            pairwise()
        states = [hst[h] for h in range(GLA_H)] if carry else None
        for u in range(nsub):
            sr = slice(u * seg, (u + 1) * seg)
            vr = slice(ci * c + u * seg, ci * c + (u + 1) * seg)
            hts = [states[h] if carry else h0_ref[u, h].T for h in range(GLA_H)]
            for h in range(GLA_H):
                y_sc[sr, vcs[h]] += _dot_nt(qe_sc[sr, kcs[h]], hts[h])
            upds = [_dot_tn(v_ref[vr, vcs[h]], ke_sc[sr, kcs[h]]) for h in range(GLA_H)]
            for h in range(GLA_H):
                new = hts[h] * etot_sc[u * seg:u * seg + 1, kcs[h]] + upds[h]
                if carry:
                    states[h] = new
                else:
                    hout_ref[u, h] = new.T
        if carry:
            for h in range(GLA_H):
                hst[h] = states[h]

        for h in range(GLA_H):
            y = y_sc[:, vcs[h]]
            yn = y * lax.rsqrt(jnp.mean(y * y, axis=-1, keepdims=True) + 1e-5) * ng_ref[:, vcs[h]]
            o_ref[rws, vcs[h]] = (yn * _silu(g_ref[rws, vcs[h]])).astype(o_ref.dtype)

    if carry:
        @pl.when(pl.program_id(1) == pl.num_programs(1) - 1)
        def _():
            for h in range(GLA_H):
                hout_ref[0, h] = hst[h].T


def _rwkv_kernel(p_ref, sp_ref, h0_ref, mu_ref, w0_ref, wup_ref, a0_ref, aup_ref, gup_ref, kk_ref, ka_ref,
                 rk_ref, lng_ref, lnb_ref, ones_ref, o_ref, hout_ref,
                 hst, last_sc, prev_sc, w_sc, y_sc, bt_sc, kt_sc, v_sc, rt_sc, u_sc, rkb_sc, etot_sc, out_sc, kap_sc,
                 pw_sc, t_sc, ak_sc, pb_sc, pk_sc, *, rows, seg, sub, carry):
    c = CHUNK
    assert rows == c
    nsub = c // sub
    n_iter = int(math.log2(sub)) - 1
    npair = RW_H // 2
    pw_ = 2 * RW_HD
    pr_i = lax.broadcasted_iota(jnp.int32, (pw_, pw_), 0)
    pc_i = lax.broadcasted_iota(jnp.int32, (pw_, pw_), 1)
    bd_mask = (pr_i >= RW_HD) == (pc_i >= RW_HD)
    low_rows = lax.broadcasted_iota(jnp.int32, (pw_, RW_HD), 0) < RW_HD

    def to_bd(stack):
        return jnp.where(bd_mask, jnp.concatenate([stack, stack], axis=1), 0.0)

    def from_bd(bd):
        return jnp.where(low_rows, bd[:, 0:RW_HD], bd[:, RW_HD:pw_])

    if carry:
        @pl.when(pl.program_id(1) == 0)
        def _():
            for pi in range(npair):
                hst[pi] = to_bd(h0_ref[0, pi])
            last_sc[...] = jnp.broadcast_to(sp_ref[0], (8, RW_IN))

        prev_row = jnp.broadcast_to(last_sc[0:1, :], (c, RW_IN))
    else:
        for s in range(rows // seg):
            prev_sc[s * seg:(s + 1) * seg, :] = jnp.broadcast_to(sp_ref[s], (seg, RW_IN))
        prev_row = prev_sc[...]
    incl, strict, same, r_i, c_i = _seg_masks(c, sub)
    lt_bf = jnp.where(incl, 1.0, 0.0).astype(BF16)
    same_bf = jnp.where(same, 1.0, 0.0).astype(BF16)
    eye = jnp.where(r_i == c_i, 1.0, 0.0)
    row_w = lax.broadcasted_iota(jnp.int32, (c, RW_IN), 0)
    first_row = (row_w & ((c if carry else seg) - 1)) == 0

    p = p_ref[...]
    prev = jnp.where(first_row, prev_row, pltpu.roll(p, 1, 0))
    xs = p + (prev - p) * mu_ref[...]
    r = xs[:, 0:GROUP]
    k = xs[:, GROUP:2 * GROUP]
    v = xs[:, 2 * GROUP:3 * GROUP]
    wa = xs[:, 3 * GROUP:3 * GROUP + 128]
    gd = xs[:, 3 * GROUP + 128:RW_IN]
    w = -_softplus(-(w0_ref[...] + _dot(jnp.tanh(wa), wup_ref[...]))) - 0.5
    ld = -jnp.exp(w)
    a = _sigmoid(a0_ref[...] + _dot(wa, aup_ref[...]))
    gate = _dot(_sigmoid(gd), gup_ref[...])
    kkr = k * kk_ref[...]
    k2 = k * (1.0 + (a - 1.0) * ka_ref[...])
    cum = _split_dot(lt_bf, ld)
    tot = _split_dot(same_bf, ld)
    ecum = jnp.exp(cum)
    einv = jnp.exp(-cum)
    eprev = jnp.exp(cum - ld)
    rkb_sc[...] = r * k2 * rk_ref[...]
    etot_sc[...] = jnp.exp(tot)
    v_sc[...] = v
    rtil = r * ecum
    rt_sc[...] = rtil
    kt_sc[...] = k2 * einv
    rn = lax.rsqrt(jnp.maximum(_head_sums(kkr * kkr, ones_ref[...]), 1e-24))
    braw = kkr * a * einv * rn
    kraw = kkr * eprev * rn

    heads = [slice(h * RW_HD, (h + 1) * RW_HD) for h in range(RW_H)]
    for h, cols in enumerate(heads):
        kap = kraw[:, cols]
        bt = braw[:, cols]
        kap_sc[:, cols] = kap
        bt_sc[:, cols] = bt
        m1 = _dot_nt(jnp.concatenate([kap, rtil[:, cols]], axis=0), jnp.concatenate([bt, kt_sc[:, cols]], axis=0))
        x = jnp.where(strict, -m1[0:c, 0:c], 0.0)
        pw_sc[0, h] = x
        t_sc[h] = eye + x
        ak_sc[h] = jnp.where(strict, m1[0:c, c:2 * c], 0.0)
        pb_sc[h] = jnp.where(incl, m1[c:2 * c, 0:c], 0.0)
        pk_sc[h] = jnp.where(incl, m1[c:2 * c, c:2 * c], 0.0)
    for h, cols in enumerate(heads):
        y_sc[:, cols] = _dot(ak_sc[h], v_sc[:, cols])
    for it in range(n_iter):
        src, dst = it % 2, (it + 1) % 2
        for h in range(RW_H):
            pw = pw_sc[src, h]
            pw_sc[dst, h] = _dot(pw, pw)
        for h in range(RW_H):
            tm = t_sc[h]
            t_sc[h] = tm + _dot(tm, pw_sc[dst, h])
    for h, cols in enumerate(heads):
        wy = _dot(t_sc[h], jnp.concatenate([kap_sc[:, cols], y_sc[:, cols]], axis=1))
        w_sc[:, cols] = wy[:, 0:RW_HD]
        y_sc[:, cols] = wy[:, RW_HD:2 * RW_HD]

    states = [hst[pi] for pi in range(npair)] if carry else None
    units = [(s, pi) for s in range(nsub) for pi in range(npair)]
    group = npair if carry else 2 * npair
    for g0 in range(0, len(units), group):
        grp_units = units[g0:g0 + group]
        sts, wrs, uus, upds = [], [], [], []
        for s, pi in grp_units:
            sts.append(states[pi] if carry else to_bd(h0_ref[s, pi]))
        for (s, pi), st in zip(grp_units, sts):
            sr, pc = slice(s * sub, (s + 1) * sub), slice(pi * pw_, (pi + 1) * pw_)
            wrs.append(_dot_nt(jnp.concatenate([w_sc[sr, pc], rt_sc[sr, pc]], axis=0), st))
        for (s, pi), wr in zip(grp_units, wrs):
            sr, pc = slice(s * sub, (s + 1) * sub), slice(pi * pw_, (pi + 1) * pw_)
            uu = -wr[0:sub, :] - y_sc[sr, pc]
            u_sc[sr, pc] = uu
            out_sc[sr, pc] = wr[sub:2 * sub, :]
            uus.append(uu)
        for (s, pi), uu in zip(grp_units, uus):
            sr, pc = slice(s * sub, (s + 1) * sub), slice(pi * pw_, (pi + 1) * pw_)
            upds.append(_dot_tn(jnp.concatenate([uu, v_sc[sr, pc]], axis=0),
                                jnp.concatenate([bt_sc[sr, pc], kt_sc[sr, pc]], axis=0)))
        for (s, pi), st, upd in zip(grp_units, sts, upds):
            pc = slice(pi * pw_, (pi + 1) * pw_)
            new = (st + jnp.where(bd_mask, upd, 0.0)) * etot_sc[s * sub:s * sub + 1, pc]
            if carry:
                states[pi] = new
            else:
                hout_ref[s, pi] = from_bd(new)
    if carry:
        for pi in range(npair):
            hst[pi] = states[pi]

    for h, cols in enumerate(heads):
        out_sc[:, cols] += _dot(jnp.concatenate([pb_sc[h], pk_sc[h]], axis=1),
                                jnp.concatenate([u_sc[:, cols], v_sc[:, cols]], axis=0))
    ones = ones_ref[...]
    o_all = out_sc[...]
    oc = o_all - _head_sums(o_all, ones) * (1.0 / RW_HD)
    var = _head_sums(oc * oc, ones) * (1.0 / RW_HD)
    on = oc * lax.rsqrt(var + RW_LN_EPS) * lng_ref[...] + lnb_ref[...]
    out_sc[...] = on + _head_sums(rkb_sc[...], ones) * v_sc[...]
    o_ref[...] = (out_sc[...] * gate).astype(o_ref.dtype)

    if carry:
        last_sc[...] = jnp.broadcast_to(p_ref[rows - 1:rows, :], (8, RW_IN))

        @pl.when(pl.program_id(1) == pl.num_programs(1) - 1)
        def _():
            for pi in range(npair):
                hout_ref[0, pi] = from_bd(hst[pi])


class _Group:
    def __init__(self, bsz, t, rows):
        self.bsz, self.t, self.rows = bsz, t, rows
        self.carry = t >= CHUNK
        self.nb = 1 if self.carry else rows // t
        self.nt = t // rows if self.carry else 1
        self.grid = (bsz // self.nb, self.nt)
        self.seg = CHUNK if self.carry else t

    def rows_spec(self, width, col_block):
        nt = self.nt
        return pl.BlockSpec((self.rows, width), lambda i, j: (i * nt + j, col_block))

    def state_spec(self, shape, layer=None):
        zeros = (0,) * len(shape)
        if layer is None:
            return pl.BlockSpec((self.nb,) + tuple(shape), lambda i, j: (i,) + zeros)
        return pl.BlockSpec((None, self.nb) + tuple(shape), lambda i, j: (layer, i) + zeros)

    def out_rows(self, width, dtype=BF16):
        nt = self.nt
        return (pl.BlockSpec((self.rows, width), lambda i, j: (i * nt + j, 0)),
                jax.ShapeDtypeStruct((self.bsz * self.t, width), dtype))


def _vec_spec(shape):
    zeros = (0,) * len(shape)
    return pl.BlockSpec(tuple(shape), lambda i, j: zeros)


def _retention(grp, p, cos_t, sin_t, h0, sl, gn):
    c = CHUNK
    shape = (RET_H, RET_HD, RET_HD)
    in_specs = [grp.rows_spec(GROUP, COL_RET // GROUP + n) for n in range(4)]
    tab = pl.BlockSpec((grp.rows, RET_HD), (lambda i, j: (j, 0)) if grp.carry else (lambda i, j: (0, 0)))
    in_specs += [tab, tab, _vec_spec((1, GROUP)), grp.state_spec(shape, sl)]
    o_spec, o_shape = grp.out_rows(GROUP)
    return pl.pallas_call(
        functools.partial(_ret_kernel, rows=grp.rows, seg=grp.seg, carry=grp.carry),
        grid=grp.grid,
        in_specs=in_specs,
        out_specs=[o_spec, grp.state_spec(shape)],
        out_shape=[o_shape, jax.ShapeDtypeStruct(h0.shape[1:], F32)],
        scratch_shapes=[pltpu.VMEM(shape, F32), pltpu.VMEM((c, RET_HD), F32),
                        pltpu.VMEM((c, RET_HD), F32), pltpu.VMEM((c, RET_HD), F32)],
        compiler_params=_cparams(2),
    )(p, p, p, p, cos_t, sin_t, gn.reshape(1, GROUP), h0)


def _mamba(grp, p, cprev, h0, sl, lp):
    c = CHUNK
    cshape, hshape = (MB_CONV - 1, MB_CONV_DIM), (MB_H, MB_HD, MB_N)
    h0 = jnp.swapaxes(h0, -1, -2)
    pad8 = lambda a: jnp.pad(a.reshape(1, MB_H), ((0, 0), (0, 128 - MB_H)))
    in_specs = [grp.rows_spec(GROUP, COL_MB_Z // GROUP), grp.rows_spec(MB_CONV_DIM, COL_MB_XBC // MB_CONV_DIM),
                grp.rows_spec(128, COL_MB_DT // 128),
                grp.state_spec(cshape, sl), grp.state_spec(hshape, sl),
                _vec_spec((MB_CONV, MB_CONV_DIM)), _vec_spec((1, MB_CONV_DIM)), _vec_spec((1, 128)),
                _vec_spec((1, 128)), _vec_spec((1, 128)), _vec_spec((1, GROUP))]
    o_spec, o_shape = grp.out_rows(GROUP)
    return pl.pallas_call(
        functools.partial(_mamba_kernel, rows=grp.rows, seg=grp.seg, carry=grp.carry),
        grid=grp.grid,
        in_specs=in_specs,
        out_specs=[o_spec, grp.state_spec(cshape), grp.state_spec(hshape)],
        out_shape=[o_shape, jax.ShapeDtypeStruct(cprev.shape[1:], F32), jax.ShapeDtypeStruct(h0.shape[1:], F32)],
        scratch_shapes=[pltpu.VMEM(hshape, F32),
                        pltpu.VMEM(((grp.rows if grp.carry else 0) + 8, MB_CONV_DIM), F32),
                        pltpu.VMEM((grp.rows, MB_CONV_DIM), F32),
                        pltpu.VMEM((c, MB_N), F32), pltpu.VMEM((c, MB_N), F32), pltpu.VMEM((c, MB_HD), F32),
                        pltpu.VMEM((c, MB_HD), F32), pltpu.VMEM((c, GROUP), F32),
                        pltpu.VMEM((c, 128), F32), pltpu.VMEM((c, 128), F32)],
        compiler_params=_cparams(2),
    )(p, p, p, cprev, h0, lp['mb_conv_w'], lp['mb_conv_b'].reshape(1, MB_CONV_DIM), pad8(lp['mb_dt_bias']),
      pad8(lp['mb_a_log']), pad8(lp['mb_d']), lp['mb_norm_g'].reshape(1, GROUP))


def _gla(grp, p, h0, sl, lp, expand):
    c = CHUNK
    shape = (GLA_H, GLA_DK, GLA_DV)
    sub = 64 if grp.carry else grp.seg
    gk_up =jnp.pad(lp['gla_gk_up'], ((0, 128 - GLA_LORA), (0, 0)))
    in_specs = [grp.rows_spec(GLA_QK, COL_GLA_Q // GLA_QK), grp.rows_spec(GLA_QK, COL_GLA_K // GLA_QK),
                grp.rows_spec(GROUP, COL_GLA_V // GROUP), grp.rows_spec(128, COL_GLA_GK // 128),
                grp.rows_spec(GROUP, COL_GLA_G // GROUP), grp.state_spec(shape, sl),
                _vec_spec((128, GLA_QK)), _vec_spec((1, GLA_QK)), _vec_spec((1, GROUP)), _vec_spec((GLA_QK, GROUP))]
    o_spec, o_shape = grp.out_rows(GROUP)
    return pl.pallas_call(
        functools.partial(_gla_kernel, rows=grp.rows, seg=sub, carry=grp.carry),
        grid=grp.grid,
        in_specs=in_specs,
        out_specs=[o_spec, grp.state_spec(shape)],
        out_shape=[o_shape, jax.ShapeDtypeStruct(h0.shape[1:], F32)],
        scratch_shapes=[pltpu.VMEM((GLA_H, GLA_DV, GLA_DK), F32)]
        + [pltpu.VMEM((c, GLA_QK), F32) for _ in range(3)]
        + [pltpu.VMEM((c // sub, sub * sub, GLA_QK), F32), pltpu.VMEM((c, GROUP), F32)],
        compiler_params=_cparams(2),
    )(p, p, p, p, p, h0, gk_up, lp['gla_gk_b'].reshape(1, GLA_QK), lp['gla_norm_g'].reshape(1, GROUP), expand)


def _rwkv(grp, p, shift_prev, h0, sl, lp):
    c = CHUNK
    head_ones = (jnp.arange(GROUP)[:, None] // RW_HD == jnp.arange(GROUP)[None, :] // RW_HD).astype(BF16)
    shape = (RW_H // 2, 2 * RW_HD, RW_HD)
    h0 = h0.reshape(h0.shape[:2] + shape)
    sub = 64 if grp.carry else grp.seg
    row = lambda a: a.reshape(1, -1)
    w_up = jnp.pad(lp['rw_w_up'], ((0, 64), (0, 0)))
    a_up = jnp.pad(lp['rw_a_up'], ((64, 0), (0, 0)))
    in_specs = [grp.rows_spec(RW_IN, 0), grp.state_spec((1, RW_IN), sl), grp.state_spec(shape, sl),
                _vec_spec((1, RW_IN)), _vec_spec((1, GROUP)), _vec_spec((128, GROUP)), _vec_spec((1, GROUP)),
                _vec_spec((128, GROUP)), _vec_spec((128, GROUP))] + [_vec_spec((1, GROUP))] * 5 + [_vec_spec((GROUP, GROUP))]
    o_spec, o_shape = grp.out_rows(GROUP)
    wide = lambda: pltpu.VMEM((c, GROUP), F32)
    return pl.pallas_call(
        functools.partial(_rwkv_kernel, rows=grp.rows, seg=grp.seg, sub=sub, carry=grp.carry),
        grid=grp.grid,
        in_specs=in_specs,
        out_specs=[o_spec, grp.state_spec(shape)],
        out_shape=[o_shape, jax.ShapeDtypeStruct(h0.shape[1:], F32)],
        scratch_shapes=[pltpu.VMEM((RW_H // 2, 2 * RW_HD, 2 * RW_HD), F32), pltpu.VMEM((8, RW_IN), F32),
                        pltpu.VMEM((c, RW_IN), F32)] + [wide() for _ in range(11)]
        + [pltpu.VMEM((2, RW_H, c, c), F32)] + [pltpu.VMEM((RW_H, c, c), F32) for _ in range(4)],
        compiler_params=_cparams(2),
    )(p, shift_prev.reshape(shift_prev.shape[0], -1, 1, RW_IN), h0, row(lp['rw_mu']), row(lp['rw_w0']), w_up,
      row(lp['rw_a0']), a_up, lp['rw_g_up'], row(lp['rw_k_k']), row(lp['rw_k_a']), row(lp['rw_r_k']),
      row(lp['rw_ln_g']), row(lp['rw_ln_b']), head_ones)


def _rope_tables(pos0, t):
    half = RET_HD // 2
    inv = 1.0 / (ROPE_BASE ** jnp.linspace(0.0, 1.0, half, dtype=F32))
    pos = pos0 + jnp.arange(t, dtype=F32)
    ang = pos[:, None] * inv[None, :]
    cos, sin = jnp.cos(ang), jnp.sin(ang)
    cos_t = jnp.stack([cos, cos], axis=-1).reshape(t, RET_HD)
    sin_t = jnp.stack([-sin, sin], axis=-1).reshape(t, RET_HD)
    return cos_t, sin_t


def _pad_w_in(w):
    z = lambda n: jnp.zeros(w.shape[:-1] + (n,), w.dtype)
    return jnp.concatenate([
        w[..., 0:1792], w[..., 3328:3336], z(120), w[..., 4360:4376], z(112), w[..., 2304:3328],
        w[..., 1792:2304], w[..., 3848:4360], w[..., 4376:4888], w[..., 3336:3592], w[..., 3592:3848],
        w[..., 4888:6936]], axis=-1)


def _layer(grp, grp_rw, x, x_bf, pos_tabs, states, sl, mem_k, mem_v, cache_layer, lp, big, wl, expand, tm):
    rw_shift, rw_state, mb_conv, mb_state, gla_state, ret_state = states
    bsz, t = grp.bsz, grp.t
    tm_ln = 512 if grp.carry else 256
    p = _matmul([x_bf], big['w_in'], F32, tm, 1024, wl)
    o_rw, rw_new = _rwkv(grp_rw, p, rw_shift, rw_state, sl, lp)
    rw_new = rw_new.reshape(bsz, RW_H, RW_HD, RW_HD)
    o_mb, conv_new, mb_new = _mamba(grp, p, mb_conv, mb_state, sl, lp)
    mb_new = jnp.swapaxes(mb_new, -1, -2)
    o_gl, gla_new = _gla(grp, p, gla_state, sl, lp, expand)
    o_rt, ret_new = _retention(grp, p, pos_tabs[0], pos_tabs[1], ret_state, sl, lp['ret_norm_g'])
    shift_new = p.reshape(bsz, t, N_PAD)[:, t - 1, 0:RW_IN]
    x, x_bf = _matmul_res_ln([o_rw, o_mb, o_gl, o_rt], big['w_out'], wl, x, lp['ln1_g'], lp['ln1_b'], tm_ln)
    q = _matmul([x_bf], big['xa_wq'], BF16 if grp.carry else F32, tm, 1024, wl)
    if grp.carry:
        att = _cross_attn(q, mem_k, mem_v, bsz, t, 1, 512, BF16)
    else:
        att = _cross_attn(q, mem_k, mem_v, bsz, t, 4, t, F32, layer=cache_layer)
    x, x_bf = _matmul_res_ln([att], big['xa_wo'], wl, x, lp['ln2_g'], lp['ln2_b'], tm_ln)
    hid = _ffn_gate_up(x_bf, big['ffn_w_gate'], big['ffn_w_up'], tm, 512, wl)
    x, x_bf = _matmul_res_ln([hid], big['ffn_w_down'], wl, x, lp['ln3_g'], lp['ln3_b'], 256)
    return x, x_bf, (shift_new, rw_new, conv_new, mb_new, gla_new, ret_new)


def kernel(x_prompt, x_sample, state_rwkv_shift, state_rwkv_wkv, state_mamba_conv, state_mamba_ssm, state_gla,
           state_ret, cache_mem_k, cache_mem_v, mem_prompt, w_in, w_out, ln1_g, ln1_b, rw_mu, rw_w0, rw_w_up,
           rw_a0, rw_a_up, rw_g_up, rw_k_k, rw_k_a, rw_r_k, rw_ln_g, rw_ln_b, mb_conv_w, mb_conv_b, mb_dt_bias,
           mb_a_log, mb_d, mb_norm_g, gla_gk_up, gla_gk_b, gla_norm_g, ret_norm_g, ln2_g, ln2_b, xa_wq, xa_wk,
           xa_wv, xa_wo, ln3_g, ln3_b, ffn_w_gate, ffn_w_up, ffn_w_down):
    small = dict(
        ln1_g=ln1_g, ln1_b=ln1_b, rw_mu=rw_mu, rw_w0=rw_w0, rw_w_up=rw_w_up, rw_a0=rw_a0,
        rw_a_up=rw_a_up, rw_g_up=rw_g_up, rw_k_k=rw_k_k, rw_k_a=rw_k_a, rw_r_k=rw_r_k, rw_ln_g=rw_ln_g,
        rw_ln_b=rw_ln_b, mb_conv_w=mb_conv_w, mb_conv_b=mb_conv_b, mb_dt_bias=mb_dt_bias, mb_a_log=mb_a_log,
        mb_d=mb_d, mb_norm_g=mb_norm_g, gla_gk_up=gla_gk_up, gla_gk_b=gla_gk_b, gla_norm_g=gla_norm_g,
        ret_norm_g=ret_norm_g, ln2_g=ln2_g, ln2_b=ln2_b, ln3_g=ln3_g, ln3_b=ln3_b)
    big = dict(w_in=_pad_w_in(w_in), w_out=w_out, xa_wq=xa_wq, xa_wo=xa_wo, ffn_w_gate=ffn_w_gate,
               ffn_w_up=ffn_w_up, ffn_w_down=ffn_w_down.astype(BF16))
    bp, tp, _ = x_prompt.shape
    bs, ts, _ = x_sample.shape
    gp = _Group(bp, tp, 512)
    gp_rw = _Group(bp, tp, CHUNK)
    gs = _Group(bs, ts, CHUNK)
    expand = (jnp.arange(GLA_QK)[:, None] // GLA_DK == jnp.arange(GROUP)[None, :] // GLA_DV).astype(BF16)
    tabs_p = _rope_tables(0.0, tp)
    tabs_s = tuple(jnp.tile(tb, (gs.nb, 1)) for tb in _rope_tables(float(PAST_LEN), ts))
    zeros_p = (jnp.zeros((1, bp, RW_IN), F32), jnp.zeros((1, bp, RW_H, RW_HD, RW_HD), F32),
               jnp.zeros((1, bp, MB_CONV - 1, MB_CONV_DIM), F32), jnp.zeros((1, bp, MB_H, MB_N, MB_HD), F32),
               jnp.zeros((1, bp, GLA_H, GLA_DK, GLA_DV), F32), jnp.zeros((1, bp, RET_H, RET_HD, RET_HD), F32))
    st_s_in = (state_rwkv_shift, state_rwkv_wkv, state_mamba_conv, state_mamba_ssm, state_gla, state_ret)

    yp = x_prompt.reshape(bp * tp, D_MODEL)
    ys = x_sample.reshape(bs * ts, D_MODEL)
    yp_bf, ys_bf = yp.astype(BF16), ys.astype(BF16)
    mem_bf = mem_prompt.reshape(bp * N_MEM, D_MODEL).astype(BF16)
    outs_p = [[] for _ in range(8)]
    outs_s = [[] for _ in range(6)]
    for i in range(DEPTH):
        lp = {name: val[i] for name, val in small.items()}
        mk = _matmul([mem_bf], xa_wk, F32, 1024, 512, i)
        mv = _matmul([mem_bf], xa_wv, F32, 1024, 512, i)
        yp, yp_bf, st_p = _layer(gp, gp_rw, yp, yp_bf, tabs_p, zeros_p, 0, mk.reshape(bp, N_MEM, D_MODEL),
                                 mv.reshape(bp, N_MEM, D_MODEL), None, lp, big, i, expand, 1024)
        for lst, val in zip(outs_p, st_p + (mk.reshape(bp, N_MEM, XA_H, XA_HD), mv.reshape(bp, N_MEM, XA_H, XA_HD))):
            lst.append(val)
        ys, ys_bf, st_s = _layer(gs, gs, ys, ys_bf, tabs_s, st_s_in, i, cache_mem_k, cache_mem_v, i, lp, big, i,
                                 expand, 1024)
        for lst, val in zip(outs_s, st_s):
            lst.append(val)
    return (yp.reshape(bp, tp, D_MODEL), ys.reshape(bs, ts, D_MODEL),
            *[jnp.stack(v) for v in outs_p], *[jnp.stack(v) for v in outs_s])
```

```python
import functools
import math

import jax
import jax.numpy as jnp
from jax import lax
from jax.experimental import pallas as pl
from jax.experimental.pallas import tpu as pltpu

F32 = jnp.float32
BF16 = jnp.bfloat16

D_MODEL = 2048
DEPTH = 2
PAST_LEN = 16384
GROUP = 512
RW_H, RW_HD = 8, 64
RW_IN = 1792
RW_LN_EPS = 64e-5
MB_H, MB_HD, MB_N, MB_G = 8, 64, 128, 2
MB_CONV = 4
MB_CONV_DIM = 1024
GLA_H, GLA_DK, GLA_DV = 4, 64, 128
GLA_QK = 256
GLA_LORA = 16
GLA_TAU = 16.0
GLA_SAFE_EXP = 60.0
RET_H, RET_HD = 4, 128
ROPE_BASE = 10000.0
N_MEM = 256
XA_H, XA_HD = 4, 512
D_FF = 5632
DN_ALPHA = (2 * DEPTH) ** 0.25
RET_LOG_GAMMA = tuple(math.log1p(-(2.0 ** (-5.0 - h))) for h in range(RET_H))

N_PAD = 7168
COL_MB_DT = 1792
COL_GLA_GK = 1920
COL_MB_XBC = 2048
COL_MB_Z = 3072
COL_GLA_V = 3584
COL_GLA_G = 4096
COL_GLA_Q = 4608
COL_GLA_K = 4864
COL_RET = 5120

CHUNK = 128
VMEM_LIMIT = 56 * 1024 * 1024


def _cparams(n_axes, vmem=VMEM_LIMIT):
    return pltpu.CompilerParams(dimension_semantics=("arbitrary",) * n_axes, vmem_limit_bytes=vmem)


def _dot(a, b):
    return jnp.dot(a.astype(BF16), b.astype(BF16), preferred_element_type=F32)


def _dot_nt(a, b):
    return lax.dot_general(a.astype(BF16), b.astype(BF16), (((1,), (1,)), ((), ())), preferred_element_type=F32)


def _dot_tn(a, b):
    return lax.dot_general(a.astype(BF16), b.astype(BF16), (((0,), (0,)), ((), ())), preferred_element_type=F32)


def _sigmoid(x):
    return 1.0 / (1.0 + jnp.exp(-x))


def _silu(x):
    return x * _sigmoid(x)


def _softplus(x):
    return jnp.maximum(x, 0.0) + jnp.log(1.0 + jnp.exp(-jnp.abs(x)))


def _split_dot(m_bf16, x):
    hi = x.astype(BF16)
    r1 = x - hi.astype(F32)
    mid = r1.astype(BF16)
    lo = (r1 - mid.astype(F32)).astype(BF16)
    return (jnp.dot(m_bf16, hi, preferred_element_type=F32)
            + jnp.dot(m_bf16, mid, preferred_element_type=F32)
            + jnp.dot(m_bf16, lo, preferred_element_type=F32))


def _spread(x, sel_bf16):
    hi = x.astype(BF16)
    r1 = x - hi.astype(F32)
    mid = r1.astype(BF16)
    lo = (r1 - mid.astype(F32)).astype(BF16)
    return (jnp.dot(hi, sel_bf16, preferred_element_type=F32)
            + jnp.dot(mid, sel_bf16, preferred_element_type=F32)
            + jnp.dot(lo, sel_bf16, preferred_element_type=F32))


def _head_sums(x, ones_bf16):
    hi = x.astype(BF16)
    lo = (x - hi.astype(F32)).astype(BF16)
    return (jnp.dot(hi, ones_bf16, preferred_element_type=F32)
            + jnp.dot(lo, ones_bf16, preferred_element_type=F32))


def _mo(x, m):
    return x if isinstance(x, int) else pl.multiple_of(x, m)


def _seg_masks(c, seg):
    sh = jnp.int32(int(math.log2(seg)))
    r = lax.broadcasted_iota(jnp.int32, (c, c), 0)
    q = lax.broadcasted_iota(jnp.int32, (c, c), 1)
    same = lax.shift_right_arithmetic(r, sh) == lax.shift_right_arithmetic(q, sh)
    incl = jnp.logical_and(same, r >= q)
    strict = jnp.logical_and(same, r > q)
    return incl, strict, same, r, q


def _mm_kernel(*refs, k_sizes):
    n_x = len(k_sizes)
    x_refs, w_ref, o_ref, wbf = refs[:n_x], refs[n_x], refs[n_x + 1], refs[n_x + 2]

    @pl.when(pl.program_id(1) == 0)
    def _():
        wbf[...] = w_ref[...].astype(BF16)

    acc = None
    off = 0
    for xr, ks in zip(x_refs, k_sizes):
        part = jnp.dot(xr[...].astype(BF16), wbf[off:off + ks, :], preferred_element_type=F32)
        acc = part if acc is None else acc + part
        off += ks
    o_ref[...] = acc.astype(o_ref.dtype)


def _w_spec(w, tn, layer):
    k = w.shape[-2]
    if w.ndim == 2:
        return pl.BlockSpec((k, tn), lambda j, i: (0, j))
    return pl.BlockSpec((None, k, tn), lambda j, i: (layer, 0, j))


def _matmul(xs, w, out_dtype, tm, tn, layer=None):
    m = xs[0].shape[0]
    k, n = w.shape[-2:]
    k_sizes = tuple(x.shape[1] for x in xs)
    assert sum(k_sizes) == k and m % tm == 0 and n % tn == 0
    in_specs = [pl.BlockSpec((tm, ks), lambda j, i: (i, 0)) for ks in k_sizes]
    in_specs.append(_w_spec(w, tn, layer))
    return pl.pallas_call(
        functools.partial(_mm_kernel, k_sizes=k_sizes),
        grid=(n // tn, m // tm),
        in_specs=in_specs,
        out_specs=pl.BlockSpec((tm, tn), lambda j, i: (i, j)),
        out_shape=jax.ShapeDtypeStruct((m, n), out_dtype),
        scratch_shapes=[pltpu.VMEM((k, tn), BF16)],
        compiler_params=_cparams(2),
    )(*xs, w)


def _ffn_gu_kernel(x_ref, wg_ref, wu_ref, o_ref, wg_bf, wu_bf):
    @pl.when(pl.program_id(1) == 0)
    def _():
        wg_bf[...] = wg_ref[...].astype(BF16)
        wu_bf[...] = wu_ref[...].astype(BF16)

    x = x_ref[...]
    gate = jnp.dot(x, wg_bf[...], preferred_element_type=F32)
    up = jnp.dot(x, wu_bf[...], preferred_element_type=F32)
    o_ref[...] = (_silu(gate) * up).astype(o_ref.dtype)


def _ffn_gate_up(x_bf, wg, wu, tm, tn, layer):
    m, k = x_bf.shape
    n = wg.shape[-1]
    return pl.pallas_call(
        _ffn_gu_kernel,
        grid=(n // tn, m // tm),
        in_specs=[pl.BlockSpec((tm, k), lambda j, i: (i, 0)), _w_spec(wg, tn, layer), _w_spec(wu, tn, layer)],
        out_specs=pl.BlockSpec((tm, tn), lambda j, i: (i, j)),
        out_shape=jax.ShapeDtypeStruct((m, n), BF16),
        scratch_shapes=[pltpu.VMEM((k, tn), BF16), pltpu.VMEM((k, tn), BF16)],
        compiler_params=_cparams(2),
    )(x_bf, wg, wu)


def _mm_ln_kernel(*refs, k_sizes, cast_w):
    n_x = len(k_sizes)
    x_refs = refs[:n_x]
    w_ref, r_ref, g_ref, b_ref, of_ref, ob_ref = refs[n_x:n_x + 6]
    if cast_w:
        wbf = refs[n_x + 6]

        @pl.when(pl.program_id(0) == 0)
        def _():
            wbf[...] = w_ref[...].astype(BF16)
    else:
        wbf = w_ref
    acc = None
    off = 0
    for xr, ks in zip(x_refs, k_sizes):
        part = jnp.dot(xr[...].astype(BF16), wbf[off:off + ks, :], preferred_element_type=F32)
        acc = part if acc is None else acc + part
        off += ks
    z = DN_ALPHA * r_ref[...] + acc
    zc = z - jnp.mean(z, axis=-1, keepdims=True)
    var = jnp.mean(zc * zc, axis=-1, keepdims=True)
    out = zc * lax.rsqrt(var + 1e-5) * g_ref[...] + b_ref[...]
    of_ref[...] = out
    ob_ref[...] = out.astype(BF16)


def _matmul_res_ln(xs, w, layer, resid, g, b, tm):
    m, n = resid.shape
    k = w.shape[-2]
    k_sizes = tuple(x.shape[1] for x in xs)
    cast_w = w.dtype != BF16
    assert sum(k_sizes) == k and m % tm == 0 and w.shape[-1] == n
    row = pl.BlockSpec((tm, n), lambda i: (i, 0))
    vec = pl.BlockSpec((1, n), lambda i: (0, 0))
    in_specs = [pl.BlockSpec((tm, ks), lambda i: (i, 0)) for ks in k_sizes]
    in_specs.append(pl.BlockSpec((None, k, n), lambda i: (layer, 0, 0), pipeline_mode=pl.Buffered(1)))
    in_specs += [row, vec, vec]
    return pl.pallas_call(
        functools.partial(_mm_ln_kernel, k_sizes=k_sizes, cast_w=cast_w),
        grid=(m // tm,),
        in_specs=in_specs,
        out_specs=[row, row],
        out_shape=[jax.ShapeDtypeStruct((m, n), F32), jax.ShapeDtypeStruct((m, n), BF16)],
        scratch_shapes=[pltpu.VMEM((k, n), BF16)] if cast_w else [],
        compiler_params=_cparams(1),
    )(*xs, w, resid, g.reshape(1, n), b.reshape(1, n))


def _xattn_kernel(q_ref, k_ref, v_ref, o_ref, *, nb, tq):
    for j in range(nb):
        rows = slice(j * tq, (j + 1) * tq)
        for h in range(XA_H):
            cols = slice(h * XA_HD, (h + 1) * XA_HD)
            s = _dot_nt(q_ref[rows, cols], k_ref[j, :, cols]) * (XA_HD ** -0.5)
            e = jnp.exp(s - jnp.max(s, axis=-1, keepdims=True))
            pr = e / jnp.sum(e, axis=-1, keepdims=True)
            o_ref[rows, cols] = _dot(pr, v_ref[j, :, cols]).astype(o_ref.dtype)


def _xattn_cache_kernel(q_ref, k_ref, v_ref, o_ref, *, nb, tq):
    nr = N_MEM * XA_H
    r = lax.broadcasted_iota(jnp.int32, (nr, XA_H * tq), 0)
    q = lax.broadcasted_iota(jnp.int32, (nr, XA_H * tq), 1)
    own = (r & (XA_H - 1)) == lax.shift_right_arithmetic(q, jnp.int32(int(math.log2(tq))))
    for j in range(nb):
        rows = slice(j * tq, (j + 1) * tq)
        kf = k_ref[j].reshape(nr, XA_HD)
        vf = v_ref[j].reshape(nr, XA_HD)
        qcat = jnp.concatenate([q_ref[rows, h * XA_HD:(h + 1) * XA_HD] for h in range(XA_H)], axis=0)
        s = jnp.where(own, _dot_nt(kf, qcat) * (XA_HD ** -0.5), -jnp.inf)
        e = jnp.exp(s - jnp.max(s, axis=0, keepdims=True))
        pr = e / jnp.sum(e, axis=0, keepdims=True)
        o = _dot_tn(pr, vf)
        for h in range(XA_H):
            o_ref[rows, h * XA_HD:(h + 1) * XA_HD] = o[h * tq:(h + 1) * tq, :].astype(o_ref.dtype)


def _cross_attn(q, mem_k, mem_v, bsz, t, nb, tq, out_dtype, layer=None):
    nt = t // tq
    rows = nb * tq
    if layer is None:
        body = _xattn_kernel
        kv_spec = pl.BlockSpec((nb, N_MEM, D_MODEL), lambda i, j: (i, 0, 0))
    else:
        body = _xattn_cache_kernel
        kv_spec = pl.BlockSpec((None, nb, N_MEM, XA_H, XA_HD), lambda i, j: (layer, i, 0, 0, 0))
    return pl.pallas_call(
        functools.partial(body, nb=nb, tq=tq),
        grid=(bsz // nb, nt),
        in_specs=[pl.BlockSpec((rows, D_MODEL), lambda i, j: (i * nt + j, 0)), kv_spec, kv_spec],
        out_specs=pl.BlockSpec((rows, D_MODEL), lambda i, j: (i * nt + j, 0)),
        out_shape=jax.ShapeDtypeStruct((bsz * t, D_MODEL), out_dtype),
        compiler_params=_cparams(2),
    )(q, mem_k, mem_v)


def _state_io(carry, h0_ref, hout_ref, hst):
    if carry:
        return (lambda s, h: hst[h]), (lambda s, h, val: hst.__setitem__(h, val))
    return (lambda s, h: h0_ref[s, h]), (lambda s, h, val: hout_ref.__setitem__((s, h), val))


def _for_segments(nseg, fn, unroll=True):
    if nseg == 1:
        fn(0)
    else:
        def body(s, c):
            fn(s)
            return c
        lax.fori_loop(0, nseg, body, 0, unroll=unroll)


def _ret_kernel(q_ref, k_ref, v_ref, g_ref, cos_ref, sin_ref, gn_ref, h0_ref, o_ref, hout_ref,
                hst, qs_sc, ks_sc, y_sc, *, rows, seg, carry):
    c = CHUNK
    nseg = c // seg
    if carry:
        @pl.when(pl.program_id(1) == 0)
        def _():
            hst[...] = h0_ref[0]
    get_h, set_h = _state_io(carry, h0_ref, hout_ref, hst)
    incl, _, _, r_i, c_i = _seg_masks(c, seg)
    dpos = (r_i - c_i).astype(F32)
    tau = (lax.broadcasted_iota(jnp.int32, (c, RET_HD), 0) & (seg - 1)).astype(F32)
    even = (lax.broadcasted_iota(jnp.int32, (c, GROUP), 1) & 1) == 0

    for ci in range(rows // c):
        rws = slice(ci * c, (ci + 1) * c)
        cosb = jnp.concatenate([cos_ref[rws, :]] * RET_H, axis=1)
        sinb = jnp.concatenate([sin_ref[rws, :]] * RET_H, axis=1)

        def rot(x):
            swapped = jnp.where(even, pltpu.roll(x, GROUP - 1, 1), pltpu.roll(x, 1, 1))
            return x * cosb + swapped * sinb

        qr = rot(q_ref[rws, :])
        kr = rot(k_ref[rws, :]) * (RET_HD ** -0.5)
        for h in range(RET_H):
            lgam = RET_LOG_GAMMA[h]
            cols = slice(h * RET_HD, (h + 1) * RET_HD)
            qh, kh = qr[:, cols], kr[:, cols]
            dm = jnp.where(incl, jnp.exp(dpos * lgam), 0.0)
            y_sc[...] = _dot(_dot_nt(qh, kh) * dm, v_ref[rws, cols])
            qs_sc[...] = qh * jnp.exp((tau + 1.0) * lgam)
            ks_sc[...] = kh * jnp.exp((seg - 1.0 - tau) * lgam)
            cd = math.exp(seg * lgam)

            def seg_step(s, h=h, cols=cols, cd=cd, ci=ci):
                sr = pl.ds(_mo(s * seg, seg), seg)
                vr = pl.ds(_mo(ci * c + s * seg, seg), seg)
                hs = get_h(s, h)
                y_sc[sr, :] += _dot(qs_sc[sr, :], hs)
                set_h(s, h, cd * hs + _dot_tn(ks_sc[sr, :], v_ref[vr, cols]))

            _for_segments(nseg, seg_step)
            y = y_sc[...]
            yn = y * lax.rsqrt(jnp.mean(y * y, axis=-1, keepdims=True) + 1e-5) * gn_ref[:, cols]
            o_ref[rws, cols] = (yn * _silu(g_ref[rws, cols])).astype(o_ref.dtype)

    if carry:
        @pl.when(pl.program_id(1) == pl.num_programs(1) - 1)
        def _():
            hout_ref[0] = hst[...]


def _mamba_kernel(z_ref, xbc_ref, dt_ref, cprev_ref, h0_ref, cw_ref, cb_ref, dtb_ref, alog_ref, dd_ref,
                  ng_ref, e64_ref, e128_ref, o_ref, cout_ref, hout_ref,
                  hst, xpad, act_sc, qs_sc, ks_sc, xdt_sc, xs_sc, yi_sc, yf_sc, etot_sc, *, rows, seg, carry):
    c = CHUNK
    nseg = c // seg
    get_h, set_h = _state_io(carry, h0_ref, hout_ref, hst)

    def conv_act(window, n):
        acc = cb_ref[...] + window[5:5 + n] * cw_ref[0:1, :]
        for i in range(1, MB_CONV):
            acc = acc + window[5 + i:5 + i + n] * cw_ref[i:i + 1, :]
        return _silu(acc)

    if carry:
        @pl.when(pl.program_id(1) == 0)
        def _():
            hst[...] = h0_ref[0]
            xpad[0:8, :] = jnp.zeros((8, MB_CONV_DIM), F32)
            xpad[5:8, :] = cprev_ref[0]

        xpad[8:8 + rows, :] = xbc_ref[...]
        for ci in range(rows // c):
            acc = cb_ref[...] + xpad[ci * c + 5:ci * c + 5 + c, :] * cw_ref[0:1, :]
            for i in range(1, MB_CONV):
                acc = acc + xpad[ci * c + 5 + i:ci * c + 5 + i + c, :] * cw_ref[i:i + 1, :]
            act_sc[ci * c:(ci + 1) * c, :] = _silu(acc)
        xpad[0:8, :] = xpad[rows:rows + 8, :]

        @pl.when(pl.program_id(1) == pl.num_programs(1) - 1)
        def _():
            cout_ref[0] = xpad[5:8, :]
    else:
        def conv_seq(s, carry_):
            sr = pl.ds(pl.multiple_of(s * seg, seg), seg)
            xpad[5:8, :] = cprev_ref[s]
            xs = xbc_ref[sr, :]
            window = jnp.concatenate([xpad[0:8, :], xs], axis=0)
            act_sc[sr, :] = conv_act(window, seg)
            cout_ref[s] = xs[seg - 3:seg]
            return carry_

        xpad[0:8, :] = jnp.zeros((8, MB_CONV_DIM), F32)
        lax.fori_loop(0, rows // seg, conv_seq, 0)

    incl, _, same, _, _ = _seg_masks(c, seg)
    lt_bf = jnp.where(incl, 1.0, 0.0).astype(BF16)
    same_bf = jnp.where(same, 1.0, 0.0).astype(BF16)
    a_neg = -jnp.exp(alog_ref[...])

    for ci in range(rows // c):
        rws = slice(ci * c, (ci + 1) * c)
        dtv = _softplus(dt_ref[rws, :] + dtb_ref[...])
        la = dtv * a_neg
        cum = _split_dot(lt_bf, la)
        tot = _split_dot(same_bf, la)
        cum_t = cum.T
        etot_sc[...] = jnp.exp(tot)
        cum_cols = _spread(cum, e128_ref[...])
        cum_full = _spread(cum, e64_ref[...])
        xh_all = act_sc[rws, 0:GROUP]
        xdt_all = xh_all * _spread(dtv, e64_ref[...])
        xdt_sc[...] = xdt_all
        xs_sc[...] = xdt_all * jnp.exp(_spread(tot, e64_ref[...]) - cum_full)
        for g in range(MB_G):
            cg = act_sc[rws, 768 + g * MB_N:768 + (g + 1) * MB_N]
            bg = act_sc[rws, 512 + g * MB_N:512 + (g + 1) * MB_N]
            gmat = _dot_nt(cg, bg)
            qs_sc[...] = cg
            ks_sc[...] = bg
            for hh in range(MB_H // MB_G):
                h = g * (MB_H // MB_G) + hh
                cols = slice(h * MB_HD, (h + 1) * MB_HD)
                lmat = jnp.exp(jnp.where(incl, cum_cols[:, h * c:(h + 1) * c] - cum_t[h:h + 1, :], -jnp.inf))
                yf_sc[:, cols] = _dot(gmat * lmat, xdt_sc[:, cols])

                def seg_step(s, h=h, cols=cols):
                    sr = pl.ds(_mo(s * seg, seg), seg)
                    first = pl.ds(_mo(s * seg, seg), 1)
                    hs = get_h(s, h)
                    yi_sc[sr, cols] = _dot_nt(qs_sc[sr, :], hs)
                    set_h(s, h, etot_sc[first, h:h + 1] * hs + _dot_tn(xs_sc[sr, cols], ks_sc[sr, :]))

                _for_segments(nseg, seg_step)
        y_all = yf_sc[...] + yi_sc[...] * jnp.exp(cum_full) + dd_ref[...] * xh_all
        yz = y_all * _silu(z_ref[rws, :])
        gw = GROUP // MB_G
        for g in range(MB_G):
            cols = slice(g * gw, (g + 1) * gw)
            part = yz[:, cols]
            nrm = part * lax.rsqrt(jnp.mean(part * part, axis=-1, keepdims=True) + 1e-5) * ng_ref[:, cols]
            o_ref[rws, cols] = nrm.astype(o_ref.dtype)

    if carry:
        @pl.when(pl.program_id(1) == pl.num_programs(1) - 1)
        def _():
            hout_ref[0] = hst[...]


def _gla_kernel(q_ref, k_ref, v_ref, gk_ref, g_ref, h0_ref, gkup_ref, gkb_ref, ng_ref, e_ref,
                o_ref, hout_ref,
                hst, qe_sc, ke_sc, etot_sc, p_sc, y_sc, *, rows, seg, carry):
    c = CHUNK
    nsub = c // seg
    if carry:
        @pl.when(pl.program_id(1) == 0)
        def _():
            for h in range(GLA_H):
                hst[h] = h0_ref[0, h].T
    incl, _, same, _, _ = _seg_masks(c, seg)
    lt_bf = jnp.where(incl, 1.0, 0.0).astype(BF16)
    same_bf = jnp.where(same, 1.0, 0.0).astype(BF16)
    row_i = lax.broadcasted_iota(jnp.int32, (seg, GLA_QK), 0)
    kcs = [slice(h * GLA_DK, (h + 1) * GLA_DK) for h in range(GLA_H)]
    vcs = [slice(h * GLA_DV, (h + 1) * GLA_DV) for h in range(GLA_H)]

    for ci in range(rows // c):
        rws = slice(ci * c, (ci + 1) * c)
        pre = _dot(gk_ref[rws, :], gkup_ref[...]) + gkb_ref[...]
        la = -_softplus(-pre) * (1.0 / GLA_TAU)
        cum = _split_dot(lt_bf, la)
        tot = _split_dot(same_bf, la)
        qv = q_ref[rws, :] * (GLA_DK ** -0.5)
        kv = k_ref[rws, :]
        qe_sc[...] = qv * jnp.exp(cum)
        ke_sc[...] = kv * jnp.exp(tot - cum)
        etot_sc[...] = jnp.exp(tot)

        safe = (jnp.max(-cum) < GLA_SAFE_EXP) if carry else False

        def factored():
            kinv = kv * jnp.exp(-cum)
            for h in range(GLA_H):
                sc = jnp.where(incl, _dot_nt(qe_sc[:, kcs[h]], kinv[:, kcs[h]]), 0.0)
                y_sc[:, vcs[h]] = _dot(sc, v_ref[rws, vcs[h]])

        def pairwise():
            for u in range(nsub):
                sr = slice(u * seg, (u + 1) * seg)
                cu, qu = cum[sr], qv[sr]
                for j in range(seg):
                    r0 = u * seg + j
                    pj = qu * jnp.exp(jnp.where(row_i >= j, cu - cum[r0:r0 + 1], -jnp.inf)) * kv[r0:r0 + 1]
                    p_sc[u, j * seg:(j + 1) * seg, :] = pj
            for u in range(nsub):
                rm = _dot(p_sc[u], e_ref[...])
                y = rm[0:seg, :] * v_ref[ci * c + u * seg:ci * c + u * seg + 1, :]
                for j in range(1, seg):
                    r0 = ci * c + u * seg + j
                    y = y + rm[j * seg:(j + 1) * seg, :] * v_ref[r0:r0 + 1, :]
                y_sc[u * seg:(u + 1) * seg, :] = y

        if carry:
            pl.when(safe)(factored)
            pl.when(jnp.logical_not(safe))(pairwise)
        else:
            pairwise()
        states = [hst[h] for h in range(GLA_H)] if carry else None
        for u in range(nsub):
            sr = slice(u * seg, (u + 1) * seg)
            vr = slice(ci * c + u * seg, ci * c + (u + 1) * seg)
            hts = [states[h] if carry else h0_ref[u, h].T for h in range(GLA_H)]
            for h in range(GLA_H):
                y_sc[sr, vcs[h]] += _dot_nt(qe_sc[sr, kcs[h]], hts[h])
            upds = [_dot_tn(v_ref[vr, vcs[h]], ke_sc[sr, kcs[h]]) for h in range(GLA_H)]
            for h in range(GLA_H):
                new = hts[h] * etot_sc[u * seg:u * seg + 1, kcs[h]] + upds[h]
                if carry:
                    states[h] = new
                else:
                    hout_ref[u, h] = new.T
        if carry:
            for h in range(GLA_H):
                hst[h] = states[h]

        for h in range(GLA_H):
            y = y_sc[:, vcs[h]]
            yn = y * lax.rsqrt(jnp.mean(y * y, axis=-1, keepdims=True) + 1e-5) * ng_ref[:, vcs[h]]
            o_ref[rws, vcs[h]] = (yn * _silu(g_ref[rws, vcs[h]])).astype(o_ref.dtype)

    if carry:
        @pl.when(pl.program_id(1) == pl.num_programs(1) - 1)
        def _():
            for h in range(GLA_H):
                hout_ref[0, h] = hst[h].T


def _rwkv_kernel(p_ref, sp_ref, h0_ref, mu_ref, w0_ref, wup_ref, a0_ref, aup_ref, gup_ref, kk_ref, ka_ref,
                 rk_ref, lng_ref, lnb_ref, ones_ref, o_ref, hout_ref,
                 hst, last_sc, prev_sc, w_sc, y_sc, bt_sc, kt_sc, v_sc, rt_sc, u_sc, rkb_sc, etot_sc, out_sc, kap_sc,
                 pw_sc, t_sc, ak_sc, pb_sc, pk_sc, *, rows, seg, sub, carry):
    c = CHUNK
    assert rows == c
    nsub = c // sub
    n_iter = int(math.log2(sub)) - 1
    npair = RW_H // 2
    pw_ = 2 * RW_HD
    pr_i = lax.broadcasted_iota(jnp.int32, (pw_, pw_), 0)
    pc_i = lax.broadcasted_iota(jnp.int32, (pw_, pw_), 1)
    bd_mask = (pr_i >= RW_HD) == (pc_i >= RW_HD)
    low_rows = lax.broadcasted_iota(jnp.int32, (pw_, RW_HD), 0) < RW_HD

    def to_bd(stack):
        return jnp.where(bd_mask, jnp.concatenate([stack, stack], axis=1), 0.0)

    def from_bd(bd):
        return jnp.where(low_rows, bd[:, 0:RW_HD], bd[:, RW_HD:pw_])

    if carry:
        @pl.when(pl.program_id(1) == 0)
        def _():
            for pi in range(npair):
                hst[pi] = to_bd(h0_ref[0, pi])
            last_sc[...] = jnp.broadcast_to(sp_ref[0], (8, RW_IN))

        prev_row = jnp.broadcast_to(last_sc[0:1, :], (c, RW_IN))
    else:
        for s in range(rows // seg):
            prev_sc[s * seg:(s + 1) * seg, :] = jnp.broadcast_to(sp_ref[s], (seg, RW_IN))
        prev_row = prev_sc[...]
    incl, strict, same, r_i, c_i = _seg_masks(c, sub)
    lt_bf = jnp.where(incl, 1.0, 0.0).astype(BF16)
    same_bf = jnp.where(same, 1.0, 0.0).astype(BF16)
    eye = jnp.where(r_i == c_i, 1.0, 0.0)
    row_w = lax.broadcasted_iota(jnp.int32, (c, RW_IN), 0)
    first_row = (row_w & ((c if carry else seg) - 1)) == 0

    p = p_ref[...]
    prev = jnp.where(first_row, prev_row, pltpu.roll(p, 1, 0))
    xs = p + (prev - p) * mu_ref[...]
    r = xs[:, 0:GROUP]
    k = xs[:, GROUP:2 * GROUP]
    v = xs[:, 2 * GROUP:3 * GROUP]
    wa = xs[:, 3 * GROUP:3 * GROUP + 128]
    gd = xs[:, 3 * GROUP + 128:RW_IN]
    w = -_softplus(-(w0_ref[...] + _dot(jnp.tanh(wa), wup_ref[...]))) - 0.5
    ld = -jnp.exp(w)
    a = _sigmoid(a0_ref[...] + _dot(wa, aup_ref[...]))
    gate = _dot(_sigmoid(gd), gup_ref[...])
    kkr = k * kk_ref[...]
    k2 = k * (1.0 + (a - 1.0) * ka_ref[...])
    cum = _split_dot(lt_bf, ld)
    tot = _split_dot(same_bf, ld)
    ecum = jnp.exp(cum)
    einv = jnp.exp(-cum)
    eprev = jnp.exp(cum - ld)
    rkb_sc[...] = r * k2 * rk_ref[...]
    etot_sc[...] = jnp.exp(tot)
    v_sc[...] = v
    rtil = r * ecum
    rt_sc[...] = rtil
    kt_sc[...] = k2 * einv
    rn = lax.rsqrt(jnp.maximum(_head_sums(kkr * kkr, ones_ref[...]), 1e-24))
    braw = kkr * a * einv * rn
    kraw = kkr * eprev * rn

    heads = [slice(h * RW_HD, (h + 1) * RW_HD) for h in range(RW_H)]
    for h, cols in enumerate(heads):
        kap = kraw[:, cols]
        bt = braw[:, cols]
        kap_sc[:, cols] = kap
        bt_sc[:, cols] = bt
        m1 = _dot_nt(jnp.concatenate([kap, rtil[:, cols]], axis=0), jnp.concatenate([bt, kt_sc[:, cols]], axis=0))
        x = jnp.where(strict, -m1[0:c, 0:c], 0.0)
        pw_sc[0, h] = x
        t_sc[h] = eye + x
        ak_sc[h] = jnp.where(strict, m1[0:c, c:2 * c], 0.0)
        pb_sc[h] = jnp.where(incl, m1[c:2 * c, 0:c], 0.0)
        pk_sc[h] = jnp.where(incl, m1[c:2 * c, c:2 * c], 0.0)
    for h, cols in enumerate(heads):
        y_sc[:, cols] = _dot(ak_sc[h], v_sc[:, cols])
    for it in range(n_iter):
        src, dst = it % 2, (it + 1) % 2
        for h in range(RW_H):
            pw = pw_sc[src, h]
            pw_sc[dst, h] = _dot(pw, pw)
        for h in range(RW_H):
            tm = t_sc[h]
            t_sc[h] = tm + _dot(tm, pw_sc[dst, h])
    for h, cols in enumerate(heads):
        wy = _dot(t_sc[h], jnp.concatenate([kap_sc[:, cols], y_sc[:, cols]], axis=1))
        w_sc[:, cols] = wy[:, 0:RW_HD]
        y_sc[:, cols] = wy[:, RW_HD:2 * RW_HD]

    states = [hst[pi] for pi in range(npair)] if carry else None
    units = [(s, pi) for s in range(nsub) for pi in range(npair)]
    group = npair if carry else 2 * npair
    for g0 in range(0, len(units), group):
        grp_units = units[g0:g0 + group]
        sts, wrs, uus, upds = [], [], [], []
        for s, pi in grp_units:
            sts.append(states[pi] if carry else to_bd(h0_ref[s, pi]))
        for (s, pi), st in zip(grp_units, sts):
            sr, pc = slice(s * sub, (s + 1) * sub), slice(pi * pw_, (pi + 1) * pw_)
            wrs.append(_dot_nt(jnp.concatenate([w_sc[sr, pc], rt_sc[sr, pc]], axis=0), st))
        for (s, pi), wr in zip(grp_units, wrs):
            sr, pc = slice(s * sub, (s + 1) * sub), slice(pi * pw_, (pi + 1) * pw_)
            uu = -wr[0:sub, :] - y_sc[sr, pc]
            u_sc[sr, pc] = uu
            out_sc[sr, pc] = wr[sub:2 * sub, :]
            uus.append(uu)
        for (s, pi), uu in zip(grp_units, uus):
            sr, pc = slice(s * sub, (s + 1) * sub), slice(pi * pw_, (pi + 1) * pw_)
            upds.append(_dot_tn(jnp.concatenate([uu, v_sc[sr, pc]], axis=0),
                                jnp.concatenate([bt_sc[sr, pc], kt_sc[sr, pc]], axis=0)))
        for (s, pi), st, upd in zip(grp_units, sts, upds):
            pc = slice(pi * pw_, (pi + 1) * pw_)
            new = (st + jnp.where(bd_mask, upd, 0.0)) * etot_sc[s * sub:s * sub + 1, pc]
            if carry:
                states[pi] = new
            else:
                hout_ref[s, pi] = from_bd(new)
    if carry:
        for pi in range(npair):
            hst[pi] = states[pi]

    for h, cols in enumerate(heads):
        out_sc[:, cols] += _dot(jnp.concatenate([pb_sc[h], pk_sc[h]], axis=1),
                                jnp.concatenate([u_sc[:, cols], v_sc[:, cols]], axis=0))
    ones = ones_ref[...]
    o_all = out_sc[...]
    oc = o_all - _head_sums(o_all, ones) * (1.0 / RW_HD)
    var = _head_sums(oc * oc, ones) * (1.0 / RW_HD)
    on = oc * lax.rsqrt(var + RW_LN_EPS) * lng_ref[...] + lnb_ref[...]
    out_sc[...] = on + _head_sums(rkb_sc[...], ones) * v_sc[...]
    o_ref[...] = (out_sc[...] * gate).astype(o_ref.dtype)

    if carry:
        last_sc[...] = jnp.broadcast_to(p_ref[rows - 1:rows, :], (8, RW_IN))

        @pl.when(pl.program_id(1) == pl.num_programs(1) - 1)
        def _():
            for pi in range(npair):
                hout_ref[0, pi] = from_bd(hst[pi])


class _Group:
    def __init__(self, bsz, t, rows):
        self.bsz, self.t, self.rows = bsz, t, rows
        self.carry = t >= CHUNK
        self.nb = 1 if self.carry else rows // t
        self.nt = t // rows if self.carry else 1
        self.grid = (bsz // self.nb, self.nt)
        self.seg = CHUNK if self.carry else t

    def rows_spec(self, width, col_block):
        nt = self.nt
        return pl.BlockSpec((self.rows, width), lambda i, j: (i * nt + j, col_block))

    def state_spec(self, shape, layer=None):
        zeros = (0,) * len(shape)
        if layer is None:
            return pl.BlockSpec((self.nb,) + tuple(shape), lambda i, j: (i,) + zeros)
        return pl.BlockSpec((None, self.nb) + tuple(shape), lambda i, j: (layer, i) + zeros)

    def out_rows(self, width, dtype=BF16):
        nt = self.nt
        return (pl.BlockSpec((self.rows, width), lambda i, j: (i * nt + j, 0)),
                jax.ShapeDtypeStruct((self.bsz * self.t, width), dtype))


def _vec_spec(shape):
    zeros = (0,) * len(shape)
    return pl.BlockSpec(tuple(shape), lambda i, j: zeros)


def _retention(grp, p, cos_t, sin_t, h0, sl, gn):
    c = CHUNK
    shape = (RET_H, RET_HD, RET_HD)
    in_specs = [grp.rows_spec(GROUP, COL_RET // GROUP + n) for n in range(4)]
    tab = pl.BlockSpec((grp.rows, RET_HD), (lambda i, j: (j, 0)) if grp.carry else (lambda i, j: (0, 0)))
    in_specs += [tab, tab, _vec_spec((1, GROUP)), grp.state_spec(shape, sl)]
    o_spec, o_shape = grp.out_rows(GROUP)
    return pl.pallas_call(
        functools.partial(_ret_kernel, rows=grp.rows, seg=grp.seg, carry=grp.carry),
        grid=grp.grid,
        in_specs=in_specs,
        out_specs=[o_spec, grp.state_spec(shape)],
        out_shape=[o_shape, jax.ShapeDtypeStruct(h0.shape[1:], F32)],
        scratch_shapes=[pltpu.VMEM(shape, F32), pltpu.VMEM((c, RET_HD), F32),
                        pltpu.VMEM((c, RET_HD), F32), pltpu.VMEM((c, RET_HD), F32)],
        compiler_params=_cparams(2),
    )(p, p, p, p, cos_t, sin_t, gn.reshape(1, GROUP), h0)


def _mamba(grp, p, cprev, h0, sl, lp):
    c = CHUNK
    cshape, hshape = (MB_CONV - 1, MB_CONV_DIM), (MB_H, MB_HD, MB_N)
    h0 = jnp.swapaxes(h0, -1, -2)
    pad8 = lambda a: jnp.pad(a.reshape(1, MB_H), ((0, 0), (0, 128 - MB_H)))
    head = jnp.arange(128)[:, None]
    e64 = (head == jnp.arange(GROUP)[None, :] // MB_HD).astype(BF16)
    e128 = (head == jnp.arange(MB_H * CHUNK)[None, :] // CHUNK).astype(BF16)
    in_specs = [grp.rows_spec(GROUP, COL_MB_Z // GROUP), grp.rows_spec(MB_CONV_DIM, COL_MB_XBC // MB_CONV_DIM),
                grp.rows_spec(128, COL_MB_DT // 128),
                grp.state_spec(cshape, sl), grp.state_spec(hshape, sl),
                _vec_spec((MB_CONV, MB_CONV_DIM)), _vec_spec((1, MB_CONV_DIM)), _vec_spec((1, 128)),
                _vec_spec((1, 128)), _vec_spec((1, GROUP)), _vec_spec((1, GROUP)),
                _vec_spec((128, GROUP)), _vec_spec((128, MB_H * CHUNK))]
    o_spec, o_shape = grp.out_rows(GROUP)
    return pl.pallas_call(
        functools.partial(_mamba_kernel, rows=grp.rows, seg=grp.seg, carry=grp.carry),
        grid=grp.grid,
        in_specs=in_specs,
        out_specs=[o_spec, grp.state_spec(cshape), grp.state_spec(hshape)],
        out_shape=[o_shape, jax.ShapeDtypeStruct(cprev.shape[1:], F32), jax.ShapeDtypeStruct(h0.shape[1:], F32)],
        scratch_shapes=[pltpu.VMEM(hshape, F32),
                        pltpu.VMEM(((grp.rows if grp.carry else 0) + 8, MB_CONV_DIM), F32),
                        pltpu.VMEM((grp.rows, MB_CONV_DIM), F32),
                        pltpu.VMEM((c, MB_N), F32), pltpu.VMEM((c, MB_N), F32), pltpu.VMEM((c, GROUP), F32),
                        pltpu.VMEM((c, GROUP), F32), pltpu.VMEM((c, GROUP), F32), pltpu.VMEM((c, GROUP), F32),
                        pltpu.VMEM((c, 128), F32)],
        compiler_params=_cparams(2),
    )(p, p, p, cprev, h0, lp['mb_conv_w'], lp['mb_conv_b'].reshape(1, MB_CONV_DIM), pad8(lp['mb_dt_bias']),
      pad8(lp['mb_a_log']), jnp.repeat(lp['mb_d'], MB_HD).reshape(1, GROUP), lp['mb_norm_g'].reshape(1, GROUP),
      e64, e128)


def _gla(grp, p, h0, sl, lp, expand):
    c = CHUNK
    shape = (GLA_H, GLA_DK, GLA_DV)
    sub = 64 if grp.carry else grp.seg
    gk_up = jnp.pad(lp['gla_gk_up'], ((0, 128 - GLA_LORA), (0, 0)))
    in_specs = [grp.rows_spec(GLA_QK, COL_GLA_Q // GLA_QK), grp.rows_spec(GLA_QK, COL_GLA_K // GLA_QK),
                grp.rows_spec(GROUP, COL_GLA_V // GROUP), grp.rows_spec(128, COL_GLA_GK // 128),
                grp.rows_spec(GROUP, COL_GLA_G // GROUP), grp.state_spec(shape, sl),
                _vec_spec((128, GLA_QK)), _vec_spec((1, GLA_QK)), _vec_spec((1, GROUP)), _vec_spec((GLA_QK, GROUP))]
    o_spec, o_shape = grp.out_rows(GROUP)
    return pl.pallas_call(
        functools.partial(_gla_kernel, rows=grp.rows, seg=sub, carry=grp.carry),
        grid=grp.grid,
        in_specs=in_specs,
        out_specs=[o_spec, grp.state_spec(shape)],
        out_shape=[o_shape, jax.ShapeDtypeStruct(h0.shape[1:], F32)],
        scratch_shapes=[pltpu.VMEM((GLA_H, GLA_DV, GLA_DK), F32)]
        + [pltpu.VMEM((c, GLA_QK), F32) for _ in range(3)]
        + [pltpu.VMEM((c // sub, sub * sub, GLA_QK), F32), pltpu.VMEM((c, GROUP), F32)],
        compiler_params=_cparams(2),
    )(p, p, p, p, p, h0, gk_up, lp['gla_gk_b'].reshape(1, GLA_QK), lp['gla_norm_g'].reshape(1, GROUP), expand)


def _rwkv(grp, p, shift_prev, h0, sl, lp):
    c = CHUNK
    head_ones = (jnp.arange(GROUP)[:, None] // RW_HD == jnp.arange(GROUP)[None, :] // RW_HD).astype(BF16)
    shape = (RW_H // 2, 2 * RW_HD, RW_HD)
    h0 = h0.reshape(h0.shape[:2] + shape)
    sub = 64 if grp.carry else grp.seg
    row = lambda a: a.reshape(1, -1)
    w_up = jnp.pad(lp['rw_w_up'], ((0, 64), (0, 0)))
    a_up = jnp.pad(lp['rw_a_up'], ((64, 0), (0, 0)))
    in_specs = [grp.rows_spec(RW_IN, 0), grp.state_spec((1, RW_IN), sl), grp.state_spec(shape, sl),
                _vec_spec((1, RW_IN)), _vec_spec((1, GROUP)), _vec_spec((128, GROUP)), _vec_spec((1, GROUP)),
                _vec_spec((128, GROUP)), _vec_spec((128, GROUP))] + [_vec_spec((1, GROUP))] * 5 + [_vec_spec((GROUP, GROUP))]
    o_spec, o_shape = grp.out_rows(GROUP)
    wide = lambda: pltpu.VMEM((c, GROUP), F32)
    return pl.pallas_call(
        functools.partial(_rwkv_kernel, rows=grp.rows, seg=grp.seg, sub=sub, carry=grp.carry),
        grid=grp.grid,
        in_specs=in_specs,
        out_specs=[o_spec, grp.state_spec(shape)],
        out_shape=[o_shape, jax.ShapeDtypeStruct(h0.shape[1:], F32)],
        scratch_shapes=[pltpu.VMEM((RW_H // 2, 2 * RW_HD, 2 * RW_HD), F32), pltpu.VMEM((8, RW_IN), F32),
                        pltpu.VMEM((c, RW_IN), F32)] + [wide() for _ in range(11)]
        + [pltpu.VMEM((2, RW_H, c, c), F32)] + [pltpu.VMEM((RW_H, c, c), F32) for _ in range(4)],
        compiler_params=_cparams(2),
    )(p, shift_prev.reshape(shift_prev.shape[0], -1, 1, RW_IN), h0, row(lp['rw_mu']), row(lp['rw_w0']), w_up,
      row(lp['rw_a0']), a_up, lp['rw_g_up'], row(lp['rw_k_k']), row(lp['rw_k_a']), row(lp['rw_r_k']),
      row(lp['rw_ln_g']), row(lp['rw_ln_b']), head_ones)


def _rope_tables(pos0, t):
    half = RET_HD // 2
    inv = 1.0 / (ROPE_BASE ** jnp.linspace(0.0, 1.0, half, dtype=F32))
    pos = pos0 + jnp.arange(t, dtype=F32)
    ang = pos[:, None] * inv[None, :]
    cos, sin = jnp.cos(ang), jnp.sin(ang)
    cos_t = jnp.stack([cos, cos], axis=-1).reshape(t, RET_HD)
    sin_t = jnp.stack([-sin, sin], axis=-1).reshape(t, RET_HD)
    return cos_t, sin_t


def _pad_w_in(w):
    z = lambda n: jnp.zeros(w.shape[:-1] + (n,), w.dtype)
    return jnp.concatenate([
        w[..., 0:1792], w[..., 3328:3336], z(120), w[..., 4360:4376], z(112), w[..., 2304:3328],
        w[..., 1792:2304], w[..., 3848:4360], w[..., 4376:4888], w[..., 3336:3592], w[..., 3592:3848],
        w[..., 4888:6936]], axis=-1)


def _layer(grp, grp_rw, x, x_bf, pos_tabs, states, sl, mem_k, mem_v, cache_layer, lp, big, wl, expand, tm):
    rw_shift, rw_state, mb_conv, mb_state, gla_state, ret_state = states
    bsz, t = grp.bsz, grp.t
    tm_ln = 512 if grp.carry else 256
    p = _matmul([x_bf], big['w_in'], F32, tm, 1024, wl)
    o_rw, rw_new = _rwkv(grp_rw, p, rw_shift, rw_state, sl, lp)
    rw_new = rw_new.reshape(bsz, RW_H, RW_HD, RW_HD)
    o_mb, conv_new, mb_new = _mamba(grp, p, mb_conv, mb_state, sl, lp)
    mb_new = jnp.swapaxes(mb_new, -1, -2)
    o_gl, gla_new = _gla(grp, p, gla_state, sl, lp, expand)
    o_rt, ret_new = _retention(grp, p, pos_tabs[0], pos_tabs[1], ret_state, sl, lp['ret_norm_g'])
    shift_new = p.reshape(bsz, t, N_PAD)[:, t - 1, 0:RW_IN]
    x, x_bf = _matmul_res_ln([o_rw, o_mb, o_gl, o_rt], big['w_out'], wl, x, lp['ln1_g'], lp['ln1_b'], tm_ln)
    q = _matmul([x_bf], big['xa_wq'], BF16 if grp.carry else F32, tm, 1024, wl)
    if grp.carry:
        att = _cross_attn(q, mem_k, mem_v, bsz, t, 1, 512, BF16)
    else:
        att = _cross_attn(q, mem_k, mem_v, bsz, t, 4, t, F32, layer=cache_layer)
    x, x_bf = _matmul_res_ln([att], big['xa_wo'], wl, x, lp['ln2_g'], lp['ln2_b'], tm_ln)
    hid = _ffn_gate_up(x_bf, big['ffn_w_gate'], big['ffn_w_up'], tm, 512, wl)
    x, x_bf = _matmul_res_ln([hid], big['ffn_w_down'], wl, x, lp['ln3_g'], lp['ln3_b'], 256)
    return x, x_bf, (shift_new, rw_new, conv_new, mb_new, gla_new, ret_new)


def kernel(x_prompt, x_sample, state_rwkv_shift, state_rwkv_wkv, state_mamba_conv, state_mamba_ssm, state_gla,
           state_ret, cache_mem_k, cache_mem_v, mem_prompt, w_in, w_out, ln1_g, ln1_b, rw_mu, rw_w0, rw_w_up,
           rw_a0, rw_a_up, rw_g_up, rw_k_k, rw_k_a, rw_r_k, rw_ln_g, rw_ln_b, mb_conv_w, mb_conv_b, mb_dt_bias,
           mb_a_log, mb_d, mb_norm_g, gla_gk_up, gla_gk_b, gla_norm_g, ret_norm_g, ln2_g, ln2_b, xa_wq, xa_wk,
           xa_wv, xa_wo, ln3_g, ln3_b, ffn_w_gate, ffn_w_up, ffn_w_down):
    small = dict(
        ln1_g=ln1_g, ln1_b=ln1_b, rw_mu=rw_mu, rw_w0=rw_w0, rw_w_up=rw_w_up, rw_a0=rw_a0,
        rw_a_up=rw_a_up, rw_g_up=rw_g_up, rw_k_k=rw_k_k, rw_k_a=rw_k_a, rw_r_k=rw_r_k, rw_ln_g=rw_ln_g,
        rw_ln_b=rw_ln_b, mb_conv_w=mb_conv_w, mb_conv_b=mb_conv_b, mb_dt_bias=mb_dt_bias, mb_a_log=mb_a_log,
        mb_d=mb_d, mb_norm_g=mb_norm_g, gla_gk_up=gla_gk_up, gla_gk_b=gla_gk_b, gla_norm_g=gla_norm_g,
        ret_norm_g=ret_norm_g, ln2_g=ln2_g, ln2_b=ln2_b, ln3_g=ln3_g, ln3_b=ln3_b)
    big = dict(w_in=_pad_w_in(w_in), w_out=w_out, xa_wq=xa_wq, xa_wo=xa_wo, ffn_w_gate=ffn_w_gate,
               ffn_w_up=ffn_w_up, ffn_w_down=ffn_w_down.astype(BF16))
    bp, tp, _ = x_prompt.shape
    bs, ts, _ = x_sample.shape
    gp = _Group(bp, tp, 512)
    gp_rw = _Group(bp, tp, CHUNK)
    gs = _Group(bs, ts, CHUNK)
    expand = (jnp.arange(GLA_QK)[:, None] // GLA_DK == jnp.arange(GROUP)[None, :] // GLA_DV).astype(BF16)
    tabs_p = _rope_tables(0.0, tp)
    tabs_s = tuple(jnp.tile(tb, (gs.nb, 1)) for tb in _rope_tables(float(PAST_LEN), ts))
    zeros_p = (jnp.zeros((1, bp, RW_IN), F32), jnp.zeros((1, bp, RW_H, RW_HD, RW_HD), F32),
               jnp.zeros((1, bp, MB_CONV - 1, MB_CONV_DIM), F32), jnp.zeros((1, bp, MB_H, MB_N, MB_HD), F32),
               jnp.zeros((1, bp, GLA_H, GLA_DK, GLA_DV), F32), jnp.zeros((1, bp, RET_H, RET_HD, RET_HD), F32))
    st_s_in = (state_rwkv_shift, state_rwkv_wkv, state_mamba_conv, state_mamba_ssm, state_gla, state_ret)

    yp = x_prompt.reshape(bp * tp, D_MODEL)
    ys = x_sample.reshape(bs * ts, D_MODEL)
    yp_bf, ys_bf = yp.astype(BF16), ys.astype(BF16)
    mem_bf = mem_prompt.reshape(bp * N_MEM, D_MODEL).astype(BF16)
    outs_p = [[] for _ in range(8)]
    outs_s = [[] for _ in range(6)]
    for i in range(DEPTH):
        lp = {name: val[i] for name, val in small.items()}
        mk = _matmul([mem_bf], xa_wk, F32, 1024, 512, i)
        mv = _matmul([mem_bf], xa_wv, F32, 1024, 512, i)
        yp, yp_bf, st_p = _layer(gp, gp_rw, yp, yp_bf, tabs_p, zeros_p, 0, mk.reshape(bp, N_MEM, D_MODEL),
                                 mv.reshape(bp, N_MEM, D_MODEL), None, lp, big, i, expand, 1024)
        for lst, val in zip(outs_p, st_p + (mk.reshape(bp, N_MEM, XA_H, XA_HD), mv.reshape(bp, N_MEM, XA_H, XA_HD))):
            lst.append(val)
        ys, ys_bf, st_s = _layer(gs, gs, ys, ys_bf, tabs_s, st_s_in, i, cache_mem_k, cache_mem_v, i, lp, big, i,
                                 expand, 1024)
        for lst, val in zip(outs_s, st_s):
            lst.append(val)
    return (yp.reshape(bp, tp, D_MODEL), ys.reshape(bs, ts, D_MODEL),
            *[jnp.stack(v) for v in outs_p], *[jnp.stack(v) for v in outs_s])
```

```python
import functools
import math

import jax
import jax.numpy as jnp
from jax import lax
from jax.experimental import pallas as pl
from jax.experimental.pallas import tpu as pltpu

F32 = jnp.float32
BF16 = jnp.bfloat16

D_MODEL = 2048
DEPTH = 2
PAST_LEN = 16384
GROUP = 512
RW_H, RW_HD = 8, 64
RW_IN = 1792
RW_LN_EPS = 64e-5
MB_H, MB_HD, MB_N, MB_G = 8, 64, 128, 2
MB_CONV = 4
MB_CONV_DIM = 1024
GLA_H, GLA_DK, GLA_DV = 4, 64, 128
GLA_QK = 256
GLA_LORA = 16
GLA_TAU = 16.0
GLA_SAFE_EXP = 60.0
RET_H, RET_HD = 4, 128
ROPE_BASE = 10000.0
N_MEM = 256
XA_H, XA_HD = 4, 512
D_FF = 5632
DN_ALPHA = (2 * DEPTH) ** 0.25
RET_LOG_GAMMA = tuple(math.log1p(-(2.0 ** (-5.0 - h))) for h in range(RET_H))

N_PAD = 7168
COL_MB_DT = 1792
COL_GLA_GK = 1920
COL_MB_XBC = 2048
COL_MB_Z = 3072
COL_GLA_V = 3584
COL_GLA_G = 4096
COL_GLA_Q = 4608
COL_GLA_K = 4864
COL_RET = 5120

CHUNK = 128
VMEM_LIMIT = 56 * 1024 * 1024


def _cparams(n_axes, vmem=VMEM_LIMIT):
    return pltpu.CompilerParams(dimension_semantics=("arbitrary",) * n_axes, vmem_limit_bytes=vmem)


def _dot(a, b):
    return jnp.dot(a.astype(BF16), b.astype(BF16), preferred_element_type=F32)


def _dot_nt(a, b):
    return lax.dot_general(a.astype(BF16), b.astype(BF16), (((1,), (1,)), ((), ())), preferred_element_type=F32)


def _dot_tn(a, b):
    return lax.dot_general(a.astype(BF16), b.astype(BF16), (((0,), (0,)), ((), ())), preferred_element_type=F32)


def _sigmoid(x):
    return 1.0 / (1.0 + jnp.exp(-x))


def _silu(x):
    return x * _sigmoid(x)


def _softplus(x):
    return jnp.maximum(x, 0.0) + jnp.log(1.0 + jnp.exp(-jnp.abs(x)))


def _split_dot(m_bf16, x):
    hi = x.astype(BF16)
    r1 = x - hi.astype(F32)
    mid = r1.astype(BF16)
    lo = (r1 - mid.astype(F32)).astype(BF16)
    return (jnp.dot(m_bf16, hi, preferred_element_type=F32)
            + jnp.dot(m_bf16, mid, preferred_element_type=F32)
            + jnp.dot(m_bf16, lo, preferred_element_type=F32))


def _spread(x, sel_bf16):
    hi = x.astype(BF16)
    r1 = x - hi.astype(F32)
    mid = r1.astype(BF16)
    lo = (r1 - mid.astype(F32)).astype(BF16)
    return (jnp.dot(hi, sel_bf16, preferred_element_type=F32)
            + jnp.dot(mid, sel_bf16, preferred_element_type=F32)
            + jnp.dot(lo, sel_bf16, preferred_element_type=F32))


def _head_sums(x, ones_bf16):
    hi = x.astype(BF16)
    lo = (x - hi.astype(F32)).astype(BF16)
    return (jnp.dot(hi, ones_bf16, preferred_element_type=F32)
            + jnp.dot(lo, ones_bf16, preferred_element_type=F32))


def _mo(x, m):
    return x if isinstance(x, int) else pl.multiple_of(x, m)


def _seg_masks(c, seg):
    sh = jnp.int32(int(math.log2(seg)))
    r = lax.broadcasted_iota(jnp.int32, (c, c), 0)
    q = lax.broadcasted_iota(jnp.int32, (c, c), 1)
    same = lax.shift_right_arithmetic(r, sh) == lax.shift_right_arithmetic(q, sh)
    incl = jnp.logical_and(same, r >= q)
    strict = jnp.logical_and(same, r > q)
    return incl, strict, same, r, q


def _mm_kernel(*refs, k_sizes):
    n_x = len(k_sizes)
    x_refs, w_ref, o_ref, wbf = refs[:n_x], refs[n_x], refs[n_x + 1], refs[n_x + 2]

    @pl.when(pl.program_id(1) == 0)
    def _():
        wbf[...] = w_ref[...].astype(BF16)

    acc = None
    off = 0
    for xr, ks in zip(x_refs, k_sizes):
        part = jnp.dot(xr[...].astype(BF16), wbf[off:off + ks, :], preferred_element_type=F32)
        acc = part if acc is None else acc + part
        off += ks
    o_ref[...] = acc.astype(o_ref.dtype)


def _w_spec(w, tn, layer):
    k = w.shape[-2]
    if w.ndim == 2:
        return pl.BlockSpec((k, tn), lambda j, i: (0, j))
    return pl.BlockSpec((None, k, tn), lambda j, i: (layer, 0, j))


def _matmul(xs, w, out_dtype, tm, tn, layer=None):
    m = xs[0].shape[0]
    k, n = w.shape[-2:]
    k_sizes = tuple(x.shape[1] for x in xs)
    assert sum(k_sizes) == k and m % tm == 0 and n % tn == 0
    in_specs = [pl.BlockSpec((tm, ks), lambda j, i: (i, 0)) for ks in k_sizes]
    in_specs.append(_w_spec(w, tn, layer))
    return pl.pallas_call(
        functools.partial(_mm_kernel, k_sizes=k_sizes),
        grid=(n // tn, m // tm),
        in_specs=in_specs,
        out_specs=pl.BlockSpec((tm, tn), lambda j, i: (i, j)),
        out_shape=jax.ShapeDtypeStruct((m, n), out_dtype),
        scratch_shapes=[pltpu.VMEM((k, tn), BF16)],
        compiler_params=_cparams(2),
    )(*xs, w)


def _ffn_gu_kernel(x_ref, wg_ref, wu_ref, o_ref, wg_bf, wu_bf):
    @pl.when(pl.program_id(1) == 0)
    def _():
        wg_bf[...] = wg_ref[...].astype(BF16)
        wu_bf[...] = wu_ref[...].astype(BF16)

    x = x_ref[...]
    gate = jnp.dot(x, wg_bf[...], preferred_element_type=F32)
    up = jnp.dot(x, wu_bf[...], preferred_element_type=F32)
    o_ref[...] = (_silu(gate) * up).astype(o_ref.dtype)


def _ffn_gate_up(x_bf, wg, wu, tm, tn, layer):
    m, k = x_bf.shape
    n = wg.shape[-1]
    return pl.pallas_call(
        _ffn_gu_kernel,
        grid=(n // tn, m // tm),
        in_specs=[pl.BlockSpec((tm, k), lambda j, i: (i, 0)), _w_spec(wg, tn, layer), _w_spec(wu, tn, layer)],
        out_specs=pl.BlockSpec((tm, tn), lambda j, i: (i, j)),
        out_shape=jax.ShapeDtypeStruct((m, n), BF16),
        scratch_shapes=[pltpu.VMEM((k, tn), BF16), pltpu.VMEM((k, tn), BF16)],
        compiler_params=_cparams(2),
    )(x_bf, wg, wu)


def _mm_ln_kernel(*refs, k_sizes, cast_w):
    n_x = len(k_sizes)
    x_refs = refs[:n_x]
    w_ref, r_ref, g_ref, b_ref, of_ref, ob_ref = refs[n_x:n_x + 6]
    if cast_w:
        wbf = refs[n_x + 6]

        @pl.when(pl.program_id(0) == 0)
        def _():
            wbf[...] = w_ref[...].astype(BF16)
    else:
        wbf = w_ref
    acc = None
    off = 0
    for xr, ks in zip(x_refs, k_sizes):
        part = jnp.dot(xr[...].astype(BF16), wbf[off:off + ks, :], preferred_element_type=F32)
        acc = part if acc is None else acc + part
        off += ks
    z = DN_ALPHA * r_ref[...] + acc
    zc = z - jnp.mean(z, axis=-1, keepdims=True)
    var = jnp.mean(zc * zc, axis=-1, keepdims=True)
    out = zc * lax.rsqrt(var + 1e-5) * g_ref[...] + b_ref[...]
    of_ref[...] = out
    ob_ref[...] = out.astype(BF16)


def _matmul_res_ln(xs, w, layer, resid, g, b, tm):
    m, n = resid.shape
    k = w.shape[-2]
    k_sizes = tuple(x.shape[1] for x in xs)
    cast_w = w.dtype != BF16
    assert sum(k_sizes) == k and m % tm == 0 and w.shape[-1] == n
    row = pl.BlockSpec((tm, n), lambda i: (i, 0))
    vec = pl.BlockSpec((1, n), lambda i: (0, 0))
    in_specs = [pl.BlockSpec((tm, ks), lambda i: (i, 0)) for ks in k_sizes]
    in_specs.append(pl.BlockSpec((None, k, n), lambda i: (layer, 0, 0), pipeline_mode=pl.Buffered(1)))
    in_specs += [row, vec, vec]
    return pl.pallas_call(
        functools.partial(_mm_ln_kernel, k_sizes=k_sizes, cast_w=cast_w),
        grid=(m // tm,),
        in_specs=in_specs,
        out_specs=[row, row],
        out_shape=[jax.ShapeDtypeStruct((m, n), F32), jax.ShapeDtypeStruct((m, n), BF16)],
        scratch_shapes=[pltpu.VMEM((k, n), BF16)] if cast_w else [],
        compiler_params=_cparams(1),
    )(*xs, w, resid, g.reshape(1, n), b.reshape(1, n))


def _xattn_kernel(q_ref, k_ref, v_ref, o_ref, *, nb, tq):
    for j in range(nb):
        rows = slice(j * tq, (j + 1) * tq)
        for h in range(XA_H):
            cols = slice(h * XA_HD, (h + 1) * XA_HD)
            s = _dot_nt(q_ref[rows, cols], k_ref[j, :, cols]) * (XA_HD ** -0.5)
            e = jnp.exp(s - jnp.max(s, axis=-1, keepdims=True))
            pr = e / jnp.sum(e, axis=-1, keepdims=True)
            o_ref[rows, cols] = _dot(pr, v_ref[j, :, cols]).astype(o_ref.dtype)


def _xattn_cache_kernel(q_ref, k_ref, v_ref, o_ref, *, nb, tq):
    nr = N_MEM * XA_H
    r = lax.broadcasted_iota(jnp.int32, (nr, XA_H * tq), 0)
    q = lax.broadcasted_iota(jnp.int32, (nr, XA_H * tq), 1)
    own = (r & (XA_H - 1)) == lax.shift_right_arithmetic(q, jnp.int32(int(math.log2(tq))))
    for j in range(nb):
        rows = slice(j * tq, (j + 1) * tq)
        kf = k_ref[j].reshape(nr, XA_HD)
        vf = v_ref[j].reshape(nr, XA_HD)
        qcat = jnp.concatenate([q_ref[rows, h * XA_HD:(h + 1) * XA_HD] for h in range(XA_H)], axis=0)
        s = jnp.where(own, _dot_nt(kf, qcat) * (XA_HD ** -0.5), -jnp.inf)
        e = jnp.exp(s - jnp.max(s, axis=0, keepdims=True))
        pr = e / jnp.sum(e, axis=0, keepdims=True)
        o = _dot_tn(pr, vf)
        for h in range(XA_H):
            o_ref[rows, h * XA_HD:(h + 1) * XA_HD] = o[h * tq:(h + 1) * tq, :].astype(o_ref.dtype)


def _cross_attn(q, mem_k, mem_v, bsz, t, nb, tq, out_dtype, layer=None):
    nt = t // tq
    rows = nb * tq
    if layer is None:
        body = _xattn_kernel
        kv_spec = pl.BlockSpec((nb, N_MEM, D_MODEL), lambda i, j: (i, 0, 0))
    else:
        body = _xattn_cache_kernel
        kv_spec = pl.BlockSpec((None, nb, N_MEM, XA_H, XA_HD), lambda i, j: (layer, i, 0, 0, 0))
    return pl.pallas_call(
        functools.partial(body, nb=nb, tq=tq),
        grid=(bsz // nb, nt),
        in_specs=[pl.BlockSpec((rows, D_MODEL), lambda i, j: (i * nt + j, 0)), kv_spec, kv_spec],
        out_specs=pl.BlockSpec((rows, D_MODEL), lambda i, j: (i * nt + j, 0)),
        out_shape=jax.ShapeDtypeStruct((bsz * t, D_MODEL), out_dtype),
        compiler_params=_cparams(2),
    )(q, mem_k, mem_v)


def _state_io(carry, h0_ref, hout_ref, hst):
    if carry:
        return (lambda s, h: hst[h]), (lambda s, h, val: hst.__setitem__(h, val))
    return (lambda s, h: h0_ref[s, h]), (lambda s, h, val: hout_ref.__setitem__((s, h), val))


def _for_segments(nseg, fn, unroll=True):
    if nseg == 1:
        fn(0)
    else:
        def body(s, c):
            fn(s)
            return c
        lax.fori_loop(0, nseg, body, 0, unroll=unroll)


def _ret_kernel(q_ref, k_ref, v_ref, g_ref, cos_ref, sin_ref, gn_ref, h0_ref, o_ref, hout_ref,
                hst, qs_sc, ks_sc, y_sc, *, rows, seg, carry):
    c = CHUNK
    nseg = c // seg
    if carry:
        @pl.when(pl.program_id(1) == 0)
        def _():
            hst[...] = h0_ref[0]
    get_h, set_h = _state_io(carry, h0_ref, hout_ref, hst)
    incl, _, _, r_i, c_i = _seg_masks(c, seg)
    dpos = (r_i - c_i).astype(F32)
    tau = (lax.broadcasted_iota(jnp.int32, (c, RET_HD), 0) & (seg - 1)).astype(F32)
    even = (lax.broadcasted_iota(jnp.int32, (c, GROUP), 1) & 1) == 0

    for ci in range(rows // c):
        rws = slice(ci * c, (ci + 1) * c)
        cosb = jnp.concatenate([cos_ref[rws, :]] * RET_H, axis=1)
        sinb = jnp.concatenate([sin_ref[rws, :]] * RET_H, axis=1)

        def rot(x):
            swapped = jnp.where(even, pltpu.roll(x, GROUP - 1, 1), pltpu.roll(x, 1, 1))
            return x * cosb + swapped * sinb

        qr = rot(q_ref[rws, :])
        kr = rot(k_ref[rws, :]) * (RET_HD ** -0.5)
        for h in range(RET_H):
            lgam = RET_LOG_GAMMA[h]
            cols = slice(h * RET_HD, (h + 1) * RET_HD)
            qh, kh = qr[:, cols], kr[:, cols]
            dm = jnp.where(incl, jnp.exp(dpos * lgam), 0.0)
            y_sc[...] = _dot(_dot_nt(qh, kh) * dm, v_ref[rws, cols])
            qs_sc[...] = qh * jnp.exp((tau + 1.0) * lgam)
            ks_sc[...] = kh * jnp.exp((seg - 1.0 - tau) * lgam)
            cd = math.exp(seg * lgam)

            def seg_step(s, h=h, cols=cols, cd=cd, ci=ci):
                sr = pl.ds(_mo(s * seg, seg), seg)
                vr = pl.ds(_mo(ci * c + s * seg, seg), seg)
                hs = get_h(s, h)
                y_sc[sr, :] += _dot(qs_sc[sr, :], hs)
                set_h(s, h, cd * hs + _dot_tn(ks_sc[sr, :], v_ref[vr, cols]))

            _for_segments(nseg, seg_step)
            y = y_sc[...]
            yn = y * lax.rsqrt(jnp.mean(y * y, axis=-1, keepdims=True) + 1e-5) * gn_ref[:, cols]
            o_ref[rws, cols] = (yn * _silu(g_ref[rws, cols])).astype(o_ref.dtype)

    if carry:
        @pl.when(pl.program_id(1) == pl.num_programs(1) - 1)
        def _():
            hout_ref[0] = hst[...]


def _mamba_kernel(z_ref, xbc_ref, dt_ref, cprev_ref, h0_ref, cw_ref, cb_ref, dtb_ref, alog_ref, dd_ref,
                  ng_ref, e64_ref, e128_ref, o_ref, cout_ref, hout_ref,
                  hst, xpad, act_sc, qs_sc, ks_sc, xdt_sc, xs_sc, yi_sc, yf_sc, etot_sc, *, rows, seg, carry):
    c = CHUNK
    nseg = c // seg
    get_h, set_h = _state_io(carry, h0_ref, hout_ref, hst)

    def conv_act(window, n):
        acc = cb_ref[...] + window[5:5 + n] * cw_ref[0:1, :]
        for i in range(1, MB_CONV):
            acc = acc + window[5 + i:5 + i + n] * cw_ref[i:i + 1, :]
        return _silu(acc)

    if carry:
        @pl.when(pl.program_id(1) == 0)
        def _():
            hst[...] = h0_ref[0]
            xpad[0:8, :] = jnp.zeros((8, MB_CONV_DIM), F32)
            xpad[5:8, :] = cprev_ref[0]

        xpad[8:8 + rows, :] = xbc_ref[...]
        for ci in range(rows // c):
            acc = cb_ref[...] + xpad[ci * c + 5:ci * c + 5 + c, :] * cw_ref[0:1, :]
            for i in range(1, MB_CONV):
                acc = acc + xpad[ci * c + 5 + i:ci * c + 5 + i + c, :] * cw_ref[i:i + 1, :]
            act_sc[ci * c:(ci + 1) * c, :] = _silu(acc)
        xpad[0:8, :] = xpad[rows:rows + 8, :]

        @pl.when(pl.program_id(1) == pl.num_programs(1) - 1)
        def _():
            cout_ref[0] = xpad[5:8, :]
    else:
        def conv_seq(s, carry_):
            sr = pl.ds(pl.multiple_of(s * seg, seg), seg)
            xpad[5:8, :] = cprev_ref[s]
            xs = xbc_ref[sr, :]
            window = jnp.concatenate([xpad[0:8, :], xs], axis=0)
            act_sc[sr, :] = conv_act(window, seg)
            cout_ref[s] = xs[seg - 3:seg]
            return carry_

        xpad[0:8, :] = jnp.zeros((8, MB_CONV_DIM), F32)
        lax.fori_loop(0, rows // seg, conv_seq, 0)

    incl, _, same, _, _ = _seg_masks(c, seg)
    lt_bf = jnp.where(incl, 1.0, 0.0).astype(BF16)
    same_bf = jnp.where(same, 1.0, 0.0).astype(BF16)
    a_neg = -jnp.exp(alog_ref[...])

    for ci in range(rows // c):
        rws = slice(ci * c, (ci + 1) * c)
        dtv = _softplus(dt_ref[rws, :] + dtb_ref[...])
        la = dtv * a_neg
        cum = _split_dot(lt_bf, la)
        tot = _split_dot(same_bf, la)
        cum_t = cum.T
        etot_sc[...] = jnp.exp(tot)
        cum_cols = _spread(cum, e128_ref[...])
        cum_full = _spread(cum, e64_ref[...])
        xh_all = act_sc[rws, 0:GROUP]
        xdt_all = xh_all * _spread(dtv, e64_ref[...])
        xdt_sc[...] = xdt_all
        xs_sc[...] = xdt_all * jnp.exp(_spread(tot, e64_ref[...]) - cum_full)
        for g in range(MB_G):
            cg = act_sc[rws, 768 + g * MB_N:768 + (g + 1) * MB_N]
            bg = act_sc[rws, 512 + g * MB_N:512 + (g + 1) * MB_N]
            gmat = _dot_nt(cg, bg)
            qs_sc[...] = cg
            ks_sc[...] = bg
            for hh in range(MB_H // MB_G):
                h = g * (MB_H // MB_G) + hh
                cols = slice(h * MB_HD, (h + 1) * MB_HD)
                lmat = jnp.exp(jnp.where(incl, cum_cols[:, h * c:(h + 1) * c] - cum_t[h:h + 1, :], -jnp.inf))
                yf_sc[:, cols] = _dot(gmat * lmat, xdt_sc[:, cols])

                def seg_step(s, h=h, cols=cols):
                    sr = pl.ds(_mo(s * seg, seg), seg)
                    first = pl.ds(_mo(s * seg, seg), 1)
                    hs = get_h(s, h)
                    yi_sc[sr, cols] = _dot_nt(qs_sc[sr, :], hs)
                    set_h(s, h, etot_sc[first, h:h + 1] * hs + _dot_tn(xs_sc[sr, cols], ks_sc[sr, :]))

                _for_segments(nseg, seg_step)
        y_all = yf_sc[...] + yi_sc[...] * jnp.exp(cum_full) + dd_ref[...] * xh_all
        yz = y_all * _silu(z_ref[rws, :])
        gw = GROUP // MB_G
        for g in range(MB_G):
            cols = slice(g * gw, (g + 1) * gw)
            part = yz[:, cols]
            nrm = part * lax.rsqrt(jnp.mean(part * part, axis=-1, keepdims=True) + 1e-5) * ng_ref[:, cols]
            o_ref[rws, cols] = nrm.astype(o_ref.dtype)

    if carry:
        @pl.when(pl.program_id(1) == pl.num_programs(1) - 1)
        def _():
            hout_ref[0] = hst[...]


def _gla_kernel(q_ref, k_ref, v_ref, gk_ref, g_ref, h0_ref, gkup_ref, gkb_ref, ng_ref, e_ref,
                o_ref, hout_ref,
                hst, qe_sc, ke_sc, etot_sc, p_sc, y_sc, *, rows, seg, carry):
    c = CHUNK
    nsub = c // seg
    if carry:
        @pl.when(pl.program_id(1) == 0)
        def _():
            for h in range(GLA_H):
                hst[h] = h0_ref[0, h].T
    incl, _, same, _, _ = _seg_masks(c, seg)
    lt_bf = jnp.where(incl, 1.0, 0.0).astype(BF16)
    same_bf = jnp.where(same, 1.0, 0.0).astype(BF16)
    row_i = lax.broadcasted_iota(jnp.int32, (seg, GLA_QK), 0)
    kcs = [slice(h * GLA_DK, (h + 1) * GLA_DK) for h in range(GLA_H)]
    vcs = [slice(h * GLA_DV, (h + 1) * GLA_DV) for h in range(GLA_H)]

    for ci in range(rows // c):
        rws = slice(ci * c, (ci + 1) * c)
        pre = _dot(gk_ref[rws, :], gkup_ref[...]) + gkb_ref[...]
        la = -_softplus(-pre) * (1.0 / GLA_TAU)
        cum = _split_dot(lt_bf, la)
        tot = _split_dot(same_bf, la)
        qv = q_ref[rws, :] * (GLA_DK ** -0.5)
        kv = k_ref[rws, :]
        qe_sc[...] = qv * jnp.exp(cum)
        ke_sc[...] = kv * jnp.exp(tot - cum)
        etot_sc[...] = jnp.exp(tot)

        safe = (jnp.max(-cum) < GLA_SAFE_EXP) if carry else False

        def factored():
            kinv = kv * jnp.exp(-cum)
            for h in range(GLA_H):
                sc = jnp.where(incl, _dot_nt(qe_sc[:, kcs[h]], kinv[:, kcs[h]]), 0.0)
                y_sc[:, vcs[h]] = _dot(sc, v_ref[rws, vcs[h]])

        def pairwise():
            for u in range(nsub):
                sr = slice(u * seg, (u + 1) * seg)
                cu, qu = cum[sr], qv[sr]
                for j in range(seg):
                    r0 = u * seg + j
                    pj = qu * jnp.exp(jnp.where(row_i >= j, cu - cum[r0:r0 + 1], -jnp.inf)) * kv[r0:r0 + 1]
                    p_sc[u, j * seg:(j + 1) * seg, :] = pj
            for u in range(nsub):
                rm = _dot(p_sc[u], e_ref[...])
                y = rm[0:seg, :] * v_ref[ci * c + u * seg:ci * c + u * seg + 1, :]
                for j in range(1, seg):
                    r0 = ci * c + u * seg + j
                    y = y + rm[j * seg:(j + 1) * seg, :] * v_ref[r0:r0 + 1, :]
                y_sc[u * seg:(u + 1) * seg, :] = y

        if carry:
            pl.when(safe)(factored)
            pl.when(jnp.logical_not(safe))(pairwise)
        else:
            pairwise()
        states = [hst[h] for h in range(GLA_H)] if carry else None
        for u in range(nsub):
            sr = slice(u * seg, (u + 1) * seg)
            vr = slice(ci * c + u * seg, ci * c + (u + 1) * seg)
            hts = [states[h] if carry else h0_ref[u, h].T for h in range(GLA_H)]
            for h in range(GLA_H):
                y_sc[sr, vcs[h]] += _dot_nt(qe_sc[sr, kcs[h]], hts[h])
            upds = [_dot_tn(v_ref[vr, vcs[h]], ke_sc[sr, kcs[h]]) for h in range(GLA_H)]
            for h in range(GLA_H):
                new = hts[h] * etot_sc[u * seg:u * seg + 1, kcs[h]] + upds[h]
                if carry:
                    states[h] = new
                else:
                    hout_ref[u, h] = new.T
        if carry:
            for h in range(GLA_H):
                hst[h] = states[h]

        for h in range(GLA_H):
            y = y_sc[:, vcs[h]]
            yn = y * lax.rsqrt(jnp.mean(y * y, axis=-1, keepdims=True) + 1e-5) * ng_ref[:, vcs[h]]
            o_ref[rws, vcs[h]] = (yn * _silu(g_ref[rws, vcs[h]])).astype(o_ref.dtype)

    if carry:
        @pl.when(pl.program_id(1) == pl.num_programs(1) - 1)
        def _():
            for h in range(GLA_H):
                hout_ref[0, h] = hst[h].T


def _rwkv_kernel(p_ref, sp_ref, h0_ref, mu_ref, w0_ref, wup_ref, a0_ref, aup_ref, gup_ref, kk_ref, ka_ref,
                 rk_ref, lng_ref, lnb_ref, ones_ref, o_ref, hout_ref,
                 hst, last_sc, prev_sc, w_sc, y_sc, bt_sc, kt_sc, v_sc, rt_sc, u_sc, rkb_sc, etot_sc, out_sc, kap_sc,
                 pw_sc, t_sc, ak_sc, pb_sc, pk_sc, *, rows, seg, sub, carry):
    c = CHUNK
    assert rows == c
    nsub = c // sub
    n_iter = int(math.log2(sub)) - 1
    npair = RW_H // 2
    pw_ = 2 * RW_HD
    pr_i = lax.broadcasted_iota(jnp.int32, (pw_, pw_), 0)
    pc_i = lax.broadcasted_iota(jnp.int32, (pw_, pw_), 1)
    bd_mask = (pr_i >= RW_HD) == (pc_i >= RW_HD)
    low_rows = lax.broadcasted_iota(jnp.int32, (pw_, RW_HD), 0) < RW_HD

    def to_bd(stack):
        return jnp.where(bd_mask, jnp.concatenate([stack, stack], axis=1), 0.0)

    def from_bd(bd):
        return jnp.where(low_rows, bd[:, 0:RW_HD], bd[:, RW_HD:pw_])

    if carry:
        @pl.when(pl.program_id(1) == 0)
        def _():
            for pi in range(npair):
                hst[pi] = to_bd(h0_ref[0, pi])
            last_sc[...] = jnp.broadcast_to(sp_ref[0], (8, RW_IN))

        prev_row = jnp.broadcast_to(last_sc[0:1, :], (c, RW_IN))
    else:
        for s in range(rows // seg):
            prev_sc[s * seg:(s + 1) * seg, :] = jnp.broadcast_to(sp_ref[s], (seg, RW_IN))
        prev_row = prev_sc[...]
    incl, strict, same, r_i, c_i = _seg_masks(c, sub)
    lt_bf = jnp.where(incl, 1.0, 0.0).astype(BF16)
    same_bf = jnp.where(same, 1.0, 0.0).astype(BF16)
    eye = jnp.where(r_i == c_i, 1.0, 0.0)
    row_w = lax.broadcasted_iota(jnp.int32, (c, RW_IN), 0)
    first_row = (row_w & ((c if carry else seg) - 1)) == 0

    p = p_ref[...]
    prev = jnp.where(first_row, prev_row, pltpu.roll(p, 1, 0))
    xs = p + (prev - p) * mu_ref[...]
    r = xs[:, 0:GROUP]
    k = xs[:, GROUP:2 * GROUP]
    v = xs[:, 2 * GROUP:3 * GROUP]
    wa = xs[:, 3 * GROUP:3 * GROUP + 128]
    gd = xs[:, 3 * GROUP + 128:RW_IN]
    w = -_softplus(-(w0_ref[...] + _dot(jnp.tanh(wa), wup_ref[...]))) - 0.5
    ld = -jnp.exp(w)
    a = _sigmoid(a0_ref[...] + _dot(wa, aup_ref[...]))
    gate = _dot(_sigmoid(gd), gup_ref[...])
    kkr = k * kk_ref[...]
    k2 = k * (1.0 + (a - 1.0) * ka_ref[...])
    cum = _split_dot(lt_bf, ld)
    tot = _split_dot(same_bf, ld)
    ecum = jnp.exp(cum)
    einv = jnp.exp(-cum)
    eprev = jnp.exp(cum - ld)
    rkb_sc[...] = r * k2 * rk_ref[...]
    etot_sc[...] = jnp.exp(tot)
    v_sc[...] = v
    rtil = r * ecum
    rt_sc[...] = rtil
    kt_sc[...] = k2 * einv
    rn = lax.rsqrt(jnp.maximum(_head_sums(kkr * kkr, ones_ref[...]), 1e-24))
    braw = kkr * a * einv * rn
    kraw = kkr * eprev * rn

    heads = [slice(h * RW_HD, (h + 1) * RW_HD) for h in range(RW_H)]
    for h, cols in enumerate(heads):
        kap = kraw[:, cols]
        bt = braw[:, cols]
        kap_sc[:, cols] = kap
        bt_sc[:, cols] = bt
        m1 = _dot_nt(jnp.concatenate([kap, rtil[:, cols]], axis=0), jnp.concatenate([bt, kt_sc[:, cols]], axis=0))
        x = jnp.where(strict, -m1[0:c, 0:c], 0.0)
        pw_sc[0, h] = x
        t_sc[h] = eye + x
        ak_sc[h] = jnp.where(strict, m1[0:c, c:2 * c], 0.0)
        pb_sc[h] = jnp.where(incl, m1[c:2 * c, 0:c], 0.0)
        pk_sc[h] = jnp.where(incl, m1[c:2 * c, c:2 * c], 0.0)
    for h, cols in enumerate(heads):
        y_sc[:, cols] = _dot(ak_sc[h], v_sc[:, cols])
    for it in range(n_iter):
        src, dst = it % 2, (it + 1) % 2
        for h in range(RW_H):
            pw = pw_sc[src, h]
            pw_sc[dst, h] = _dot(pw, pw)
        for h in range(RW_H):
            tm = t_sc[h]
            t_sc[h] = tm + _dot(tm, pw_sc[dst, h])
    for h, cols in enumerate(heads):
        wy = _dot(t_sc[h], jnp.concatenate([kap_sc[:, cols], y_sc[:, cols]], axis=1))
        w_sc[:, cols] = wy[:, 0:RW_HD]
        y_sc[:, cols] = wy[:, RW_HD:2 * RW_HD]

    states = [hst[pi] for pi in range(npair)] if carry else None
    units = [(s, pi) for s in range(nsub) for pi in range(npair)]
    group = npair if carry else 2 * npair
    for g0 in range(0, len(units), group):
        grp_units = units[g0:g0 + group]
        sts, wrs, uus, upds = [], [], [], []
        for s, pi in grp_units:
            sts.append(states[pi] if carry else to_bd(h0_ref[s, pi]))
        for (s, pi), st in zip(grp_units, sts):
            sr, pc = slice(s * sub, (s + 1) * sub), slice(pi * pw_, (pi + 1) * pw_)
            wrs.append(_dot_nt(jnp.concatenate([w_sc[sr, pc], rt_sc[sr, pc]], axis=0), st))
        for (s, pi), wr in zip(grp_units, wrs):
            sr, pc = slice(s * sub, (s + 1) * sub), slice(pi * pw_, (pi + 1) * pw_)
            uu = -wr[0:sub, :] - y_sc[sr, pc]
            u_sc[sr, pc] = uu
            out_sc[sr, pc] = wr[sub:2 * sub, :]
            uus.append(uu)
        for (s, pi), uu in zip(grp_units, uus):
            sr, pc = slice(s * sub, (s + 1) * sub), slice(pi * pw_, (pi + 1) * pw_)
            upds.append(_dot_tn(jnp.concatenate([uu, v_sc[sr, pc]], axis=0),
                                jnp.concatenate([bt_sc[sr, pc], kt_sc[sr, pc]], axis=0)))
        for (s, pi), st, upd in zip(grp_units, sts, upds):
            pc = slice(pi * pw_, (pi + 1) * pw_)
            new = (st + jnp.where(bd_mask, upd, 0.0)) * etot_sc[s * sub:s * sub + 1, pc]
            if carry:
                states[pi] = new
            else:
                hout_ref[s, pi] = from_bd(new)
    if carry:
        for pi in range(npair):
            hst[pi] = states[pi]

    for h, cols in enumerate(heads):
        out_sc[:, cols] += _dot(jnp.concatenate([pb_sc[h], pk_sc[h]], axis=1),
                                jnp.concatenate([u_sc[:, cols], v_sc[:, cols]], axis=0))
    ones = ones_ref[...]
    o_all = out_sc[...]
    oc = o_all - _head_sums(o_all, ones) * (1.0 / RW_HD)
    var = _head_sums(oc * oc, ones) * (1.0 / RW_HD)
    on = oc * lax.rsqrt(var + RW_LN_EPS) * lng_ref[...] + lnb_ref[...]
    out_sc[...] = on + _head_sums(rkb_sc[...], ones) * v_sc[...]
    o_ref[...] = (out_sc[...] * gate).astype(o_ref.dtype)

    if carry:
        last_sc[...] = jnp.broadcast_to(p_ref[rows - 1:rows, :], (8, RW_IN))

        @pl.when(pl.program_id(1) == pl.num_programs(1) - 1)
        def _():
            for pi in range(npair):
                hout_ref[0, pi] = from_bd(hst[pi])


class _Group:
    def __init__(self, bsz, t, rows):
        self.bsz, self.t, self.rows = bsz, t, rows
        self.carry = t >= CHUNK
        self.nb = 1 if self.carry else rows // t
        self.nt = t // rows if self.carry else 1
        self.grid = (bsz // self.nb, self.nt)
        self.seg = CHUNK if self.carry else t

    def rows_spec(self, width, col_block):
        nt = self.nt
        return pl.BlockSpec((self.rows, width), lambda i, j: (i * nt + j, col_block))

    def state_spec(self, shape, layer=None):
        zeros = (0,) * len(shape)
        if layer is None:
            return pl.BlockSpec((self.nb,) + tuple(shape), lambda i, j: (i,) + zeros)
        return pl.BlockSpec((None, self.nb) + tuple(shape), lambda i, j: (layer, i) + zeros)

    def out_rows(self, width, dtype=BF16):
        nt = self.nt
        return (pl.BlockSpec((self.rows, width), lambda i, j: (i * nt + j, 0)),
                jax.ShapeDtypeStruct((self.bsz * self.t, width), dtype))


def _vec_spec(shape):
    zeros = (0,) * len(shape)
    return pl.BlockSpec(tuple(shape), lambda i, j: zeros)


def _retention(grp, p, cos_t, sin_t, h0, sl, gn):
    c = CHUNK
    shape = (RET_H, RET_HD, RET_HD)
    in_specs = [grp.rows_spec(GROUP, COL_RET // GROUP + n) for n in range(4)]
    tab = pl.BlockSpec((grp.rows, RET_HD), (lambda i, j: (j, 0)) if grp.carry else (lambda i, j: (0, 0)))
    in_specs += [tab, tab, _vec_spec((1, GROUP)), grp.state_spec(shape, sl)]
    o_spec, o_shape = grp.out_rows(GROUP)
    return pl.pallas_call(
        functools.partial(_ret_kernel, rows=grp.rows, seg=grp.seg, carry=grp.carry),
        grid=grp.grid,
        in_specs=in_specs,
        out_specs=[o_spec, grp.state_spec(shape)],
        out_shape=[o_shape, jax.ShapeDtypeStruct(h0.shape[1:], F32)],
        scratch_shapes=[pltpu.VMEM(shape, F32), pltpu.VMEM((c, RET_HD), F32),
                        pltpu.VMEM((c, RET_HD), F32), pltpu.VMEM((c, RET_HD), F32)],
        compiler_params=_cparams(2),
    )(p, p, p, p, cos_t, sin_t, gn.reshape(1, GROUP), h0)


def _mamba(grp, p, cprev, h0, sl, lp):
    c = CHUNK
    cshape, hshape = (MB_CONV - 1, MB_CONV_DIM), (MB_H, MB_HD, MB_N)
    h0 = jnp.swapaxes(h0, -1, -2)
    pad8 = lambda a: jnp.pad(a.reshape(1, MB_H), ((0, 0), (0, 128 - MB_H)))
    head = jnp.arange(128)[:, None]
    e64 = (head == jnp.arange(GROUP)[None, :] // MB_HD).astype(BF16)
    e128 = (head == jnp.arange(MB_H * CHUNK)[None, :] // CHUNK).astype(BF16)
    in_specs = [grp.rows_spec(GROUP, COL_MB_Z // GROUP), grp.rows_spec(MB_CONV_DIM, COL_MB_XBC // MB_CONV_DIM),
                grp.rows_spec(128, COL_MB_DT // 128),
                grp.state_spec(cshape, sl), grp.state_spec(hshape, sl),
                _vec_spec((MB_CONV, MB_CONV_DIM)), _vec_spec((1, MB_CONV_DIM)), _vec_spec((1, 128)),
                _vec_spec((1, 128)), _vec_spec((1, GROUP)), _vec_spec((1, GROUP)),
                _vec_spec((128, GROUP)), _vec_spec((128, MB_H * CHUNK))]
    o_spec, o_shape = grp.out_rows(GROUP)
    return pl.pallas_call(
        functools.partial(_mamba_kernel, rows=grp.rows, seg=grp.seg, carry=grp.carry),
        grid=grp.grid,
        in_specs=in_specs,
        out_specs=[o_spec, grp.state_spec(cshape), grp.state_spec(hshape)],
        out_shape=[o_shape, jax.ShapeDtypeStruct(cprev.shape[1:], F32), jax.ShapeDtypeStruct(h0.shape[1:], F32)],
        scratch_shapes=[pltpu.VMEM(hshape, F32),
                        pltpu.VMEM(((grp.rows if grp.carry else 0) + 8, MB_CONV_DIM), F32),
                        pltpu.VMEM((grp.rows, MB_CONV_DIM), F32),
                        pltpu.VMEM((c, MB_N), F32), pltpu.VMEM((c, MB_N), F32), pltpu.VMEM((c, GROUP), F32),
                        pltpu.VMEM((c, GROUP), F32), pltpu.VMEM((c, GROUP), F32), pltpu.VMEM((c, GROUP), F32),
                        pltpu.VMEM((c, 128), F32)],
        compiler_params=_cparams(2),
    )(p, p, p, cprev, h0, lp['mb_conv_w'], lp['mb_conv_b'].reshape(1, MB_CONV_DIM), pad8(lp['mb_dt_bias']),
      pad8(lp['mb_a_log']), jnp.repeat(lp['mb_d'], MB_HD).reshape(1, GROUP), lp['mb_norm_g'].reshape(1, GROUP),
      e64, e128)


def _gla(grp, p, h0, sl, lp, expand):
    c = CHUNK
    shape = (GLA_H, GLA_DK, GLA_DV)
    sub = 64 if grp.carry else grp.seg
    gk_up = jnp.pad(lp['gla_gk_up'], ((0, 128 - GLA_LORA), (0, 0)))
    in_specs = [grp.rows_spec(GLA_QK, COL_GLA_Q // GLA_QK), grp.rows_spec(GLA_QK, COL_GLA_K // GLA_QK),
                grp.rows_spec(GROUP, COL_GLA_V // GROUP), grp.rows_spec(128, COL_GLA_GK // 128),
                grp.rows_spec(GROUP, COL_GLA_G // GROUP), grp.state_spec(shape, sl),
                _vec_spec((128, GLA_QK)), _vec_spec((1, GLA_QK)), _vec_spec((1, GROUP)), _vec_spec((GLA_QK, GROUP))]
    o_spec, o_shape = grp.out_rows(GROUP)
    return pl.pallas_call(
        functools.partial(_gla_kernel, rows=grp.rows, seg=sub, carry=grp.carry),
        grid=grp.grid,
        in_specs=in_specs,
        out_specs=[o_spec, grp.state_spec(shape)],
        out_shape=[o_shape, jax.ShapeDtypeStruct(h0.shape[1:], F32)],
        scratch_shapes=[pltpu.VMEM((GLA_H, GLA_DV, GLA_DK), F32)]
        + [pltpu.VMEM((c, GLA_QK), F32) for _ in range(3)]
        + [pltpu.VMEM((c // sub, sub * sub, GLA_QK), F32), pltpu.VMEM((c, GROUP), F32)],
        compiler_params=_cparams(2),
    )(p, p, p, p, p, h0, gk_up, lp['gla_gk_b'].reshape(1, GLA_QK), lp['gla_norm_g'].reshape(1, GROUP), expand)


def _rwkv(grp, p, shift_prev, h0, sl, lp):
    c = CHUNK
    head_ones = (jnp.arange(GROUP)[:, None] // RW_HD == jnp.arange(GROUP)[None, :] // RW_HD).astype(BF16)
    shape = (RW_H // 2, 2 * RW_HD, RW_HD)
    h0 = h0.reshape(h0.shape[:2] + shape)
    sub = 64 if grp.carry else grp.seg
    row = lambda a: a.reshape(1, -1)
    w_up = jnp.pad(lp['rw_w_up'], ((0, 64), (0, 0)))
    a_up = jnp.pad(lp['rw_a_up'], ((64, 0), (0, 0)))
    in_specs = [grp.rows_spec(RW_IN, 0), grp.state_spec((1, RW_IN), sl), grp.state_spec(shape, sl),
                _vec_spec((1, RW_IN)), _vec_spec((1, GROUP)), _vec_spec((128, GROUP)), _vec_spec((1, GROUP)),
                _vec_spec((128, GROUP)), _vec_spec((128, GROUP))] + [_vec_spec((1, GROUP))] * 5 + [_vec_spec((GROUP, GROUP))]
    o_spec, o_shape = grp.out_rows(GROUP)
    wide = lambda: pltpu.VMEM((c, GROUP), F32)
    return pl.pallas_call(
        functools.partial(_rwkv_kernel, rows=grp.rows, seg=grp.seg, sub=sub, carry=grp.carry),
        grid=grp.grid,
        in_specs=in_specs,
        out_specs=[o_spec, grp.state_spec(shape)],
        out_shape=[o_shape, jax.ShapeDtypeStruct(h0.shape[1:], F32)],
        scratch_shapes=[pltpu.VMEM((RW_H // 2, 2 * RW_HD, 2 * RW_HD), F32), pltpu.VMEM((8, RW_IN), F32),
                        pltpu.VMEM((c, RW_IN), F32)] + [wide() for _ in range(11)]
        + [pltpu.VMEM((2, RW_H, c, c), F32)] + [pltpu.VMEM((RW_H, c, c), F32) for _ in range(4)],
        compiler_params=_cparams(2),
    )(p, shift_prev.reshape(shift_prev.shape[0], -1, 1, RW_IN), h0, row(lp['rw_mu']), row(lp['rw_w0']), w_up,
      row(lp['rw_a0']), a_up, lp['rw_g_up'], row(lp['rw_k_k']), row(lp['rw_k_a']), row(lp['rw_r_k']),
      row(lp['rw_ln_g']), row(lp['rw_ln_b']), head_ones)


def _rope_tables(pos0, t):
    half = RET_HD // 2
    inv = 1.0 / (ROPE_BASE ** jnp.linspace(0.0, 1.0, half, dtype=F32))
    pos = pos0 + jnp.arange(t, dtype=F32)
    ang = pos[:, None] * inv[None, :]
    cos, sin = jnp.cos(ang), jnp.sin(ang)
    cos_t = jnp.stack([cos, cos], axis=-1).reshape(t, RET_HD)
    sin_t = jnp.stack([-sin, sin], axis=-1).reshape(t, RET_HD)
    return cos_t, sin_t


def _pad_w_in(w):
    z = lambda n: jnp.zeros(w.shape[:-1] + (n,), w.dtype)
    return jnp.concatenate([
        w[..., 0:1792], w[..., 3328:3336], z(120), w[..., 4360:4376], z(112), w[..., 2304:3328],
        w[..., 1792:2304], w[..., 3848:4360], w[..., 4376:4888], w[..., 3336:3592], w[..., 3592:3848],
        w[..., 4888:6936]], axis=-1)


def _layer(grp, grp_rw, x, x_bf, pos_tabs, states, sl, mem_k, mem_v, cache_layer, lp, big, wl, expand, tm):
    rw_shift, rw_state, mb_conv, mb_state, gla_state, ret_state = states
    bsz, t = grp.bsz, grp.t
    tm_ln = 512 if grp.carry else 256
    tm_big = 2 * tm if grp.carry else tm
    p = _matmul([x_bf], big['w_in'], F32, tm, 1024, wl)
    o_rw, rw_new = _rwkv(grp_rw, p, rw_shift, rw_state, sl, lp)
    rw_new = rw_new.reshape(bsz, RW_H, RW_HD, RW_HD)
    o_mb, conv_new, mb_new = _mamba(grp, p, mb_conv, mb_state, sl, lp)
    mb_new = jnp.swapaxes(mb_new, -1, -2)
    o_gl, gla_new = _gla(grp, p, gla_state, sl, lp, expand)
    o_rt, ret_new = _retention(grp, p, pos_tabs[0], pos_tabs[1], ret_state, sl, lp['ret_norm_g'])
    shift_new = p.reshape(bsz, t, N_PAD)[:, t - 1, 0:RW_IN]
    x, x_bf = _matmul_res_ln([o_rw, o_mb, o_gl, o_rt], big['w_out'], wl, x, lp['ln1_g'], lp['ln1_b'], tm_ln)
    q = _matmul([x_bf], big['xa_wq'], BF16 if grp.carry else F32, tm_big, 1024, wl)
    if grp.carry:
        att = _cross_attn(q, mem_k, mem_v, bsz, t, 1, 1024, BF16)
    else:
        att = _cross_attn(q, mem_k, mem_v, bsz, t, 4, t, F32, layer=cache_layer)
    x, x_bf = _matmul_res_ln([att], big['xa_wo'], wl, x, lp['ln2_g'], lp['ln2_b'], tm_ln)
    hid = _ffn_gate_up(x_bf, big['ffn_w_gate'], big['ffn_w_up'], tm_big, 512, wl)
    x, x_bf = _matmul_res_ln([hid], big['ffn_w_down'], wl, x, lp['ln3_g'], lp['ln3_b'], 256)
    return x, x_bf, (shift_new, rw_new, conv_new, mb_new, gla_new, ret_new)


def kernel(x_prompt, x_sample, state_rwkv_shift, state_rwkv_wkv, state_mamba_conv, state_mamba_ssm, state_gla,
           state_ret, cache_mem_k, cache_mem_v, mem_prompt, w_in, w_out, ln1_g, ln1_b, rw_mu, rw_w0, rw_w_up,
           rw_a0, rw_a_up, rw_g_up, rw_k_k, rw_k_a, rw_r_k, rw_ln_g, rw_ln_b, mb_conv_w, mb_conv_b, mb_dt_bias,
           mb_a_log, mb_d, mb_norm_g, gla_gk_up, gla_gk_b, gla_norm_g, ret_norm_g, ln2_g, ln2_b, xa_wq, xa_wk,
           xa_wv, xa_wo, ln3_g, ln3_b, ffn_w_gate, ffn_w_up, ffn_w_down):
    small = dict(
        ln1_g=ln1_g, ln1_b=ln1_b, rw_mu=rw_mu, rw_w0=rw_w0, rw_w_up=rw_w_up, rw_a0=rw_a0,
        rw_a_up=rw_a_up, rw_g_up=rw_g_up, rw_k_k=rw_k_k, rw_k_a=rw_k_a, rw_r_k=rw_r_k, rw_ln_g=rw_ln_g,
        rw_ln_b=rw_ln_b, mb_conv_w=mb_conv_w, mb_conv_b=mb_conv_b, mb_dt_bias=mb_dt_bias, mb_a_log=mb_a_log,
        mb_d=mb_d, mb_norm_g=mb_norm_g, gla_gk_up=gla_gk_up, gla_gk_b=gla_gk_b, gla_norm_g=gla_norm_g,
        ret_norm_g=ret_norm_g, ln2_g=ln2_g, ln2_b=ln2_b, ln3_g=ln3_g, ln3_b=ln3_b)
    big = dict(w_in=_pad_w_in(w_in), w_out=w_out, xa_wq=xa_wq, xa_wo=xa_wo, ffn_w_gate=ffn_w_gate,
               ffn_w_up=ffn_w_up, ffn_w_down=ffn_w_down.astype(BF16))
    bp, tp, _ = x_prompt.shape
    bs, ts, _ = x_sample.shape
    gp = _Group(bp, tp, 512)
    gp_rw = _Group(bp, tp, CHUNK)
    gs = _Group(bs, ts, CHUNK)
    expand = (jnp.arange(GLA_QK)[:, None] // GLA_DK == jnp.arange(GROUP)[None, :] // GLA_DV).astype(BF16)
    tabs_p = _rope_tables(0.0, tp)
    tabs_s = tuple(jnp.tile(tb, (gs.nb, 1)) for tb in _rope_tables(float(PAST_LEN), ts))
    zeros_p = (jnp.zeros((1, bp, RW_IN), F32), jnp.zeros((1, bp, RW_H, RW_HD, RW_HD), F32),
               jnp.zeros((1, bp, MB_CONV - 1, MB_CONV_DIM), F32), jnp.zeros((1, bp, MB_H, MB_N, MB_HD), F32),
               jnp.zeros((1, bp, GLA_H, GLA_DK, GLA_DV), F32), jnp.zeros((1, bp, RET_H, RET_HD, RET_HD), F32))
    st_s_in = (state_rwkv_shift, state_rwkv_wkv, state_mamba_conv, state_mamba_ssm, state_gla, state_ret)

    yp = x_prompt.reshape(bp * tp, D_MODEL)
    ys = x_sample.reshape(bs * ts, D_MODEL)
    yp_bf, ys_bf = yp.astype(BF16), ys.astype(BF16)
    mem_bf = mem_prompt.reshape(bp * N_MEM, D_MODEL).astype(BF16)
    outs_p = [[] for _ in range(8)]
    outs_s = [[] for _ in range(6)]
    for i in range(DEPTH):
        lp = {name: val[i] for name, val in small.items()}
        mk = _matmul([mem_bf], xa_wk, F32, 1024, 512, i)
        mv = _matmul([mem_bf], xa_wv, F32, 1024, 512, i)
        yp, yp_bf, st_p = _layer(gp, gp_rw, yp, yp_bf, tabs_p, zeros_p, 0, mk.reshape(bp, N_MEM, D_MODEL),
                                 mv.reshape(bp, N_MEM, D_MODEL), None, lp, big, i, expand, 1024)
        for lst, val in zip(outs_p, st_p + (mk.reshape(bp, N_MEM, XA_H, XA_HD), mv.reshape(bp, N_MEM, XA_H, XA_HD))):
            lst.append(val)
        ys, ys_bf, st_s = _layer(gs, gs, ys, ys_bf, tabs_s, st_s_in, i, cache_mem_k, cache_mem_v, i, lp, big, i,
                                 expand, 1024)
        for lst, val in zip(outs_s, st_s):
            lst.append(val)
    return (yp.reshape(bp, tp, D_MODEL), ys.reshape(bs, ts, D_MODEL),
            *[jnp.stack(v) for v in outs_p], *[jnp.stack(v) for v in outs_s])
```

```python
import functools
import math

import jax
import jax.numpy as jnp
from jax import lax
from jax.experimental import pallas as pl
from jax.experimental.pallas import tpu as pltpu

F32 = jnp.float32
BF16 = jnp.bfloat16

D_MODEL = 2048
DEPTH = 2
PAST_LEN = 16384
GROUP = 512
RW_H, RW_HD = 8, 64
RW_IN = 1792
RW_LN_EPS = 64e-5
MB_H, MB_HD, MB_N, MB_G = 8, 64, 128, 2
MB_CONV = 4
MB_CONV_DIM = 1024
GLA_H, GLA_DK, GLA_DV = 4, 64, 128
GLA_QK = 256
GLA_LORA = 16
GLA_TAU = 16.0
GLA_SAFE_EXP = 60.0
RET_H, RET_HD = 4, 128
ROPE_BASE = 10000.0
N_MEM = 256
XA_H, XA_HD = 4, 512
D_FF = 5632
DN_ALPHA = (2 * DEPTH) ** 0.25
RET_LOG_GAMMA = tuple(math.log1p(-(2.0 ** (-5.0 - h))) for h in range(RET_H))

N_PAD = 7168
COL_MB_DT = 1792
COL_GLA_GK = 1920
COL_MB_XBC = 2048
COL_MB_Z = 3072
COL_GLA_V = 3584
COL_GLA_G = 4096
COL_GLA_Q = 4608
COL_GLA_K = 4864
COL_RET = 5120

CHUNK = 128
VMEM_LIMIT = 56 * 1024 * 1024


def _cparams(n_axes, vmem=VMEM_LIMIT):
    return pltpu.CompilerParams(dimension_semantics=("arbitrary",) * n_axes, vmem_limit_bytes=vmem)


def _dot(a, b):
    return jnp.dot(a.astype(BF16), b.astype(BF16), preferred_element_type=F32)


def _dot_nt(a, b):
    return lax.dot_general(a.astype(BF16), b.astype(BF16), (((1,), (1,)), ((), ())), preferred_element_type=F32)


def _dot_tn(a, b):
    return lax.dot_general(a.astype(BF16), b.astype(BF16), (((0,), (0,)), ((), ())), preferred_element_type=F32)


def _sigmoid(x):
    return 1.0 / (1.0 + jnp.exp(-x))


def _silu(x):
    return x * _sigmoid(x)


def _softplus(x):
    return jnp.maximum(x, 0.0) + jnp.log(1.0 + jnp.exp(-jnp.abs(x)))


def _split_dot(m_bf16, x):
    hi = x.astype(BF16)
    r1 = x - hi.astype(F32)
    mid = r1.astype(BF16)
    lo = (r1 - mid.astype(F32)).astype(BF16)
    return (jnp.dot(m_bf16, hi, preferred_element_type=F32)
            + jnp.dot(m_bf16, mid, preferred_element_type=F32)
            + jnp.dot(m_bf16, lo, preferred_element_type=F32))


def _spread(x, sel_bf16):
    hi = x.astype(BF16)
    r1 = x - hi.astype(F32)
    mid = r1.astype(BF16)
    lo = (r1 - mid.astype(F32)).astype(BF16)
    return (jnp.dot(hi, sel_bf16, preferred_element_type=F32)
            + jnp.dot(mid, sel_bf16, preferred_element_type=F32)
            + jnp.dot(lo, sel_bf16, preferred_element_type=F32))


def _head_sums(x, ones_bf16):
    hi = x.astype(BF16)
    lo = (x - hi.astype(F32)).astype(BF16)
    return (jnp.dot(hi, ones_bf16, preferred_element_type=F32)
            + jnp.dot(lo, ones_bf16, preferred_element_type=F32))


def _mo(x, m):
    return x if isinstance(x, int) else pl.multiple_of(x, m)


def _seg_masks(c, seg):
    sh = jnp.int32(int(math.log2(seg)))
    r = lax.broadcasted_iota(jnp.int32, (c, c), 0)
    q = lax.broadcasted_iota(jnp.int32, (c, c), 1)
    same = lax.shift_right_arithmetic(r, sh) == lax.shift_right_arithmetic(q, sh)
    incl = jnp.logical_and(same, r >= q)
    strict = jnp.logical_and(same, r > q)
    return incl, strict, same, r, q


def _mm_kernel(*refs, k_sizes):
    n_x = len(k_sizes)
    x_refs, w_ref, o_ref, wbf = refs[:n_x], refs[n_x], refs[n_x + 1], refs[n_x + 2]

    @pl.when(pl.program_id(1) == 0)
    def _():
        wbf[...] = w_ref[...].astype(BF16)

    acc = None
    off = 0
    for xr, ks in zip(x_refs, k_sizes):
        part = jnp.dot(xr[...].astype(BF16), wbf[off:off + ks, :], preferred_element_type=F32)
        acc = part if acc is None else acc + part
        off += ks
    o_ref[...] = acc.astype(o_ref.dtype)


def _w_spec(w, tn, layer):
    k = w.shape[-2]
    if w.ndim == 2:
        return pl.BlockSpec((k, tn), lambda j, i: (0, j))
    return pl.BlockSpec((None, k, tn), lambda j, i: (layer, 0, j))


def _matmul(xs, w, out_dtype, tm, tn, layer=None):
    m = xs[0].shape[0]
    k, n = w.shape[-2:]
    k_sizes = tuple(x.shape[1] for x in xs)
    assert sum(k_sizes) == k and m % tm == 0 and n % tn == 0
    in_specs = [pl.BlockSpec((tm, ks), lambda j, i: (i, 0)) for ks in k_sizes]
    in_specs.append(_w_spec(w, tn, layer))
    return pl.pallas_call(
        functools.partial(_mm_kernel, k_sizes=k_sizes),
        grid=(n // tn, m // tm),
        in_specs=in_specs,
        out_specs=pl.BlockSpec((tm, tn), lambda j, i: (i, j)),
        out_shape=jax.ShapeDtypeStruct((m, n), out_dtype),
        scratch_shapes=[pltpu.VMEM((k, tn), BF16)],
        compiler_params=_cparams(2),
    )(*xs, w)


def _ffn_gu_kernel(x_ref, wg_ref, wu_ref, o_ref, wg_bf, wu_bf):
    @pl.when(pl.program_id(1) == 0)
    def _():
        wg_bf[...] = wg_ref[...].astype(BF16)
        wu_bf[...] = wu_ref[...].astype(BF16)

    x = x_ref[...]
    gate = jnp.dot(x, wg_bf[...], preferred_element_type=F32)
    up = jnp.dot(x, wu_bf[...], preferred_element_type=F32)
    o_ref[...] = (_silu(gate) * up).astype(o_ref.dtype)


def _ffn_gate_up(x_bf, wg, wu, tm, tn, layer):
    m, k = x_bf.shape
    n = wg.shape[-1]
    return pl.pallas_call(
        _ffn_gu_kernel,
        grid=(n // tn, m // tm),
        in_specs=[pl.BlockSpec((tm, k), lambda j, i: (i, 0)), _w_spec(wg, tn, layer), _w_spec(wu, tn, layer)],
        out_specs=pl.BlockSpec((tm, tn), lambda j, i: (i, j)),
        out_shape=jax.ShapeDtypeStruct((m, n), BF16),
        scratch_shapes=[pltpu.VMEM((k, tn), BF16), pltpu.VMEM((k, tn), BF16)],
        compiler_params=_cparams(2),
    )(x_bf, wg, wu)


def _mm_ln_kernel(*refs, k_sizes, cast_w):
    n_x = len(k_sizes)
    x_refs = refs[:n_x]
    w_ref, r_ref, g_ref, b_ref, of_ref, ob_ref = refs[n_x:n_x + 6]
    if cast_w:
        wbf = refs[n_x + 6]

        @pl.when(pl.program_id(0) == 0)
        def _():
            wbf[...] = w_ref[...].astype(BF16)
    else:
        wbf = w_ref
    acc = None
    off = 0
    for xr, ks in zip(x_refs, k_sizes):
        part = jnp.dot(xr[...].astype(BF16), wbf[off:off + ks, :], preferred_element_type=F32)
        acc = part if acc is None else acc + part
        off += ks
    z = DN_ALPHA * r_ref[...] + acc
    zc = z - jnp.mean(z, axis=-1, keepdims=True)
    var = jnp.mean(zc * zc, axis=-1, keepdims=True)
    out = zc * lax.rsqrt(var + 1e-5) * g_ref[...] + b_ref[...]
    of_ref[...] = out
    ob_ref[...] = out.astype(BF16)


def _matmul_res_ln(xs, w, layer, resid, g, b, tm):
    m, n = resid.shape
    k = w.shape[-2]
    k_sizes = tuple(x.shape[1] for x in xs)
    cast_w = w.dtype != BF16
    assert sum(k_sizes) == k and m % tm == 0 and w.shape[-1] == n
    row = pl.BlockSpec((tm, n), lambda i: (i, 0))
    vec = pl.BlockSpec((1, n), lambda i: (0, 0))
    in_specs = [pl.BlockSpec((tm, ks), lambda i: (i, 0)) for ks in k_sizes]
    in_specs.append(pl.BlockSpec((None, k, n), lambda i: (layer, 0, 0), pipeline_mode=pl.Buffered(1)))
    in_specs += [row, vec, vec]
    return pl.pallas_call(
        functools.partial(_mm_ln_kernel, k_sizes=k_sizes, cast_w=cast_w),
        grid=(m // tm,),
        in_specs=in_specs,
        out_specs=[row, row],
        out_shape=[jax.ShapeDtypeStruct((m, n), F32), jax.ShapeDtypeStruct((m, n), BF16)],
        scratch_shapes=[pltpu.VMEM((k, n), BF16)] if cast_w else [],
        compiler_params=_cparams(1),
    )(*xs, w, resid, g.reshape(1, n), b.reshape(1, n))


def _xattn_kernel(q_ref, k_ref, v_ref, o_ref, *, nb, tq):
    for j in range(nb):
        rows = slice(j * tq, (j + 1) * tq)
        for h in range(XA_H):
            cols = slice(h * XA_HD, (h + 1) * XA_HD)
            s = _dot_nt(q_ref[rows, cols], k_ref[j, :, cols]) * (XA_HD ** -0.5)
            e = jnp.exp(s - jnp.max(s, axis=-1, keepdims=True))
            pr = e / jnp.sum(e, axis=-1, keepdims=True)
            o_ref[rows, cols] = _dot(pr, v_ref[j, :, cols]).astype(o_ref.dtype)


def _xattn_cache_body(q_ref, k_ref, v_ref, o_ref, nb, tq):
    nr = N_MEM * XA_H
    r = lax.broadcasted_iota(jnp.int32, (nr, XA_H * tq), 0)
    q = lax.broadcasted_iota(jnp.int32, (nr, XA_H * tq), 1)
    own = (r & (XA_H - 1)) == lax.shift_right_arithmetic(q, jnp.int32(int(math.log2(tq))))
    for j in range(nb):
        rows = slice(j * tq, (j + 1) * tq)
        kf = k_ref[j].reshape(nr, XA_HD)
        vf = v_ref[j].reshape(nr, XA_HD)
        qcat = jnp.concatenate([q_ref[rows, h * XA_HD:(h + 1) * XA_HD] for h in range(XA_H)], axis=0)
        s = jnp.where(own, _dot_nt(kf, qcat) * (XA_HD ** -0.5), -jnp.inf)
        e = jnp.exp(s - jnp.max(s, axis=0, keepdims=True))
        pr = e / jnp.sum(e, axis=0, keepdims=True)
        o = _dot_tn(pr, vf)
        for h in range(XA_H):
            o_ref[rows, h * XA_HD:(h + 1) * XA_HD] = o[h * tq:(h + 1) * tq, :].astype(o_ref.dtype)


def _xattn_cache_kernel(q_ref, k_ref, v_ref, o_ref, *, nb, tq):
    _xattn_cache_body(q_ref, k_ref, v_ref, o_ref, nb, tq)


def _ffn_xattn_kernel(x_ref, wg_ref, wu_ref, q_ref, k_ref, v_ref, o_ref, att_ref, wg_bf, wu_bf, *, nb, tq, n_att):
    _ffn_gu_kernel(x_ref, wg_ref, wu_ref, o_ref, wg_bf, wu_bf)
    step = pl.program_id(0) * pl.num_programs(1) + pl.program_id(1)

    @pl.when(step < n_att)
    def _():
        _xattn_cache_body(q_ref, k_ref, v_ref, att_ref, nb, tq)


def _ffn_gate_up_with_cache_attn(x_bf, wg, wu, tm, tn, layer, q, cache_k, cache_v, bsz, t, nb):
    m, k = x_bf.shape
    n = wg.shape[-1]
    n_inner = m // tm
    n_att = bsz // nb
    assert (n // tn) * n_inner >= n_att
    rows = nb * t

    def blk(j, i):
        return jnp.minimum(j * n_inner + i, n_att - 1)

    kv_spec = pl.BlockSpec((None, nb, N_MEM, XA_H, XA_HD), lambda j, i: (layer, blk(j, i), 0, 0, 0))
    q_spec = pl.BlockSpec((rows, D_MODEL), lambda j, i: (blk(j, i), 0))
    return pl.pallas_call(
        functools.partial(_ffn_xattn_kernel, nb=nb, tq=t, n_att=n_att),
        grid=(n // tn, n_inner),
        in_specs=[pl.BlockSpec((tm, k), lambda j, i: (i, 0)), _w_spec(wg, tn, layer), _w_spec(wu, tn, layer),
                  q_spec, kv_spec, kv_spec],
        out_specs=[pl.BlockSpec((tm, tn), lambda j, i: (i, j)), q_spec],
        out_shape=[jax.ShapeDtypeStruct((m, n), BF16), jax.ShapeDtypeStruct((bsz * t, D_MODEL), F32)],
        scratch_shapes=[pltpu.VMEM((k, tn), BF16), pltpu.VMEM((k, tn), BF16)],
        compiler_params=_cparams(2),
    )(x_bf, wg, wu, q, cache_k, cache_v)


def _cross_attn(q, mem_k, mem_v, bsz, t, nb, tq, out_dtype, layer=None):
    nt = t // tq
    rows = nb * tq
    if layer is None:
        body = _xattn_kernel
        kv_spec = pl.BlockSpec((nb, N_MEM, D_MODEL), lambda i, j: (i, 0, 0))
    else:
        body = _xattn_cache_kernel
        kv_spec = pl.BlockSpec((None, nb, N_MEM, XA_H, XA_HD), lambda i, j: (layer, i, 0, 0, 0))
    return pl.pallas_call(
        functools.partial(body, nb=nb, tq=tq),
        grid=(bsz // nb, nt),
        in_specs=[pl.BlockSpec((rows, D_MODEL), lambda i, j: (i * nt + j, 0)), kv_spec, kv_spec],
        out_specs=pl.BlockSpec((rows, D_MODEL), lambda i, j: (i * nt + j, 0)),
        out_shape=jax.ShapeDtypeStruct((bsz * t, D_MODEL), out_dtype),
        compiler_params=_cparams(2),
    )(q, mem_k, mem_v)


def _state_io(carry, h0_ref, hout_ref, hst):
    if carry:
        return (lambda s, h: hst[h]), (lambda s, h, val: hst.__setitem__(h, val))
    return (lambda s, h: h0_ref[s, h]), (lambda s, h, val: hout_ref.__setitem__((s, h), val))


def _for_segments(nseg, fn, unroll=True):
    if nseg == 1:
        fn(0)
    else:
        def body(s, c):
            fn(s)
            return c
        lax.fori_loop(0, nseg, body, 0, unroll=unroll)


def _ret_kernel(q_ref, k_ref, v_ref, g_ref, cos_ref, sin_ref, gn_ref, h0_ref, o_ref, hout_ref,
                hst, qs_sc, ks_sc, y_sc, *, rows, seg, carry):
    c = CHUNK
    nseg = c // seg
    if carry:
        @pl.when(pl.program_id(1) == 0)
        def _():
            hst[...] = h0_ref[0]
    get_h, set_h = _state_io(carry, h0_ref, hout_ref, hst)
    incl, _, _, r_i, c_i = _seg_masks(c, seg)
    dpos = (r_i - c_i).astype(F32)
    tau = (lax.broadcasted_iota(jnp.int32, (c, RET_HD), 0) & (seg - 1)).astype(F32)
    even = (lax.broadcasted_iota(jnp.int32, (c, GROUP), 1) & 1) == 0

    for ci in range(rows // c):
        rws = slice(ci * c, (ci + 1) * c)
        cosb = jnp.concatenate([cos_ref[rws, :]] * RET_H, axis=1)
        sinb = jnp.concatenate([sin_ref[rws, :]] * RET_H, axis=1)

        def rot(x):
            swapped = jnp.where(even, pltpu.roll(x, GROUP - 1, 1), pltpu.roll(x, 1, 1))
            return x * cosb + swapped * sinb

        qr = rot(q_ref[rws, :])
        kr = rot(k_ref[rws, :]) * (RET_HD ** -0.5)
        for h in range(RET_H):
            lgam = RET_LOG_GAMMA[h]
            cols = slice(h * RET_HD, (h + 1) * RET_HD)
            qh, kh = qr[:, cols], kr[:, cols]
            dm = jnp.where(incl, jnp.exp(dpos * lgam), 0.0)
            y_sc[...] = _dot(_dot_nt(qh, kh) * dm, v_ref[rws, cols])
            qs_sc[...] = qh * jnp.exp((tau + 1.0) * lgam)
            ks_sc[...] = kh * jnp.exp((seg - 1.0 - tau) * lgam)
            cd = math.exp(seg * lgam)

            def seg_step(s, h=h, cols=cols, cd=cd, ci=ci):
                sr = pl.ds(_mo(s * seg, seg), seg)
                vr = pl.ds(_mo(ci * c + s * seg, seg), seg)
                hs = get_h(s, h)
                y_sc[sr, :] += _dot(qs_sc[sr, :], hs)
                set_h(s, h, cd * hs + _dot_tn(ks_sc[sr, :], v_ref[vr, cols]))

            _for_segments(nseg, seg_step)
            y = y_sc[...]
            yn = y * lax.rsqrt(jnp.mean(y * y, axis=-1, keepdims=True) + 1e-5) * gn_ref[:, cols]
            o_ref[rws, cols] = (yn * _silu(g_ref[rws, cols])).astype(o_ref.dtype)

    if carry:
        @pl.when(pl.program_id(1) == pl.num_programs(1) - 1)
        def _():
            hout_ref[0] = hst[...]


def _mamba_kernel(z_ref, xbc_ref, dt_ref, cprev_ref, h0_ref, cw_ref, cb_ref, dtb_ref, alog_ref, dd_ref,
                  ng_ref, e64_ref, e128_ref, o_ref, cout_ref, hout_ref,
                  hst, xpad, act_sc, qs_sc, ks_sc, xdt_sc, xs_sc, yi_sc, yf_sc, etot_sc, *, rows, seg, carry):
    c = CHUNK
    nseg = c // seg
    get_h, set_h = _state_io(carry, h0_ref, hout_ref, hst)

    def conv_act(window, n):
        acc = cb_ref[...] + window[5:5 + n] * cw_ref[0:1, :]
        for i in range(1, MB_CONV):
            acc = acc + window[5 + i:5 + i + n] * cw_ref[i:i + 1, :]
        return _silu(acc)

    if carry:
        @pl.when(pl.program_id(1) == 0)
        def _():
            hst[...] = h0_ref[0]
            xpad[0:8, :] = jnp.zeros((8, MB_CONV_DIM), F32)
            xpad[5:8, :] = cprev_ref[0]

        xpad[8:8 + rows, :] = xbc_ref[...]
        for ci in range(rows // c):
            acc = cb_ref[...] + xpad[ci * c + 5:ci * c + 5 + c, :] * cw_ref[0:1, :]
            for i in range(1, MB_CONV):
                acc = acc + xpad[ci * c + 5 + i:ci * c + 5 + i + c, :] * cw_ref[i:i + 1, :]
            act_sc[ci * c:(ci + 1) * c, :] = _silu(acc)
        xpad[0:8, :] = xpad[rows:rows + 8, :]

        @pl.when(pl.program_id(1) == pl.num_programs(1) - 1)
        def _():
            cout_ref[0] = xpad[5:8, :]
    else:
        def conv_seq(s, carry_):
            sr = pl.ds(pl.multiple_of(s * seg, seg), seg)
            xpad[5:8, :] = cprev_ref[s]
            xs = xbc_ref[sr, :]
            window = jnp.concatenate([xpad[0:8, :], xs], axis=0)
            act_sc[sr, :] = conv_act(window, seg)
            cout_ref[s] = xs[seg - 3:seg]
            return carry_

        xpad[0:8, :] = jnp.zeros((8, MB_CONV_DIM), F32)
        lax.fori_loop(0, rows // seg, conv_seq, 0)

    incl, _, same, _, _ = _seg_masks(c, seg)
    lt_bf = jnp.where(incl, 1.0, 0.0).astype(BF16)
    same_bf = jnp.where(same, 1.0, 0.0).astype(BF16)
    a_neg = -jnp.exp(alog_ref[...])

    for ci in range(rows // c):
        rws = slice(ci * c, (ci + 1) * c)
        dtv = _softplus(dt_ref[rws, :] + dtb_ref[...])
        la = dtv * a_neg
        cum = _split_dot(lt_bf, la)
        tot = _split_dot(same_bf, la)
        cum_t = cum.T
        etot_sc[...] = jnp.exp(tot)
        cum_cols = _spread(cum, e128_ref[...])
        cum_full = _spread(cum, e64_ref[...])
        xh_all = act_sc[rws, 0:GROUP]
        xdt_all = xh_all * _spread(dtv, e64_ref[...])
        xdt_sc[...] = xdt_all
        xs_sc[...] = xdt_all * jnp.exp(_spread(tot, e64_ref[...]) - cum_full)
        for g in range(MB_G):
            cg = act_sc[rws, 768 + g * MB_N:768 + (g + 1) * MB_N]
            bg = act_sc[rws, 512 + g * MB_N:512 + (g + 1) * MB_N]
            gmat = _dot_nt(cg, bg)
            qs_sc[...] = cg
            ks_sc[...] = bg
            for hh in range(MB_H // MB_G):
                h = g * (MB_H // MB_G) + hh
                cols = slice(h * MB_HD, (h + 1) * MB_HD)
                lmat = jnp.exp(jnp.where(incl, cum_cols[:, h * c:(h + 1) * c] - cum_t[h:h + 1, :], -jnp.inf))
                yf_sc[:, cols] = _dot(gmat * lmat, xdt_sc[:, cols])

                def seg_step(s, h=h, cols=cols):
                    sr = pl.ds(_mo(s * seg, seg), seg)
                    first = pl.ds(_mo(s * seg, seg), 1)
                    hs = get_h(s, h)
                    yi_sc[sr, cols] = _dot_nt(qs_sc[sr, :], hs)
                    set_h(s, h, etot_sc[first, h:h + 1] * hs + _dot_tn(xs_sc[sr, cols], ks_sc[sr, :]))

                _for_segments(nseg, seg_step)
        y_all = yf_sc[...] + yi_sc[...] * jnp.exp(cum_full) + dd_ref[...] * xh_all
        yz = y_all * _silu(z_ref[rws, :])
        gw = GROUP // MB_G
        for g in range(MB_G):
            cols = slice(g * gw, (g + 1) * gw)
            part = yz[:, cols]
            nrm = part * lax.rsqrt(jnp.mean(part * part, axis=-1, keepdims=True) + 1e-5) * ng_ref[:, cols]
            o_ref[rws, cols] = nrm.astype(o_ref.dtype)

    if carry:
        @pl.when(pl.program_id(1) == pl.num_programs(1) - 1)
        def _():
            hout_ref[0] = hst[...]


def _gla_kernel(q_ref, k_ref, v_ref, gk_ref, g_ref, h0_ref, gkup_ref, gkb_ref, ng_ref, e_ref,
                o_ref, hout_ref,
                hst, qe_sc, ke_sc, etot_sc, p_sc, y_sc, *, rows, seg, carry):
    c = CHUNK
    nsub = c // seg
    if carry:
        @pl.when(pl.program_id(1) == 0)
        def _():
            for h in range(GLA_H):
                hst[h] = h0_ref[0, h].T
    incl, _, same, _, _ = _seg_masks(c, seg)
    lt_bf = jnp.where(incl, 1.0, 0.0).astype(BF16)
    same_bf = jnp.where(same, 1.0, 0.0).astype(BF16)
    row_i = lax.broadcasted_iota(jnp.int32, (seg, GLA_QK), 0)
    kcs = [slice(h * GLA_DK, (h + 1) * GLA_DK) for h in range(GLA_H)]
    vcs = [slice(h * GLA_DV, (h + 1) * GLA_DV) for h in range(GLA_H)]

    for ci in range(rows // c):
        rws = slice(ci * c, (ci + 1) * c)
        pre = _dot(gk_ref[rws, :], gkup_ref[...]) + gkb_ref[...]
        la = -_softplus(-pre) * (1.0 / GLA_TAU)
        cum = _split_dot(lt_bf, la)
        tot = _split_dot(same_bf, la)
        qv = q_ref[rws, :] * (GLA_DK ** -0.5)
        kv = k_ref[rws, :]
        qe_sc[...] = qv * jnp.exp(cum)
        ke_sc[...] = kv * jnp.exp(tot - cum)
        etot_sc[...] = jnp.exp(tot)

        safe = (jnp.max(-cum) < GLA_SAFE_EXP) if carry else False

        def factored():
            kinv = kv * jnp.exp(-cum)
            for h in range(GLA_H):
                sc = jnp.where(incl, _dot_nt(qe_sc[:, kcs[h]], kinv[:, kcs[h]]), 0.0)
                y_sc[:, vcs[h]] = _dot(sc, v_ref[rws, vcs[h]])

        def pairwise():
            for u in range(nsub):
                sr = slice(u * seg, (u + 1) * seg)
                cu, qu = cum[sr], qv[sr]
                for j in range(seg):
                    r0 = u * seg + j
                    pj = qu * jnp.exp(jnp.where(row_i >= j, cu - cum[r0:r0 + 1], -jnp.inf)) * kv[r0:r0 + 1]
                    p_sc[u, j * seg:(j + 1) * seg, :] = pj
            for u in range(nsub):
                rm = _dot(p_sc[u], e_ref[...])
                y = rm[0:seg, :] * v_ref[ci * c + u * seg:ci * c + u * seg + 1, :]
                for j in range(1, seg):
                    r0 = ci * c + u * seg + j
                    y = y + rm[j * seg:(j + 1) * seg, :] * v_ref[r0:r0 + 1, :]
                y_sc[u * seg:(u + 1) * seg, :] = y

        if carry:
            pl.when(safe)(factored)
            pl.when(jnp.logical_not(safe))(pairwise)
        else:
            pairwise()
        states = [hst[h] for h in range(GLA_H)] if carry else None
        for u in range(nsub):
            sr = slice(u * seg, (u + 1) * seg)
            vr = slice(ci * c + u * seg, ci * c + (u + 1) * seg)
            hts = [states[h] if carry else h0_ref[u, h].T for h in range(GLA_H)]
            for h in range(GLA_H):
                y_sc[sr, vcs[h]] += _dot_nt(qe_sc[sr, kcs[h]], hts[h])
            upds = [_dot_tn(v_ref[vr, vcs[h]], ke_sc[sr, kcs[h]]) for h in range(GLA_H)]
            for h in range(GLA_H):
                new = hts[h] * etot_sc[u * seg:u * seg + 1, kcs[h]] + upds[h]
                if carry:
                    states[h] = new
                else:
                    hout_ref[u, h] = new.T
        if carry:
            for h in range(GLA_H):
                hst[h] = states[h]

        for h in range(GLA_H):
            y = y_sc[:, vcs[h]]
            yn = y * lax.rsqrt(jnp.mean(y * y, axis=-1, keepdims=True) + 1e-5) * ng_ref[:, vcs[h]]
            o_ref[rws, vcs[h]] = (yn * _silu(g_ref[rws, vcs[h]])).astype(o_ref.dtype)

    if carry:
        @pl.when(pl.program_id(1) == pl.num_programs(1) - 1)
        def _():
            for h in range(GLA_H):
                hout_ref[0, h] = hst[h].T


def _rwkv_kernel(p_ref, sp_ref, h0_ref, mu_ref, w0_ref, wup_ref, a0_ref, aup_ref, gup_ref, kk_ref, ka_ref,
                 rk_ref, lng_ref, lnb_ref, ones_ref, o_ref, hout_ref,
                 hst, last_sc, prev_sc, w_sc, y_sc, bt_sc, kt_sc, v_sc, rt_sc, u_sc, rkb_sc, etot_sc, out_sc, kap_sc,
                 pw_sc, t_sc, ak_sc, pb_sc, pk_sc, *, rows, seg, sub, carry):
    c = CHUNK
    assert rows == c
    nsub = c // sub
    n_iter = int(math.log2(sub)) - 1
    npair = RW_H // 2
    pw_ = 2 * RW_HD
    pr_i = lax.broadcasted_iota(jnp.int32, (pw_, pw_), 0)
    pc_i = lax.broadcasted_iota(jnp.int32, (pw_, pw_), 1)
    bd_mask = (pr_i >= RW_HD) == (pc_i >= RW_HD)
    low_rows = lax.broadcasted_iota(jnp.int32, (pw_, RW_HD), 0) < RW_HD

    def to_bd(stack):
        return jnp.where(bd_mask, jnp.concatenate([stack, stack], axis=1), 0.0)

    def from_bd(bd):
        return jnp.where(low_rows, bd[:, 0:RW_HD], bd[:, RW_HD:pw_])

    if carry:
        @pl.when(pl.program_id(1) == 0)
        def _():
            for pi in range(npair):
                hst[pi] = to_bd(h0_ref[0, pi])
            last_sc[...] = jnp.broadcast_to(sp_ref[0], (8, RW_IN))

        prev_row = jnp.broadcast_to(last_sc[0:1, :], (c, RW_IN))
    else:
        for s in range(rows // seg):
            prev_sc[s * seg:(s + 1) * seg, :] = jnp.broadcast_to(sp_ref[s], (seg, RW_IN))
        prev_row = prev_sc[...]
    incl, strict, same, r_i, c_i = _seg_masks(c, sub)
    lt_bf = jnp.where(incl, 1.0, 0.0).astype(BF16)
    same_bf = jnp.where(same, 1.0, 0.0).astype(BF16)
    eye = jnp.where(r_i == c_i, 1.0, 0.0)
    row_w = lax.broadcasted_iota(jnp.int32, (c, RW_IN), 0)
    first_row = (row_w & ((c if carry else seg) - 1)) == 0

    p = p_ref[...]
    prev = jnp.where(first_row, prev_row, pltpu.roll(p, 1, 0))
    xs = p + (prev - p) * mu_ref[...]
    r = xs[:, 0:GROUP]
    k = xs[:, GROUP:2 * GROUP]
    v = xs[:, 2 * GROUP:3 * GROUP]
    wa = xs[:, 3 * GROUP:3 * GROUP + 128]
    gd = xs[:, 3 * GROUP + 128:RW_IN]
    w = -_softplus(-(w0_ref[...] + _dot(jnp.tanh(wa), wup_ref[...]))) - 0.5
    ld = -jnp.exp(w)
    a = _sigmoid(a0_ref[...] + _dot(wa, aup_ref[...]))
    gate = _dot(_sigmoid(gd), gup_ref[...])
    kkr = k * kk_ref[...]
    k2 = k * (1.0 + (a - 1.0) * ka_ref[...])
    cum = _split_dot(lt_bf, ld)
    tot = _split_dot(same_bf, ld)
    ecum = jnp.exp(cum)
    einv = jnp.exp(-cum)
    eprev = jnp.exp(cum - ld)
    rkb_sc[...] = r * k2 * rk_ref[...]
    etot_sc[...] = jnp.exp(tot)
    v_sc[...] = v
    rtil = r * ecum
    rt_sc[...] = rtil
    kt_sc[...] = k2 * einv
    rn = lax.rsqrt(jnp.maximum(_head_sums(kkr * kkr, ones_ref[...]), 1e-24))
    braw = kkr * a * einv * rn
    kraw = kkr * eprev * rn

    heads = [slice(h * RW_HD, (h + 1) * RW_HD) for h in range(RW_H)]
    for h, cols in enumerate(heads):
        kap = kraw[:, cols]
        bt = braw[:, cols]
        kap_sc[:, cols] = kap
        bt_sc[:, cols] = bt
        m1 = _dot_nt(jnp.concatenate([kap, rtil[:, cols]], axis=0), jnp.concatenate([bt, kt_sc[:, cols]], axis=0))
        x = jnp.where(strict, -m1[0:c, 0:c], 0.0)
        pw_sc[0, h] = x
        t_sc[h] = eye + x
        ak_sc[h] = jnp.where(strict, m1[0:c, c:2 * c], 0.0)
        pb_sc[h] = jnp.where(incl, m1[c:2 * c, 0:c], 0.0)
        pk_sc[h] = jnp.where(incl, m1[c:2 * c, c:2 * c], 0.0)
    for h, cols in enumerate(heads):
        y_sc[:, cols] = _dot(ak_sc[h], v_sc[:, cols])
    for it in range(n_iter):
        src, dst = it % 2, (it + 1) % 2
        for h in range(RW_H):
            pw = pw_sc[src, h]
            pw_sc[dst, h] = _dot(pw, pw)
        for h in range(RW_H):
            tm = t_sc[h]
            t_sc[h] = tm + _dot(tm, pw_sc[dst, h])
    for h, cols in enumerate(heads):
        wy = _dot(t_sc[h], jnp.concatenate([kap_sc[:, cols], y_sc[:, cols]], axis=1))
        w_sc[:, cols] = wy[:, 0:RW_HD]
        y_sc[:, cols] = wy[:, RW_HD:2 * RW_HD]

    states = [hst[pi] for pi in range(npair)] if carry else None
    units = [(s, pi) for s in range(nsub) for pi in range(npair)]
    group = npair if carry else 2 * npair
    for g0 in range(0, len(units), group):
        grp_units = units[g0:g0 + group]
        sts, wrs, uus, upds = [], [], [], []
        for s, pi in grp_units:
            sts.append(states[pi] if carry else to_bd(h0_ref[s, pi]))
        for (s, pi), st in zip(grp_units, sts):
            sr, pc = slice(s * sub, (s + 1) * sub), slice(pi * pw_, (pi + 1) * pw_)
            wrs.append(_dot_nt(jnp.concatenate([w_sc[sr, pc], rt_sc[sr, pc]], axis=0), st))
        for (s, pi), wr in zip(grp_units, wrs):
            sr, pc = slice(s * sub, (s + 1) * sub), slice(pi * pw_, (pi + 1) * pw_)
            uu = -wr[0:sub, :] - y_sc[sr, pc]
            u_sc[sr, pc] = uu
            out_sc[sr, pc] = wr[sub:2 * sub, :]
            uus.append(uu)
        for (s, pi), uu in zip(grp_units, uus):
            sr, pc = slice(s * sub, (s + 1) * sub), slice(pi * pw_, (pi + 1) * pw_)
            upds.append(_dot_tn(jnp.concatenate([uu, v_sc[sr, pc]], axis=0),
                                jnp.concatenate([bt_sc[sr, pc], kt_sc[sr, pc]], axis=0)))
        for (s, pi), st, upd in zip(grp_units, sts, upds):
            pc = slice(pi * pw_, (pi + 1) * pw_)
            new = (st + jnp.where(bd_mask, upd, 0.0)) * etot_sc[s * sub:s * sub + 1, pc]
            if carry:
                states[pi] = new
            else:
                hout_ref[s, pi] = from_bd(new)
    if carry:
        for pi in range(npair):
            hst[pi] = states[pi]

    for h, cols in enumerate(heads):
        out_sc[:, cols] += _dot(jnp.concatenate([pb_sc[h], pk_sc[h]], axis=1),
                                jnp.concatenate([u_sc[:, cols], v_sc[:, cols]], axis=0))
    ones = ones_ref[...]
    o_all = out_sc[...]
    oc = o_all - _head_sums(o_all, ones) * (1.0 / RW_HD)
    var = _head_sums(oc * oc, ones) * (1.0 / RW_HD)
    on = oc * lax.rsqrt(var + RW_LN_EPS) * lng_ref[...] + lnb_ref[...]
    out_sc[...] = on + _head_sums(rkb_sc[...], ones) * v_sc[...]
    o_ref[...] = (out_sc[...] * gate).astype(o_ref.dtype)

    if carry:
        last_sc[...] = jnp.broadcast_to(p_ref[rows - 1:rows, :], (8, RW_IN))

        @pl.when(pl.program_id(1) == pl.num_programs(1) - 1)
        def _():
            for pi in range(npair):
                hout_ref[0, pi] = from_bd(hst[pi])


class _Group:
    def __init__(self, bsz, t, rows):
        self.bsz, self.t, self.rows = bsz, t, rows
        self.carry = t >= CHUNK
        self.nb = 1 if self.carry else rows // t
        self.nt = t // rows if self.carry else 1
        self.grid = (bsz // self.nb, self.nt)
        self.seg = CHUNK if self.carry else t

    def rows_spec(self, width, col_block):
        nt = self.nt
        return pl.BlockSpec((self.rows, width), lambda i, j: (i * nt + j, col_block))

    def state_spec(self, shape, layer=None):
        zeros = (0,) * len(shape)
        if layer is None:
            return pl.BlockSpec((self.nb,) + tuple(shape), lambda i, j: (i,) + zeros)
        return pl.BlockSpec((None, self.nb) + tuple(shape), lambda i, j: (layer, i) + zeros)

    def out_rows(self, width, dtype=BF16):
        nt = self.nt
        return (pl.BlockSpec((self.rows, width), lambda i, j: (i * nt + j, 0)),
                jax.ShapeDtypeStruct((self.bsz * self.t, width), dtype))


def _vec_spec(shape):
    zeros = (0,) * len(shape)
    return pl.BlockSpec(tuple(shape), lambda i, j: zeros)


def _retention(grp, p, cos_t, sin_t, h0, sl, gn):
    c = CHUNK
    shape = (RET_H, RET_HD, RET_HD)
    in_specs = [grp.rows_spec(GROUP, COL_RET // GROUP + n) for n in range(4)]
    tab = pl.BlockSpec((grp.rows, RET_HD), (lambda i, j: (j, 0)) if grp.carry else (lambda i, j: (0, 0)))
    in_specs += [tab, tab, _vec_spec((1, GROUP)), grp.state_spec(shape, sl)]
    o_spec, o_shape = grp.out_rows(GROUP)
    return pl.pallas_call(
        functools.partial(_ret_kernel, rows=grp.rows, seg=grp.seg, carry=grp.carry),
        grid=grp.grid,
        in_specs=in_specs,
        out_specs=[o_spec, grp.state_spec(shape)],
        out_shape=[o_shape, jax.ShapeDtypeStruct(h0.shape[1:], F32)],
        scratch_shapes=[pltpu.VMEM(shape, F32), pltpu.VMEM((c, RET_HD), F32),
                        pltpu.VMEM((c, RET_HD), F32), pltpu.VMEM((c, RET_HD), F32)],
        compiler_params=_cparams(2),
    )(p, p, p, p, cos_t, sin_t, gn.reshape(1, GROUP), h0)


def _mamba(grp, p, cprev, h0, sl, lp):
    c = CHUNK
    cshape, hshape = (MB_CONV - 1, MB_CONV_DIM), (MB_H, MB_HD, MB_N)
    h0 = jnp.swapaxes(h0, -1, -2)
    pad8 = lambda a: jnp.pad(a.reshape(1, MB_H), ((0, 0), (0, 128 - MB_H)))
    head = jnp.arange(128)[:, None]
    e64 = (head == jnp.arange(GROUP)[None, :] // MB_HD).astype(BF16)
    e128 = (head == jnp.arange(MB_H * CHUNK)[None, :] // CHUNK).astype(BF16)
    in_specs = [grp.rows_spec(GROUP, COL_MB_Z // GROUP), grp.rows_spec(MB_CONV_DIM, COL_MB_XBC // MB_CONV_DIM),
                grp.rows_spec(128, COL_MB_DT // 128),
                grp.state_spec(cshape, sl), grp.state_spec(hshape, sl),
                _vec_spec((MB_CONV, MB_CONV_DIM)), _vec_spec((1, MB_CONV_DIM)), _vec_spec((1, 128)),
                _vec_spec((1, 128)), _vec_spec((1, GROUP)), _vec_spec((1, GROUP)),
                _vec_spec((128, GROUP)), _vec_spec((128, MB_H * CHUNK))]
    o_spec, o_shape = grp.out_rows(GROUP)
    return pl.pallas_call(
        functools.partial(_mamba_kernel, rows=grp.rows, seg=grp.seg, carry=grp.carry),
        grid=grp.grid,
        in_specs=in_specs,
        out_specs=[o_spec, grp.state_spec(cshape), grp.state_spec(hshape)],
        out_shape=[o_shape, jax.ShapeDtypeStruct(cprev.shape[1:], F32), jax.ShapeDtypeStruct(h0.shape[1:], F32)],
        scratch_shapes=[pltpu.VMEM(hshape, F32),
                        pltpu.VMEM(((grp.rows if grp.carry else 0) + 8, MB_CONV_DIM), F32),
                        pltpu.VMEM((grp.rows, MB_CONV_DIM), F32),
                        pltpu.VMEM((c, MB_N), F32), pltpu.VMEM((c, MB_N), F32), pltpu.VMEM((c, GROUP), F32),
                        pltpu.VMEM((c, GROUP), F32), pltpu.VMEM((c, GROUP), F32), pltpu.VMEM((c, GROUP), F32),
                        pltpu.VMEM((c, 128), F32)],
        compiler_params=_cparams(2),
    )(p, p, p, cprev, h0, lp['mb_conv_w'], lp['mb_conv_b'].reshape(1, MB_CONV_DIM), pad8(lp['mb_dt_bias']),
      pad8(lp['mb_a_log']), jnp.repeat(lp['mb_d'], MB_HD).reshape(1, GROUP), lp['mb_norm_g'].reshape(1, GROUP),
      e64, e128)


def _gla(grp, p, h0, sl, lp, expand):
    c = CHUNK
    shape = (GLA_H, GLA_DK, GLA_DV)
    sub = 64 if grp.carry else grp.seg
    gk_up = jnp.pad(lp['gla_gk_up'], ((0, 128 - GLA_LORA), (0, 0)))
    in_specs = [grp.rows_spec(GLA_QK, COL_GLA_Q // GLA_QK), grp.rows_spec(GLA_QK, COL_GLA_K // GLA_QK),
                grp.rows_spec(GROUP, COL_GLA_V // GROUP), grp.rows_spec(128, COL_GLA_GK // 128),
                grp.rows_spec(GROUP, COL_GLA_G // GROUP), grp.state_spec(shape, sl),
                _vec_spec((128, GLA_QK)), _vec_spec((1, GLA_QK)), _vec_spec((1, GROUP)), _vec_spec((GLA_QK, GROUP))]
    o_spec, o_shape = grp.out_rows(GROUP)
    return pl.pallas_call(
        functools.partial(_gla_kernel, rows=grp.rows, seg=sub, carry=grp.carry),
        grid=grp.grid,
        in_specs=in_specs,
        out_specs=[o_spec, grp.state_spec(shape)],
        out_shape=[o_shape, jax.ShapeDtypeStruct(h0.shape[1:], F32)],
        scratch_shapes=[pltpu.VMEM((GLA_H, GLA_DV, GLA_DK), F32)]
        + [pltpu.VMEM((c, GLA_QK), F32) for _ in range(3)]
        + [pltpu.VMEM((c // sub, sub * sub, GLA_QK), F32), pltpu.VMEM((c, GROUP), F32)],
        compiler_params=_cparams(2),
    )(p, p, p, p, p, h0, gk_up, lp['gla_gk_b'].reshape(1, GLA_QK), lp['gla_norm_g'].reshape(1, GROUP), expand)


def _rwkv(grp, p, shift_prev, h0, sl, lp):
    c = CHUNK
    head_ones = (jnp.arange(GROUP)[:, None] // RW_HD == jnp.arange(GROUP)[None, :] // RW_HD).astype(BF16)
    shape = (RW_H // 2, 2 * RW_HD, RW_HD)
    h0 = h0.reshape(h0.shape[:2] + shape)
    sub = 64 if grp.carry else grp.seg
    row = lambda a: a.reshape(1, -1)
    w_up = jnp.pad(lp['rw_w_up'], ((0, 64), (0, 0)))
    a_up = jnp.pad(lp['rw_a_up'], ((64, 0), (0, 0)))
    in_specs = [grp.rows_spec(RW_IN, 0), grp.state_spec((1, RW_IN), sl), grp.state_spec(shape, sl),
                _vec_spec((1, RW_IN)), _vec_spec((1, GROUP)), _vec_spec((128, GROUP)), _vec_spec((1, GROUP)),
                _vec_spec((128, GROUP)), _vec_spec((128, GROUP))] + [_vec_spec((1, GROUP))] * 5 + [_vec_spec((GROUP, GROUP))]
    o_spec, o_shape = grp.out_rows(GROUP)
    wide = lambda: pltpu.VMEM((c, GROUP), F32)
    return pl.pallas_call(
        functools.partial(_rwkv_kernel, rows=grp.rows, seg=grp.seg, sub=sub, carry=grp.carry),
        grid=grp.grid,
        in_specs=in_specs,
        out_specs=[o_spec, grp.state_spec(shape)],
        out_shape=[o_shape, jax.ShapeDtypeStruct(h0.shape[1:], F32)],
        scratch_shapes=[pltpu.VMEM((RW_H // 2, 2 * RW_HD, 2 * RW_HD), F32), pltpu.VMEM((8, RW_IN), F32),
                        pltpu.VMEM((c, RW_IN), F32)] + [wide() for _ in range(11)]
        + [pltpu.VMEM((2, RW_H, c, c), F32)] + [pltpu.VMEM((RW_H, c, c), F32) for _ in range(4)],
        compiler_params=_cparams(2),
    )(p, shift_prev.reshape(shift_prev.shape[0], -1, 1, RW_IN), h0, row(lp['rw_mu']), row(lp['rw_w0']), w_up,
      row(lp['rw_a0']), a_up, lp['rw_g_up'], row(lp['rw_k_k']), row(lp['rw_k_a']), row(lp['rw_r_k']),
      row(lp['rw_ln_g']), row(lp['rw_ln_b']), head_ones)


def _rope_tables(pos0, t):
    half = RET_HD // 2
    inv = 1.0 / (ROPE_BASE ** jnp.linspace(0.0, 1.0, half, dtype=F32))
    pos = pos0 + jnp.arange(t, dtype=F32)
    ang = pos[:, None] * inv[None, :]
    cos, sin = jnp.cos(ang), jnp.sin(ang)
    cos_t = jnp.stack([cos, cos], axis=-1).reshape(t, RET_HD)
    sin_t = jnp.stack([-sin, sin], axis=-1).reshape(t, RET_HD)
    return cos_t, sin_t


def _pad_w_in(w):
    z = lambda n: jnp.zeros(w.shape[:-1] + (n,), w.dtype)
    return jnp.concatenate([
        w[..., 0:1792], w[..., 3328:3336], z(120), w[..., 4360:4376], z(112), w[..., 2304:3328],
        w[..., 1792:2304], w[..., 3848:4360], w[..., 4376:4888], w[..., 3336:3592], w[..., 3592:3848],
        w[..., 4888:6936]], axis=-1)


def _mix_block(grp, grp_rw, x, x_bf, pos_tabs, states, sl, lp, big, wl, expand, tm):
    rw_shift, rw_state, mb_conv, mb_state, gla_state, ret_state = states
    bsz, t = grp.bsz, grp.t
    p = _matmul([x_bf], big['w_in'], F32, tm, 1024, wl)
    o_rw, rw_new = _rwkv(grp_rw, p, rw_shift, rw_state, sl, lp)
    rw_new = rw_new.reshape(bsz, RW_H, RW_HD, RW_HD)
    o_mb, conv_new, mb_new = _mamba(grp, p, mb_conv, mb_state, sl, lp)
    mb_new = jnp.swapaxes(mb_new, -1, -2)
    o_gl, gla_new = _gla(grp, p, gla_state, sl, lp, expand)
    o_rt, ret_new = _retention(grp, p, pos_tabs[0], pos_tabs[1], ret_state, sl, lp['ret_norm_g'])
    shift_new = p.reshape(bsz, t, N_PAD)[:, t - 1, 0:RW_IN]
    x, x_bf = _matmul_res_ln([o_rw, o_mb, o_gl, o_rt], big['w_out'], wl, x, lp['ln1_g'], lp['ln1_b'],
                             512 if grp.carry else 256)
    q = _matmul([x_bf], big['xa_wq'], BF16 if grp.carry else F32, tm, 1024, wl)
    return x, q, (shift_new, rw_new, conv_new, mb_new, gla_new, ret_new)


def _layer_pair(gp, gp_rw, gs, xp, xp_bf, xs, xs_bf, tabs_p, tabs_s, zeros_p, st_s_in, mk, mv, cache_k, cache_v,
                lp, big, wl, expand):
    xp, qp, st_p = _mix_block(gp, gp_rw, xp, xp_bf, tabs_p, zeros_p, 0, lp, big, wl, expand, 1024)
    att_p = _cross_attn(qp, mk, mv, gp.bsz, gp.t, 1, 1024, BF16)
    xp, xp_bf = _matmul_res_ln([att_p], big['xa_wo'], wl, xp, lp['ln2_g'], lp['ln2_b'], 512)
    xs, qs, st_s = _mix_block(gs, gs, xs, xs_bf, tabs_s, st_s_in, wl, lp, big, wl, expand, 1024)
    hid_p, att_s = _ffn_gate_up_with_cache_attn(xp_bf, big['ffn_w_gate'], big['ffn_w_up'], 1024, 512, wl,
                                                qs, cache_k, cache_v, gs.bsz, gs.t, 2)
    xp, xp_bf = _matmul_res_ln([hid_p], big['ffn_w_down'], wl, xp, lp['ln3_g'], lp['ln3_b'], 256)
    xs, xs_bf = _matmul_res_ln([att_s], big['xa_wo'], wl, xs, lp['ln2_g'], lp['ln2_b'], 256)
    hid_s = _ffn_gate_up(xs_bf, big['ffn_w_gate'], big['ffn_w_up'], 1024, 512, wl)
    xs, xs_bf = _matmul_res_ln([hid_s], big['ffn_w_down'], wl, xs, lp['ln3_g'], lp['ln3_b'], 256)
    return xp, xp_bf, st_p, xs, xs_bf, st_s


def kernel(x_prompt, x_sample, state_rwkv_shift, state_rwkv_wkv, state_mamba_conv, state_mamba_ssm, state_gla,
           state_ret, cache_mem_k, cache_mem_v, mem_prompt, w_in, w_out, ln1_g, ln1_b, rw_mu, rw_w0, rw_w_up,
           rw_a0, rw_a_up, rw_g_up, rw_k_k, rw_k_a, rw_r_k, rw_ln_g, rw_ln_b, mb_conv_w, mb_conv_b, mb_dt_bias,
           mb_a_log, mb_d, mb_norm_g, gla_gk_up, gla_gk_b, gla_norm_g, ret_norm_g, ln2_g, ln2_b, xa_wq, xa_wk,
           xa_wv, xa_wo, ln3_g, ln3_b, ffn_w_gate, ffn_w_up, ffn_w_down):
    small = dict(
        ln1_g=ln1_g, ln1_b=ln1_b, rw_mu=rw_mu, rw_w0=rw_w0, rw_w_up=rw_w_up, rw_a0=rw_a0,
        rw_a_up=rw_a_up, rw_g_up=rw_g_up, rw_k_k=rw_k_k, rw_k_a=rw_k_a, rw_r_k=rw_r_k, rw_ln_g=rw_ln_g,
        rw_ln_b=rw_ln_b, mb_conv_w=mb_conv_w, mb_conv_b=mb_conv_b, mb_dt_bias=mb_dt_bias, mb_a_log=mb_a_log,
        mb_d=mb_d, mb_norm_g=mb_norm_g, gla_gk_up=gla_gk_up, gla_gk_b=gla_gk_b, gla_norm_g=gla_norm_g,
        ret_norm_g=ret_norm_g, ln2_g=ln2_g, ln2_b=ln2_b, ln3_g=ln3_g, ln3_b=ln3_b)
    big = dict(w_in=_pad_w_in(w_in), w_out=w_out, xa_wq=xa_wq, xa_wo=xa_wo, ffn_w_gate=ffn_w_gate,
               ffn_w_up=ffn_w_up, ffn_w_down=ffn_w_down.astype(BF16))
    bp, tp, _ = x_prompt.shape
    bs, ts, _ = x_sample.shape
    gp = _Group(bp, tp, 512)
    gp_rw = _Group(bp, tp, CHUNK)
    gs = _Group(bs, ts, CHUNK)
    expand = (jnp.arange(GLA_QK)[:, None] // GLA_DK == jnp.arange(GROUP)[None, :] // GLA_DV).astype(BF16)
    tabs_p = _rope_tables(0.0, tp)
    tabs_s = tuple(jnp.tile(tb, (gs.nb, 1)) for tb in _rope_tables(float(PAST_LEN), ts))
    zeros_p = (jnp.zeros((1, bp, RW_IN), F32), jnp.zeros((1, bp, RW_H, RW_HD, RW_HD), F32),
               jnp.zeros((1, bp, MB_CONV - 1, MB_CONV_DIM), F32), jnp.zeros((1, bp, MB_H, MB_N, MB_HD), F32),
               jnp.zeros((1, bp, GLA_H, GLA_DK, GLA_DV), F32), jnp.zeros((1, bp, RET_H, RET_HD, RET_HD), F32))
    st_s_in = (state_rwkv_shift, state_rwkv_wkv, state_mamba_conv, state_mamba_ssm, state_gla, state_ret)

    yp = x_prompt.reshape(bp * tp, D_MODEL)
    ys = x_sample.reshape(bs * ts, D_MODEL)
    yp_bf, ys_bf = yp.astype(BF16), ys.astype(BF16)
    mem_bf = mem_prompt.reshape(bp * N_MEM, D_MODEL).astype(BF16)
    outs_p = [[] for _ in range(8)]
    outs_s = [[] for _ in range(6)]
    for i in range(DEPTH):
        lp = {name: val[i] for name, val in small.items()}
        mk = _matmul([mem_bf], xa_wk, F32, 1024, 512, i)
        mv = _matmul([mem_bf], xa_wv, F32, 1024, 512, i)
        yp, yp_bf, st_p, ys, ys_bf, st_s = _layer_pair(
            gp, gp_rw, gs, yp, yp_bf, ys, ys_bf, tabs_p, tabs_s, zeros_p, st_s_in, mk.reshape(bp, N_MEM, D_MODEL),
            mv.reshape(bp, N_MEM, D_MODEL), cache_mem_k, cache_mem_v, lp, big, i, expand)
        for lst, val in zip(outs_p, st_p + (mk.reshape(bp, N_MEM, XA_H, XA_HD), mv.reshape(bp, N_MEM, XA_H, XA_HD))):
            lst.append(val)
        for lst, val in zip(outs_s, st_s):
            lst.append(val)
    return (yp.reshape(bp, tp, D_MODEL), ys.reshape(bs, ts, D_MODEL),
            *[jnp.stack(v) for v in outs_p], *[jnp.stack(v) for v in outs_s])
```

```python
import functools
import math

import jax
import jax.numpy as jnp
from jax import lax
from jax.experimental import pallas as pl
from jax.experimental.pallas import tpu as pltpu

F32 = jnp.float32
BF16 = jnp.bfloat16

D_MODEL = 2048
DEPTH = 2
PAST_LEN = 16384
GROUP = 512
RW_H, RW_HD = 8, 64
RW_IN = 1792
RW_LN_EPS = 64e-5
MB_H, MB_HD, MB_N, MB_G = 8, 64, 128, 2
MB_CONV = 4
MB_CONV_DIM = 1024
GLA_H, GLA_DK, GLA_DV = 4, 64, 128
GLA_QK = 256
GLA_LORA = 16
GLA_TAU = 16.0
GLA_SAFE_EXP = 60.0
RET_H, RET_HD = 4, 128
ROPE_BASE = 10000.0
N_MEM = 256
XA_H, XA_HD = 4, 512
D_FF = 5632
DN_ALPHA = (2 * DEPTH) ** 0.25
RET_LOG_GAMMA = tuple(math.log1p(-(2.0 ** (-5.0 - h))) for h in range(RET_H))

N_PAD = 7168
COL_MB_DT = 1792
COL_GLA_GK = 1920
COL_MB_XBC = 2048
COL_MB_Z = 3072
COL_GLA_V = 3584
COL_GLA_G = 4096
COL_GLA_Q = 4608
COL_GLA_K = 4864
COL_RET = 5120

CHUNK = 128
VMEM_LIMIT = 56 * 1024 * 1024


def _cparams(n_axes, vmem=VMEM_LIMIT):
    return pltpu.CompilerParams(dimension_semantics=("arbitrary",) * n_axes, vmem_limit_bytes=vmem)


def _dot(a, b):
    return jnp.dot(a.astype(BF16), b.astype(BF16), preferred_element_type=F32)


def _dot_nt(a, b):
    return lax.dot_general(a.astype(BF16), b.astype(BF16), (((1,), (1,)), ((), ())), preferred_element_type=F32)


def _dot_tn(a, b):
    return lax.dot_general(a.astype(BF16), b.astype(BF16), (((0,), (0,)), ((), ())), preferred_element_type=F32)


def _sigmoid(x):
    return 1.0 / (1.0 + jnp.exp(-x))


def _silu(x):
    return x * _sigmoid(x)


def _softplus(x):
    return jnp.maximum(x, 0.0) + jnp.log(1.0 + jnp.exp(-jnp.abs(x)))


def _split_dot(m_bf16, x):
    hi = x.astype(BF16)
    r1 = x - hi.astype(F32)
    mid = r1.astype(BF16)
    lo = (r1 - mid.astype(F32)).astype(BF16)
    return (jnp.dot(m_bf16, hi, preferred_element_type=F32)
            + jnp.dot(m_bf16, mid, preferred_element_type=F32)
            + jnp.dot(m_bf16, lo, preferred_element_type=F32))


def _spread(x, sel_bf16):
    hi = x.astype(BF16)
    r1 = x - hi.astype(F32)
    mid = r1.astype(BF16)
    lo = (r1 - mid.astype(F32)).astype(BF16)
    return (jnp.dot(hi, sel_bf16, preferred_element_type=F32)
            + jnp.dot(mid, sel_bf16, preferred_element_type=F32)
            + jnp.dot(lo, sel_bf16, preferred_element_type=F32))


def _head_sums(x, ones_bf16):
    hi = x.astype(BF16)
    lo = (x - hi.astype(F32)).astype(BF16)
    return (jnp.dot(hi, ones_bf16, preferred_element_type=F32)
            + jnp.dot(lo, ones_bf16, preferred_element_type=F32))


def _mo(x, m):
    return x if isinstance(x, int) else pl.multiple_of(x, m)


def _seg_masks(c, seg):
    sh = jnp.int32(int(math.log2(seg)))
    r = lax.broadcasted_iota(jnp.int32, (c, c), 0)
    q = lax.broadcasted_iota(jnp.int32, (c, c), 1)
    same = lax.shift_right_arithmetic(r, sh) == lax.shift_right_arithmetic(q, sh)
    incl = jnp.logical_and(same, r >= q)
    strict = jnp.logical_and(same, r > q)
    return incl, strict, same, r, q


def _mm_kernel(*refs, k_sizes):
    n_x = len(k_sizes)
    x_refs, w_ref, o_ref, wbf = refs[:n_x], refs[n_x], refs[n_x + 1], refs[n_x + 2]

    @pl.when(pl.program_id(1) == 0)
    def _():
        wbf[...] = w_ref[...].astype(BF16)

    acc = None
    off = 0
    for xr, ks in zip(x_refs, k_sizes):
        part = jnp.dot(xr[...].astype(BF16), wbf[off:off + ks, :], preferred_element_type=F32)
        acc = part if acc is None else acc + part
        off += ks
    o_ref[...] = acc.astype(o_ref.dtype)


def _w_spec(w, tn, layer):
    k = w.shape[-2]
    if w.ndim == 2:
        return pl.BlockSpec((k, tn), lambda j, i: (0, j))
    return pl.BlockSpec((None, k, tn), lambda j, i: (layer, 0, j))


def _matmul(xs, w, out_dtype, tm, tn, layer=None):
    m = xs[0].shape[0]
    k, n = w.shape[-2:]
    k_sizes = tuple(x.shape[1] for x in xs)
    assert sum(k_sizes) == k and m % tm == 0 and n % tn == 0
    in_specs = [pl.BlockSpec((tm, ks), lambda j, i: (i, 0)) for ks in k_sizes]
    in_specs.append(_w_spec(w, tn, layer))
    return pl.pallas_call(
        functools.partial(_mm_kernel, k_sizes=k_sizes),
        grid=(n // tn, m // tm),
        in_specs=in_specs,
        out_specs=pl.BlockSpec((tm, tn), lambda j, i: (i, j)),
        out_shape=jax.ShapeDtypeStruct((m, n), out_dtype),
        scratch_shapes=[pltpu.VMEM((k, tn), BF16)],
        compiler_params=_cparams(2),
    )(*xs, w)


def _ffn_gu_kernel(x_ref, wg_ref, wu_ref, o_ref, wg_bf, wu_bf):
    @pl.when(pl.program_id(1) == 0)
    def _():
        wg_bf[...] = wg_ref[...].astype(BF16)
        wu_bf[...] = wu_ref[...].astype(BF16)

    x = x_ref[...]
    gate = jnp.dot(x, wg_bf[...], preferred_element_type=F32)
    up = jnp.dot(x, wu_bf[...], preferred_element_type=F32)
    o_ref[...] = (_silu(gate) * up).astype(o_ref.dtype)


def _ffn_gate_up(x_bf, wg, wu, tm, tn, layer):
    m, k = x_bf.shape
    n = wg.shape[-1]
    return pl.pallas_call(
        _ffn_gu_kernel,
        grid=(n // tn, m // tm),
        in_specs=[pl.BlockSpec((tm, k), lambda j, i: (i, 0)), _w_spec(wg, tn, layer), _w_spec(wu, tn, layer)],
        out_specs=pl.BlockSpec((tm, tn), lambda j, i: (i, j)),
        out_shape=jax.ShapeDtypeStruct((m, n), BF16),
        scratch_shapes=[pltpu.VMEM((k, tn), BF16), pltpu.VMEM((k, tn), BF16)],
        compiler_params=_cparams(2),
    )(x_bf, wg, wu)


def _mm_ln_kernel(*refs, k_sizes, cast_w):
    n_x = len(k_sizes)
    x_refs = refs[:n_x]
    w_ref, r_ref, g_ref, b_ref, of_ref, ob_ref = refs[n_x:n_x + 6]
    if cast_w:
        wbf = refs[n_x + 6]

        @pl.when(pl.program_id(0) == 0)
        def _():
            wbf[...] = w_ref[...].astype(BF16)
    else:
        wbf = w_ref
    acc = None
    off = 0
    for xr, ks in zip(x_refs, k_sizes):
        part = jnp.dot(xr[...].astype(BF16), wbf[off:off + ks, :], preferred_element_type=F32)
        acc = part if acc is None else acc + part
        off += ks
    z = DN_ALPHA * r_ref[...] + acc
    zc = z - jnp.mean(z, axis=-1, keepdims=True)
    var = jnp.mean(zc * zc, axis=-1, keepdims=True)
    out = zc * lax.rsqrt(var + 1e-5) * g_ref[...] + b_ref[...]
    of_ref[...] = out
    ob_ref[...] = out.astype(BF16)


def _matmul_res_ln(xs, w, layer, resid, g, b, tm):
    m, n = resid.shape
    k = w.shape[-2]
    k_sizes = tuple(x.shape[1] for x in xs)
    cast_w = w.dtype != BF16
    assert sum(k_sizes) == k and m % tm == 0 and w.shape[-1] == n
    row = pl.BlockSpec((tm, n), lambda i: (i, 0))
    vec = pl.BlockSpec((1, n), lambda i: (0, 0))
    in_specs = [pl.BlockSpec((tm, ks), lambda i: (i, 0)) for ks in k_sizes]
    in_specs.append(pl.BlockSpec((None, k, n), lambda i: (layer, 0, 0), pipeline_mode=pl.Buffered(1)))
    in_specs += [row, vec, vec]
    return pl.pallas_call(
        functools.partial(_mm_ln_kernel, k_sizes=k_sizes, cast_w=cast_w),
        grid=(m // tm,),
        in_specs=in_specs,
        out_specs=[row, row],
        out_shape=[jax.ShapeDtypeStruct((m, n), F32), jax.ShapeDtypeStruct((m, n), BF16)],
        scratch_shapes=[pltpu.VMEM((k, n), BF16)] if cast_w else [],
        compiler_params=_cparams(1),
    )(*xs, w, resid, g.reshape(1, n), b.reshape(1, n))


def _xattn_kernel(q_ref, k_ref, v_ref, o_ref, *, nb, tq):
    for j in range(nb):
        rows = slice(j * tq, (j + 1) * tq)
        for h in range(XA_H):
            cols = slice(h * XA_HD, (h + 1) * XA_HD)
            s = _dot_nt(q_ref[rows, cols], k_ref[j, :, cols]) * (XA_HD ** -0.5)
            e = jnp.exp(s - jnp.max(s, axis=-1, keepdims=True))
            pr = e / jnp.sum(e, axis=-1, keepdims=True)
            o_ref[rows, cols] = _dot(pr, v_ref[j, :, cols]).astype(o_ref.dtype)


def _xattn_cache_body(q_ref, k_ref, v_ref, o_ref, nb, tq):
    nr = N_MEM * XA_H
    r = lax.broadcasted_iota(jnp.int32, (nr, XA_H * tq), 0)
    q = lax.broadcasted_iota(jnp.int32, (nr, XA_H * tq), 1)
    own = (r & (XA_H - 1)) == lax.shift_right_arithmetic(q, jnp.int32(int(math.log2(tq))))
    for j in range(nb):
        rows = slice(j * tq, (j + 1) * tq)
        kf = k_ref[j].reshape(nr, XA_HD)
        vf = v_ref[j].reshape(nr, XA_HD)
        qcat = jnp.concatenate([q_ref[rows, h * XA_HD:(h + 1) * XA_HD] for h in range(XA_H)], axis=0)
        s = jnp.where(own, _dot_nt(kf, qcat) * (XA_HD ** -0.5), -jnp.inf)
        e = jnp.exp(s - jnp.max(s, axis=0, keepdims=True))
        pr = e / jnp.sum(e, axis=0, keepdims=True)
        o = _dot_tn(pr, vf)
        for h in range(XA_H):
            o_ref[rows, h * XA_HD:(h + 1) * XA_HD] = o[h * tq:(h + 1) * tq, :].astype(o_ref.dtype)


def _ffn_xattn_kernel(x_ref, wg_ref, wu_ref, q_ref, k_ref, v_ref, o_ref, att_ref, wg_bf, wu_bf, *, nb, tq, n_att):
    _ffn_gu_kernel(x_ref, wg_ref, wu_ref, o_ref, wg_bf, wu_bf)
    step = pl.program_id(0) * pl.num_programs(1) + pl.program_id(1)

    @pl.when(step < n_att)
    def _():
        _xattn_cache_body(q_ref, k_ref, v_ref, att_ref, nb, tq)


def _ffn_gate_up_with_cache_attn(x_bf, wg, wu, tm, tn, layer, q, cache_k, cache_v, bsz, t, nb):
    m, k = x_bf.shape
    n = wg.shape[-1]
    n_inner = m // tm
    n_att = bsz // nb
    assert (n // tn) * n_inner >= n_att
    rows = nb * t

    def blk(j, i):
        return jnp.minimum(j * n_inner + i, n_att - 1)

    kv_spec = pl.BlockSpec((None, nb, N_MEM, XA_H, XA_HD), lambda j, i: (layer, blk(j, i), 0, 0, 0))
    q_spec = pl.BlockSpec((rows, D_MODEL), lambda j, i: (blk(j, i), 0))
    return pl.pallas_call(
        functools.partial(_ffn_xattn_kernel, nb=nb, tq=t, n_att=n_att),
        grid=(n // tn, n_inner),
        in_specs=[pl.BlockSpec((tm, k), lambda j, i: (i, 0)), _w_spec(wg, tn, layer), _w_spec(wu, tn, layer),
                  q_spec, kv_spec, kv_spec],
        out_specs=[pl.BlockSpec((tm, tn), lambda j, i: (i, j)), q_spec],
        out_shape=[jax.ShapeDtypeStruct((m, n), BF16), jax.ShapeDtypeStruct((bsz * t, D_MODEL), F32)],
        scratch_shapes=[pltpu.VMEM((k, tn), BF16), pltpu.VMEM((k, tn), BF16)],
        compiler_params=_cparams(2),
    )(x_bf, wg, wu, q, cache_k, cache_v)


def _cross_attn(q, mem_k, mem_v, bsz, t, nb, tq, out_dtype):
    nt = t // tq
    rows = nb * tq
    kv_spec = pl.BlockSpec((nb, N_MEM, D_MODEL), lambda i, j: (i, 0, 0))
    return pl.pallas_call(
        functools.partial(_xattn_kernel, nb=nb, tq=tq),
        grid=(bsz // nb, nt),
        in_specs=[pl.BlockSpec((rows, D_MODEL), lambda i, j: (i * nt + j, 0)), kv_spec, kv_spec],
        out_specs=pl.BlockSpec((rows, D_MODEL), lambda i, j: (i * nt + j, 0)),
        out_shape=jax.ShapeDtypeStruct((bsz * t, D_MODEL), out_dtype),
        compiler_params=_cparams(2),
    )(q, mem_k, mem_v)


def _state_io(carry, h0_ref, hout_ref, hst):
    if carry:
        return (lambda s, h: hst[h]), (lambda s, h, val: hst.__setitem__(h, val))
    return (lambda s, h: h0_ref[s, h]), (lambda s, h, val: hout_ref.__setitem__((s, h), val))


def _for_segments(nseg, fn):
    if nseg == 1:
        fn(0)
    else:
        def body(s, c):
            fn(s)
            return c
        lax.fori_loop(0, nseg, body, 0, unroll=True)


def _ret_kernel(q_ref, k_ref, v_ref, g_ref, cos_ref, sin_ref, gn_ref, h0_ref, o_ref, hout_ref,
                hst, qs_sc, ks_sc, y_sc, *, rows, seg, carry):
    c = CHUNK
    nseg = c // seg
    if carry:
        @pl.when(pl.program_id(1) == 0)
        def _():
            hst[...] = h0_ref[0]
    get_h, set_h = _state_io(carry, h0_ref, hout_ref, hst)
    incl, _, _, r_i, c_i = _seg_masks(c, seg)
    dpos = (r_i - c_i).astype(F32)
    tau = (lax.broadcasted_iota(jnp.int32, (c, RET_HD), 0) & (seg - 1)).astype(F32)
    even = (lax.broadcasted_iota(jnp.int32, (c, GROUP), 1) & 1) == 0

    for ci in range(rows // c):
        rws = slice(ci * c, (ci + 1) * c)
        cosb = jnp.concatenate([cos_ref[rws, :]] * RET_H, axis=1)
        sinb = jnp.concatenate([sin_ref[rws, :]] * RET_H, axis=1)

        def rot(x):
            swapped = jnp.where(even, pltpu.roll(x, GROUP - 1, 1), pltpu.roll(x, 1, 1))
            return x * cosb + swapped * sinb

        qr = rot(q_ref[rws, :])
        kr = rot(k_ref[rws, :]) * (RET_HD ** -0.5)
        for h in range(RET_H):
            lgam = RET_LOG_GAMMA[h]
            cols = slice(h * RET_HD, (h + 1) * RET_HD)
            qh, kh = qr[:, cols], kr[:, cols]
            dm = jnp.where(incl, jnp.exp(dpos * lgam), 0.0)
            y_sc[...] = _dot(_dot_nt(qh, kh) * dm, v_ref[rws, cols])
            qs_sc[...] = qh * jnp.exp((tau + 1.0) * lgam)
            ks_sc[...] = kh * jnp.exp((seg - 1.0 - tau) * lgam)
            cd = math.exp(seg * lgam)

            def seg_step(s, h=h, cols=cols, cd=cd, ci=ci):
                sr = pl.ds(_mo(s * seg, seg), seg)
                vr = pl.ds(_mo(ci * c + s * seg, seg), seg)
                hs = get_h(s, h)
                y_sc[sr, :] += _dot(qs_sc[sr, :], hs)
                set_h(s, h, cd * hs + _dot_tn(ks_sc[sr, :], v_ref[vr, cols]))

            _for_segments(nseg, seg_step)
            y = y_sc[...]
            yn = y * lax.rsqrt(jnp.mean(y * y, axis=-1, keepdims=True) + 1e-5) * gn_ref[:, cols]
            o_ref[rws, cols] = (yn * _silu(g_ref[rws, cols])).astype(o_ref.dtype)

    if carry:
        @pl.when(pl.program_id(1) == pl.num_programs(1) - 1)
        def _():
            hout_ref[0] = hst[...]


def _mamba_kernel(z_ref, xbc_ref, dt_ref, cprev_ref, h0_ref, cw_ref, cb_ref, dtb_ref, alog_ref, dd_ref,
                  ng_ref, e64_ref, e128_ref, o_ref, cout_ref, hout_ref,
                  hst, xpad, act_sc, qs_sc, ks_sc, xdt_sc, xs_sc, yi_sc, yf_sc, etot_sc, *, rows, seg, carry):
    c = CHUNK
    nseg = c // seg
    get_h, set_h = _state_io(carry, h0_ref, hout_ref, hst)

    def conv_act(window, n):
        acc = cb_ref[...] + window[5:5 + n] * cw_ref[0:1, :]
        for i in range(1, MB_CONV):
            acc = acc + window[5 + i:5 + i + n] * cw_ref[i:i + 1, :]
        return _silu(acc)

    if carry:
        @pl.when(pl.program_id(1) == 0)
        def _():
            hst[...] = h0_ref[0]
            xpad[0:8, :] = jnp.zeros((8, MB_CONV_DIM), F32)
            xpad[5:8, :] = cprev_ref[0]

        xpad[8:8 + rows, :] = xbc_ref[...]
        for ci in range(rows // c):
            acc = cb_ref[...] + xpad[ci * c + 5:ci * c + 5 + c, :] * cw_ref[0:1, :]
            for i in range(1, MB_CONV):
                acc = acc + xpad[ci * c + 5 + i:ci * c + 5 + i + c, :] * cw_ref[i:i + 1, :]
            act_sc[ci * c:(ci + 1) * c, :] = _silu(acc)
        xpad[0:8, :] = xpad[rows:rows + 8, :]

        @pl.when(pl.program_id(1) == pl.num_programs(1) - 1)
        def _():
            cout_ref[0] = xpad[5:8, :]
    else:
        def conv_seq(s, carry_):
            sr = pl.ds(pl.multiple_of(s * seg, seg), seg)
            xpad[5:8, :] = cprev_ref[s]
            xs = xbc_ref[sr, :]
            window = jnp.concatenate([xpad[0:8, :], xs], axis=0)
            act_sc[sr, :] = conv_act(window, seg)
            cout_ref[s] = xs[seg - 3:seg]
            return carry_

        xpad[0:8, :] = jnp.zeros((8, MB_CONV_DIM), F32)
        lax.fori_loop(0, rows // seg, conv_seq, 0)

    incl, _, same, _, _ = _seg_masks(c, seg)
    lt_bf = jnp.where(incl, 1.0, 0.0).astype(BF16)
    same_bf = jnp.where(same, 1.0, 0.0).astype(BF16)
    a_neg = -jnp.exp(alog_ref[...])

    for ci in range(rows // c):
        rws = slice(ci * c, (ci + 1) * c)
        dtv = _softplus(dt_ref[rws, :] + dtb_ref[...])
        la = dtv * a_neg
        cum = _split_dot(lt_bf, la)
        tot = _split_dot(same_bf, la)
        cum_t = cum.T
        etot_sc[...] = jnp.exp(tot)
        cum_cols = _spread(cum, e128_ref[...])
        cum_full = _spread(cum, e64_ref[...])
        xh_all = act_sc[rws, 0:GROUP]
        xdt_all = xh_all * _spread(dtv, e64_ref[...])
        xdt_sc[...] = xdt_all
        xs_sc[...] = xdt_all * jnp.exp(_spread(tot, e64_ref[...]) - cum_full)
        for g in range(MB_G):
            cg = act_sc[rws, 768 + g * MB_N:768 + (g + 1) * MB_N]
            bg = act_sc[rws, 512 + g * MB_N:512 + (g + 1) * MB_N]
            gmat = _dot_nt(cg, bg)
            qs_sc[...] = cg
            ks_sc[...] = bg
            for hh in range(MB_H // MB_G):
                h = g * (MB_H // MB_G) + hh
                cols = slice(h * MB_HD, (h + 1) * MB_HD)
                lmat = jnp.exp(jnp.where(incl, cum_cols[:, h * c:(h + 1) * c] - cum_t[h:h + 1, :], -jnp.inf))
                yf_sc[:, cols] = _dot(gmat * lmat, xdt_sc[:, cols])

                def seg_step(s, h=h, cols=cols):
                    sr = pl.ds(_mo(s * seg, seg), seg)
                    first = pl.ds(_mo(s * seg, seg), 1)
                    hs = get_h(s, h)
                    yi_sc[sr, cols] = _dot_nt(qs_sc[sr, :], hs)
                    set_h(s, h, etot_sc[first, h:h + 1] * hs + _dot_tn(xs_sc[sr, cols], ks_sc[sr, :]))

                _for_segments(nseg, seg_step)
        y_all = yf_sc[...] + yi_sc[...] * jnp.exp(cum_full) + dd_ref[...] * xh_all
        yz = y_all * _silu(z_ref[rws, :])
        gw = GROUP // MB_G
        for g in range(MB_G):
            cols = slice(g * gw, (g + 1) * gw)
            part = yz[:, cols]
            nrm = part * lax.rsqrt(jnp.mean(part * part, axis=-1, keepdims=True) + 1e-5) * ng_ref[:, cols]
            o_ref[rws, cols] = nrm.astype(o_ref.dtype)

    if carry:
        @pl.when(pl.program_id(1) == pl.num_programs(1) - 1)
        def _():
            hout_ref[0] = hst[...]


def _gla_kernel(q_ref, k_ref, v_ref, gk_ref, g_ref, h0_ref, gkup_ref, gkb_ref, ng_ref, e_ref,
                o_ref, hout_ref,
                hst, qe_sc, ke_sc, etot_sc, p_sc, y_sc, *, rows, seg, carry):
    c = CHUNK
    nsub = c // seg
    if carry:
        @pl.when(pl.program_id(1) == 0)
        def _():
            for h in range(GLA_H):
                hst[h] = h0_ref[0, h].T
    incl, _, same, _, _ = _seg_masks(c, seg)
    lt_bf = jnp.where(incl, 1.0, 0.0).astype(BF16)
    same_bf = jnp.where(same, 1.0, 0.0).astype(BF16)
    row_i = lax.broadcasted_iota(jnp.int32, (seg, GLA_QK), 0)
    kcs = [slice(h * GLA_DK, (h + 1) * GLA_DK) for h in range(GLA_H)]
    vcs = [slice(h * GLA_DV, (h + 1) * GLA_DV) for h in range(GLA_H)]

    for ci in range(rows // c):
        rws = slice(ci * c, (ci + 1) * c)
        pre = _dot(gk_ref[rws, :], gkup_ref[...]) + gkb_ref[...]
        la = -_softplus(-pre) * (1.0 / GLA_TAU)
        cum = _split_dot(lt_bf, la)
        tot = _split_dot(same_bf, la)
        qv = q_ref[rws, :] * (GLA_DK ** -0.5)
        kv = k_ref[rws, :]
        qe_sc[...] = qv * jnp.exp(cum)
        ke_sc[...] = kv * jnp.exp(tot - cum)
        etot_sc[...] = jnp.exp(tot)

        safe = (jnp.max(-cum) < GLA_SAFE_EXP) if carry else False

        def factored():
            kinv = kv * jnp.exp(-cum)
            for h in range(GLA_H):
                sc = jnp.where(incl, _dot_nt(qe_sc[:, kcs[h]], kinv[:, kcs[h]]), 0.0)
                y_sc[:, vcs[h]] = _dot(sc, v_ref[rws, vcs[h]])

        def pairwise():
            for u in range(nsub):
                sr = slice(u * seg, (u + 1) * seg)
                cu, qu = cum[sr], qv[sr]
                for j in range(seg):
                    r0 = u * seg + j
                    pj = qu * jnp.exp(jnp.where(row_i >= j, cu - cum[r0:r0 + 1], -jnp.inf)) * kv[r0:r0 + 1]
                    p_sc[u, j * seg:(j + 1) * seg, :] = pj
            for u in range(nsub):
                rm = _dot(p_sc[u], e_ref[...])
                y = rm[0:seg, :] * v_ref[ci * c + u * seg:ci * c + u * seg + 1, :]
                for j in range(1, seg):
                    r0 = ci * c + u * seg + j
                    y = y + rm[j * seg:(j + 1) * seg, :] * v_ref[r0:r0 + 1, :]
                y_sc[u * seg:(u + 1) * seg, :] = y

        if carry:
            pl.when(safe)(factored)
            pl.when(jnp.logical_not(safe))(pairwise)
        else:
            pairwise()
        states = [hst[h] for h in range(GLA_H)] if carry else None
        for u in range(nsub):
            sr = slice(u * seg, (u + 1) * seg)
            vr = slice(ci * c + u * seg, ci * c + (u + 1) * seg)
            hts = [states[h] if carry else h0_ref[u, h].T for h in range(GLA_H)]
            for h in range(GLA_H):
                y_sc[sr, vcs[h]] += _dot_nt(qe_sc[sr, kcs[h]], hts[h])
            upds = [_dot_tn(v_ref[vr, vcs[h]], ke_sc[sr, kcs[h]]) for h in range(GLA_H)]
            for h in range(GLA_H):
                new = hts[h] * etot_sc[u * seg:u * seg + 1, kcs[h]] + upds[h]
                if carry:
                    states[h] = new
                else:
                    hout_ref[u, h] = new.T
        if carry:
            for h in range(GLA_H):
                hst[h] = states[h]

        for h in range(GLA_H):
            y = y_sc[:, vcs[h]]
            yn = y * lax.rsqrt(jnp.mean(y * y, axis=-1, keepdims=True) + 1e-5) * ng_ref[:, vcs[h]]
            o_ref[rws, vcs[h]] = (yn * _silu(g_ref[rws, vcs[h]])).astype(o_ref.dtype)

    if carry:
        @pl.when(pl.program_id(1) == pl.num_programs(1) - 1)
        def _():
            for h in range(GLA_H):
                hout_ref[0, h] = hst[h].T


def _rwkv_kernel(p_ref, sp_ref, h0_ref, mu_ref, w0_ref, wup_ref, a0_ref, aup_ref, gup_ref, kk_ref, ka_ref,
                 rk_ref, lng_ref, lnb_ref, ones_ref, o_ref, hout_ref,
                 hst, last_sc, prev_sc, w_sc, y_sc, bt_sc, kt_sc, v_sc, rt_sc, u_sc, rkb_sc, etot_sc, out_sc, kap_sc,
                 pw_sc, t_sc, ak_sc, pb_sc, pk_sc, *, rows, seg, sub, carry):
    c = CHUNK
    assert rows == c
    nsub = c // sub
    n_iter = int(math.log2(sub)) - 1
    npair = RW_H // 2
    pw_ = 2 * RW_HD
    pr_i = lax.broadcasted_iota(jnp.int32, (pw_, pw_), 0)
    pc_i = lax.broadcasted_iota(jnp.int32, (pw_, pw_), 1)
    bd_mask = (pr_i >= RW_HD) == (pc_i >= RW_HD)
    low_rows = lax.broadcasted_iota(jnp.int32, (pw_, RW_HD), 0) < RW_HD

    def to_bd(stack):
        return jnp.where(bd_mask, jnp.concatenate([stack, stack], axis=1), 0.0)

    def from_bd(bd):
        return jnp.where(low_rows, bd[:, 0:RW_HD], bd[:, RW_HD:pw_])

    if carry:
        @pl.when(pl.program_id(1) == 0)
        def _():
            for pi in range(npair):
                hst[pi] = to_bd(h0_ref[0, pi])
            last_sc[...] = jnp.broadcast_to(sp_ref[0], (8, RW_IN))

        prev_row = jnp.broadcast_to(last_sc[0:1, :], (c, RW_IN))
    else:
        for s in range(rows // seg):
            prev_sc[s * seg:(s + 1) * seg, :] = jnp.broadcast_to(sp_ref[s], (seg, RW_IN))
        prev_row = prev_sc[...]
    incl, strict, same, r_i, c_i = _seg_masks(c, sub)
    lt_bf = jnp.where(incl, 1.0, 0.0).astype(BF16)
    same_bf = jnp.where(same, 1.0, 0.0).astype(BF16)
    eye = jnp.where(r_i == c_i, 1.0, 0.0)
    row_w = lax.broadcasted_iota(jnp.int32, (c, RW_IN), 0)
    first_row = (row_w & ((c if carry else seg) - 1)) == 0

    p = p_ref[...]
    prev = jnp.where(first_row, prev_row, pltpu.roll(p, 1, 0))
    xs = p + (prev - p) * mu_ref[...]
    r = xs[:, 0:GROUP]
    k = xs[:, GROUP:2 * GROUP]
    v = xs[:, 2 * GROUP:3 * GROUP]
    wa = xs[:, 3 * GROUP:3 * GROUP + 128]
    gd = xs[:, 3 * GROUP + 128:RW_IN]
    w = -_softplus(-(w0_ref[...] + _dot(jnp.tanh(wa), wup_ref[...]))) - 0.5
    ld = -jnp.exp(w)
    a = _sigmoid(a0_ref[...] + _dot(wa, aup_ref[...]))
    gate = _dot(_sigmoid(gd), gup_ref[...])
    kkr = k * kk_ref[...]
    k2 = k * (1.0 + (a - 1.0) * ka_ref[...])
    cum = _split_dot(lt_bf, ld)
    tot = _split_dot(same_bf, ld)
    ecum = jnp.exp(cum)
    einv = jnp.exp(-cum)
    eprev = jnp.exp(cum - ld)
    rkb_sc[...] = r * k2 * rk_ref[...]
    etot_sc[...] = jnp.exp(tot)
    v_sc[...] = v
    rtil = r * ecum
    rt_sc[...] = rtil
    kt_sc[...] = k2 * einv
    rn = lax.rsqrt(jnp.maximum(_head_sums(kkr * kkr, ones_ref[...]), 1e-24))
    braw = kkr * a * einv * rn
    kraw = kkr * eprev * rn

    heads = [slice(h * RW_HD, (h + 1) * RW_HD) for h in range(RW_H)]
    for h, cols in enumerate(heads):
        kap = kraw[:, cols]
        bt = braw[:, cols]
        kap_sc[:, cols] = kap
        bt_sc[:, cols] = bt
        m1 = _dot_nt(jnp.concatenate([kap, rtil[:, cols]], axis=0), jnp.concatenate([bt, kt_sc[:, cols]], axis=0))
        x = jnp.where(strict, -m1[0:c, 0:c], 0.0)
        pw_sc[0, h] = x
        t_sc[h] = eye + x
        ak_sc[h] = jnp.where(strict, m1[0:c, c:2 * c], 0.0)
        pb_sc[h] = jnp.where(incl, m1[c:2 * c, 0:c], 0.0)
        pk_sc[h] = jnp.where(incl, m1[c:2 * c, c:2 * c], 0.0)
    for h, cols in enumerate(heads):
        y_sc[:, cols] = _dot(ak_sc[h], v_sc[:, cols])
    for it in range(n_iter):
        src, dst = it % 2, (it + 1) % 2
        for h in range(RW_H):
            pw = pw_sc[src, h]
            pw_sc[dst, h] = _dot(pw, pw)
        for h in range(RW_H):
            tm = t_sc[h]
            t_sc[h] = tm + _dot(tm, pw_sc[dst, h])
    for h, cols in enumerate(heads):
        wy = _dot(t_sc[h], jnp.concatenate([kap_sc[:, cols], y_sc[:, cols]], axis=1))
        w_sc[:, cols] = wy[:, 0:RW_HD]
        y_sc[:, cols] = wy[:, RW_HD:2 * RW_HD]

    states = [hst[pi] for pi in range(npair)] if carry else None
    units = [(s, pi) for s in range(nsub) for pi in range(npair)]
    group = npair if carry else 2 * npair
    for g0 in range(0, len(units), group):
        grp_units = units[g0:g0 + group]
        sts, wrs, uus, upds = [], [], [], []
        for s, pi in grp_units:
            sts.append(states[pi] if carry else to_bd(h0_ref[s, pi]))
        for (s, pi), st in zip(grp_units, sts):
            sr, pc = slice(s * sub, (s + 1) * sub), slice(pi * pw_, (pi + 1) * pw_)
            wrs.append(_dot_nt(jnp.concatenate([w_sc[sr, pc], rt_sc[sr, pc]], axis=0), st))
        for (s, pi), wr in zip(grp_units, wrs):
            sr, pc = slice(s * sub, (s + 1) * sub), slice(pi * pw_, (pi + 1) * pw_)
            uu = -wr[0:sub, :] - y_sc[sr, pc]
            u_sc[sr, pc] = uu
            out_sc[sr, pc] = wr[sub:2 * sub, :]
            uus.append(uu)
        for (s, pi), uu in zip(grp_units, uus):
            sr, pc = slice(s * sub, (s + 1) * sub), slice(pi * pw_, (pi + 1) * pw_)
            upds.append(_dot_tn(jnp.concatenate([uu, v_sc[sr, pc]], axis=0),
                                jnp.concatenate([bt_sc[sr, pc], kt_sc[sr, pc]], axis=0)))
        for (s, pi), st, upd in zip(grp_units, sts, upds):
            pc = slice(pi * pw_, (pi + 1) * pw_)
            new = (st + jnp.where(bd_mask, upd, 0.0)) * etot_sc[s * sub:s * sub + 1, pc]
            if carry:
                states[pi] = new
            else:
                hout_ref[s, pi] = from_bd(new)
    if carry:
        for pi in range(npair):
            hst[pi] = states[pi]

    for h, cols in enumerate(heads):
        out_sc[:, cols] += _dot(jnp.concatenate([pb_sc[h], pk_sc[h]], axis=1),
                                jnp.concatenate([u_sc[:, cols], v_sc[:, cols]], axis=0))
    ones = ones_ref[...]
    o_all = out_sc[...]
    oc = o_all - _head_sums(o_all, ones) * (1.0 / RW_HD)
    var = _head_sums(oc * oc, ones) * (1.0 / RW_HD)
    on = oc * lax.rsqrt(var + RW_LN_EPS) * lng_ref[...] + lnb_ref[...]
    out_sc[...] = on + _head_sums(rkb_sc[...], ones) * v_sc[...]
    o_ref[...] = (out_sc[...] * gate).astype(o_ref.dtype)

    if carry:
        last_sc[...] = jnp.broadcast_to(p_ref[rows - 1:rows, :], (8, RW_IN))

        @pl.when(pl.program_id(1) == pl.num_programs(1) - 1)
        def _():
            for pi in range(npair):
                hout_ref[0, pi] = from_bd(hst[pi])


class _Group:
    def __init__(self, bsz, t, rows):
        self.bsz, self.t, self.rows = bsz, t, rows
        self.carry = t >= CHUNK
        self.nb = 1 if self.carry else rows // t
        self.nt = t // rows if self.carry else 1
        self.grid = (bsz // self.nb, self.nt)
        self.seg = CHUNK if self.carry else t

    def rows_spec(self, width, col_block):
        nt = self.nt
        return pl.BlockSpec((self.rows, width), lambda i, j: (i * nt + j, col_block))

    def state_spec(self, shape, layer=None):
        zeros = (0,) * len(shape)
        if layer is None:
            return pl.BlockSpec((self.nb,) + tuple(shape), lambda i, j: (i,) + zeros)
        return pl.BlockSpec((None, self.nb) + tuple(shape), lambda i, j: (layer, i) + zeros)

    def out_rows(self, width, dtype=BF16):
        nt = self.nt
        return (pl.BlockSpec((self.rows, width), lambda i, j: (i * nt + j, 0)),
                jax.ShapeDtypeStruct((self.bsz * self.t, width), dtype))


def _vec_spec(shape):
    zeros = (0,) * len(shape)
    return pl.BlockSpec(tuple(shape), lambda i, j: zeros)


def _retention(grp, p, cos_t, sin_t, h0, sl, gn):
    c = CHUNK
    shape = (RET_H, RET_HD, RET_HD)
    in_specs = [grp.rows_spec(GROUP, COL_RET // GROUP + n) for n in range(4)]
    tab = pl.BlockSpec((grp.rows, RET_HD), (lambda i, j: (j, 0)) if grp.carry else (lambda i, j: (0, 0)))
    in_specs += [tab, tab, _vec_spec((1, GROUP)), grp.state_spec(shape, sl)]
    o_spec, o_shape = grp.out_rows(GROUP)
    return pl.pallas_call(
        functools.partial(_ret_kernel, rows=grp.rows, seg=grp.seg, carry=grp.carry),
        grid=grp.grid,
        in_specs=in_specs,
        out_specs=[o_spec, grp.state_spec(shape)],
        out_shape=[o_shape, jax.ShapeDtypeStruct(h0.shape[1:], F32)],
        scratch_shapes=[pltpu.VMEM(shape, F32), pltpu.VMEM((c, RET_HD), F32),
                        pltpu.VMEM((c, RET_HD), F32), pltpu.VMEM((c, RET_HD), F32)],
        compiler_params=_cparams(2),
    )(p, p, p, p, cos_t, sin_t, gn.reshape(1, GROUP), h0)


def _mamba(grp, p, cprev, h0, sl, lp):
    c = CHUNK
    cshape, hshape = (MB_CONV - 1, MB_CONV_DIM), (MB_H, MB_HD, MB_N)
    h0 = jnp.swapaxes(h0, -1, -2)
    pad8 = lambda a: jnp.pad(a.reshape(1, MB_H), ((0, 0), (0, 128 - MB_H)))
    head = jnp.arange(128)[:, None]
    e64 = (head == jnp.arange(GROUP)[None, :] // MB_HD).astype(BF16)
    e128 = (head == jnp.arange(MB_H * CHUNK)[None, :] // CHUNK).astype(BF16)
    in_specs = [grp.rows_spec(GROUP, COL_MB_Z // GROUP), grp.rows_spec(MB_CONV_DIM, COL_MB_XBC // MB_CONV_DIM),
                grp.rows_spec(128, COL_MB_DT // 128),
                grp.state_spec(cshape, sl), grp.state_spec(hshape, sl),
                _vec_spec((MB_CONV, MB_CONV_DIM)), _vec_spec((1, MB_CONV_DIM)), _vec_spec((1, 128)),
                _vec_spec((1, 128)), _vec_spec((1, GROUP)), _vec_spec((1, GROUP)),
                _vec_spec((128, GROUP)), _vec_spec((128, MB_H * CHUNK))]
    o_spec, o_shape = grp.out_rows(GROUP)
    return pl.pallas_call(
        functools.partial(_mamba_kernel, rows=grp.rows, seg=grp.seg, carry=grp.carry),
        grid=grp.grid,
        in_specs=in_specs,
        out_specs=[o_spec, grp.state_spec(cshape), grp.state_spec(hshape)],
        out_shape=[o_shape, jax.ShapeDtypeStruct(cprev.shape[1:], F32), jax.ShapeDtypeStruct(h0.shape[1:], F32)],
        scratch_shapes=[pltpu.VMEM(hshape, F32),
                        pltpu.VMEM(((grp.rows if grp.carry else 0) + 8, MB_CONV_DIM), F32),
                        pltpu.VMEM((grp.rows, MB_CONV_DIM), F32),
                        pltpu.VMEM((c, MB_N), F32), pltpu.VMEM((c, MB_N), F32), pltpu.VMEM((c, GROUP), F32),
                        pltpu.VMEM((c, GROUP), F32), pltpu.VMEM((c, GROUP), F32), pltpu.VMEM((c, GROUP), F32),
                        pltpu.VMEM((c, 128), F32)],
        compiler_params=_cparams(2),
    )(p, p, p, cprev, h0, lp['mb_conv_w'], lp['mb_conv_b'].reshape(1, MB_CONV_DIM), pad8(lp['mb_dt_bias']),
      pad8(lp['mb_a_log']), jnp.repeat(lp['mb_d'], MB_HD).reshape(1, GROUP), lp['mb_norm_g'].reshape(1, GROUP),
      e64, e128)


def _gla(grp, p, h0, sl, lp, expand):
    c = CHUNK
    shape = (GLA_H, GLA_DK, GLA_DV)
    sub = 64 if grp.carry else grp.seg
    gk_up = jnp.pad(lp['gla_gk_up'], ((0, 128 - GLA_LORA), (0, 0)))
    in_specs = [grp.rows_spec(GLA_QK, COL_GLA_Q // GLA_QK), grp.rows_spec(GLA_QK, COL_GLA_K // GLA_QK),
                grp.rows_spec(GROUP, COL_GLA_V // GROUP), grp.rows_spec(128, COL_GLA_GK // 128),
                grp.rows_spec(GROUP, COL_GLA_G // GROUP), grp.state_spec(shape, sl),
                _vec_spec((128, GLA_QK)), _vec_spec((1, GLA_QK)), _vec_spec((1, GROUP)), _vec_spec((GLA_QK, GROUP))]
    o_spec, o_shape = grp.out_rows(GROUP)
    return pl.pallas_call(
        functools.partial(_gla_kernel, rows=grp.rows, seg=sub, carry=grp.carry),
        grid=grp.grid,
        in_specs=in_specs,
        out_specs=[o_spec, grp.state_spec(shape)],
        out_shape=[o_shape, jax.ShapeDtypeStruct(h0.shape[1:], F32)],
        scratch_shapes=[pltpu.VMEM((GLA_H, GLA_DV, GLA_DK), F32)]
        + [pltpu.VMEM((c, GLA_QK), F32) for _ in range(3)]
        + [pltpu.VMEM((c // sub, sub * sub, GLA_QK), F32), pltpu.VMEM((c, GROUP), F32)],
        compiler_params=_cparams(2),
    )(p, p, p, p, p, h0, gk_up, lp['gla_gk_b'].reshape(1, GLA_QK), lp['gla_norm_g'].reshape(1, GROUP), expand)


def _rwkv(grp, p, shift_prev, h0, sl, lp):
    c = CHUNK
    head_ones = (jnp.arange(GROUP)[:, None] // RW_HD == jnp.arange(GROUP)[None, :] // RW_HD).astype(BF16)
    shape = (RW_H // 2, 2 * RW_HD, RW_HD)
    h0 = h0.reshape(h0.shape[:2] + shape)
    sub = 64 if grp.carry else grp.seg
    row = lambda a: a.reshape(1, -1)
    w_up = jnp.pad(lp['rw_w_up'], ((0, 64), (0, 0)))
    a_up = jnp.pad(lp['rw_a_up'], ((64, 0), (0, 0)))
    in_specs = [grp.rows_spec(RW_IN, 0), grp.state_spec((1, RW_IN), sl), grp.state_spec(shape, sl),
                _vec_spec((1, RW_IN)), _vec_spec((1, GROUP)), _vec_spec((128, GROUP)), _vec_spec((1, GROUP)),
                _vec_spec((128, GROUP)), _vec_spec((128, GROUP))] + [_vec_spec((1, GROUP))] * 5 + [_vec_spec((GROUP, GROUP))]
    o_spec, o_shape = grp.out_rows(GROUP)
    wide = lambda: pltpu.VMEM((c, GROUP), F32)
    return pl.pallas_call(
        functools.partial(_rwkv_kernel, rows=grp.rows, seg=grp.seg, sub=sub, carry=grp.carry),
        grid=grp.grid,
        in_specs=in_specs,
        out_specs=[o_spec, grp.state_spec(shape)],
        out_shape=[o_shape, jax.ShapeDtypeStruct(h0.shape[1:], F32)],
        scratch_shapes=[pltpu.VMEM((RW_H // 2, 2 * RW_HD, 2 * RW_HD), F32), pltpu.VMEM((8, RW_IN), F32),
                        pltpu.VMEM((c, RW_IN), F32)] + [wide() for _ in range(11)]
        + [pltpu.VMEM((2, RW_H, c, c), F32)] + [pltpu.VMEM((RW_H, c, c), F32) for _ in range(4)],
        compiler_params=_cparams(2),
    )(p, shift_prev.reshape(shift_prev.shape[0], -1, 1, RW_IN), h0, row(lp['rw_mu']), row(lp['rw_w0']), w_up,
      row(lp['rw_a0']), a_up, lp['rw_g_up'], row(lp['rw_k_k']), row(lp['rw_k_a']), row(lp['rw_r_k']),
      row(lp['rw_ln_g']), row(lp['rw_ln_b']), head_ones)


def _rope_tables(pos0, t):
    half = RET_HD // 2
    inv = 1.0 / (ROPE_BASE ** jnp.linspace(0.0, 1.0, half, dtype=F32))
    pos = pos0 + jnp.arange(t, dtype=F32)
    ang = pos[:, None] * inv[None, :]
    cos, sin = jnp.cos(ang), jnp.sin(ang)
    cos_t = jnp.stack([cos, cos], axis=-1).reshape(t, RET_HD)
    sin_t = jnp.stack([-sin, sin], axis=-1).reshape(t, RET_HD)
    return cos_t, sin_t


def _pad_w_in(w):
    z = lambda n: jnp.zeros(w.shape[:-1] + (n,), w.dtype)
    return jnp.concatenate([
        w[..., 0:1792], w[..., 3328:3336], z(120), w[..., 4360:4376], z(112), w[..., 2304:3328],
        w[..., 1792:2304], w[..., 3848:4360], w[..., 4376:4888], w[..., 3336:3592], w[..., 3592:3848],
        w[..., 4888:6936]], axis=-1)


def _mix_block(grp, grp_rw, x, x_bf, pos_tabs, states, sl, lp, big, wl, expand, tm):
    rw_shift, rw_state, mb_conv, mb_state, gla_state, ret_state = states
    bsz, t = grp.bsz, grp.t
    p = _matmul([x_bf], big['w_in'], F32, tm, 1024, wl)
    o_rw, rw_new = _rwkv(grp_rw, p, rw_shift, rw_state, sl, lp)
    rw_new = rw_new.reshape(bsz, RW_H, RW_HD, RW_HD)
    o_mb, conv_new, mb_new = _mamba(grp, p, mb_conv, mb_state, sl, lp)
    mb_new = jnp.swapaxes(mb_new, -1, -2)
    o_gl, gla_new = _gla(grp, p, gla_state, sl, lp, expand)
    o_rt, ret_new = _retention(grp, p, pos_tabs[0], pos_tabs[1], ret_state, sl, lp['ret_norm_g'])
    shift_new = p.reshape(bsz, t, N_PAD)[:, t - 1, 0:RW_IN]
    x, x_bf = _matmul_res_ln([o_rw, o_mb, o_gl, o_rt], big['w_out'], wl, x, lp['ln1_g'], lp['ln1_b'],
                             512 if grp.carry else 256)
    q = _matmul([x_bf], big['xa_wq'], BF16 if grp.carry else F32, tm, 1024, wl)
    return x, q, (shift_new, rw_new, conv_new, mb_new, gla_new, ret_new)


def _layer_pair(gp, gp_rw, gs, xp, xp_bf, xs, xs_bf, tabs_p, tabs_s, zeros_p, st_s_in, mk, mv, cache_k, cache_v,
                lp, big, wl, expand):
    xp, qp, st_p = _mix_block(gp, gp_rw, xp, xp_bf, tabs_p, zeros_p, 0, lp, big, wl, expand, 1024)
    att_p = _cross_attn(qp, mk, mv, gp.bsz, gp.t, 1, 1024, BF16)
    xp, xp_bf = _matmul_res_ln([att_p], big['xa_wo'], wl, xp, lp['ln2_g'], lp['ln2_b'], 512)
    xs, qs, st_s = _mix_block(gs, gs, xs, xs_bf, tabs_s, st_s_in, wl, lp, big, wl, expand, 1024)
    hid_p, att_s = _ffn_gate_up_with_cache_attn(xp_bf, big['ffn_w_gate'], big['ffn_w_up'], 1024, 512, wl,
                                                qs, cache_k, cache_v, gs.bsz, gs.t, 2)
    xp, xp_bf = _matmul_res_ln([hid_p], big['ffn_w_down'], wl, xp, lp['ln3_g'], lp['ln3_b'], 256)
    xs, xs_bf = _matmul_res_ln([att_s], big['xa_wo'], wl, xs, lp['ln2_g'], lp['ln2_b'], 256)
    hid_s = _ffn_gate_up(xs_bf, big['ffn_w_gate'], big['ffn_w_up'], 1024, 512, wl)
    xs, xs_bf = _matmul_res_ln([hid_s], big['ffn_w_down'], wl, xs, lp['ln3_g'], lp['ln3_b'], 256)
    return xp, xp_bf, st_p, xs, xs_bf, st_s


def kernel(x_prompt, x_sample, state_rwkv_shift, state_rwkv_wkv, state_mamba_conv, state_mamba_ssm, state_gla,
           state_ret, cache_mem_k, cache_mem_v, mem_prompt, w_in, w_out, ln1_g, ln1_b, rw_mu, rw_w0, rw_w_up,
           rw_a0, rw_a_up, rw_g_up, rw_k_k, rw_k_a, rw_r_k, rw_ln_g, rw_ln_b, mb_conv_w, mb_conv_b, mb_dt_bias,
           mb_a_log, mb_d, mb_norm_g, gla_gk_up, gla_gk_b, gla_norm_g, ret_norm_g, ln2_g, ln2_b, xa_wq, xa_wk,
           xa_wv, xa_wo, ln3_g, ln3_b, ffn_w_gate, ffn_w_up, ffn_w_down):
    small = dict(
        ln1_g=ln1_g, ln1_b=ln1_b, rw_mu=rw_mu, rw_w0=rw_w0, rw_w_up=rw_w_up, rw_a0=rw_a0,
        rw_a_up=rw_a_up, rw_g_up=rw_g_up, rw_k_k=rw_k_k, rw_k_a=rw_k_a, rw_r_k=rw_r_k, rw_ln_g=rw_ln_g,
        rw_ln_b=rw_ln_b, mb_conv_w=mb_conv_w, mb_conv_b=mb_conv_b, mb_dt_bias=mb_dt_bias, mb_a_log=mb_a_log,
        mb_d=mb_d, mb_norm_g=mb_norm_g, gla_gk_up=gla_gk_up, gla_gk_b=gla_gk_b, gla_norm_g=gla_norm_g,
        ret_norm_g=ret_norm_g, ln2_g=ln2_g, ln2_b=ln2_b, ln3_g=ln3_g, ln3_b=ln3_b)
    big = dict(w_in=_pad_w_in(w_in), w_out=w_out, xa_wq=xa_wq, xa_wo=xa_wo, ffn_w_gate=ffn_w_gate,
               ffn_w_up=ffn_w_up, ffn_w_down=ffn_w_down.astype(BF16))
    bp, tp, _ = x_prompt.shape
    bs, ts, _ = x_sample.shape
    gp = _Group(bp, tp, 512)
    gp_rw = _Group(bp, tp, CHUNK)
    gs = _Group(bs, ts, CHUNK)
    expand = (jnp.arange(GLA_QK)[:, None] // GLA_DK == jnp.arange(GROUP)[None, :] // GLA_DV).astype(BF16)
    tabs_p = _rope_tables(0.0, tp)
    tabs_s = tuple(jnp.tile(tb, (gs.nb, 1)) for tb in _rope_tables(float(PAST_LEN), ts))
    zeros_p = (jnp.zeros((1, bp, RW_IN), F32), jnp.zeros((1, bp, RW_H, RW_HD, RW_HD), F32),
               jnp.zeros((1, bp, MB_CONV - 1, MB_CONV_DIM), F32), jnp.zeros((1, bp, MB_H, MB_N, MB_HD), F32),
               jnp.zeros((1, bp, GLA_H, GLA_DK, GLA_DV), F32), jnp.zeros((1, bp, RET_H, RET_HD, RET_HD), F32))
    st_s_in = (state_rwkv_shift, state_rwkv_wkv, state_mamba_conv, state_mamba_ssm, state_gla, state_ret)

    yp = x_prompt.reshape(bp * tp, D_MODEL)
    ys = x_sample.reshape(bs * ts, D_MODEL)
    yp_bf, ys_bf = yp.astype(BF16), ys.astype(BF16)
    mem_bf = mem_prompt.reshape(bp * N_MEM, D_MODEL).astype(BF16)
    outs_p = [[] for _ in range(8)]
    outs_s = [[] for _ in range(6)]
    for i in range(DEPTH):
        lp = {name: val[i] for name, val in small.items()}
        mk = _matmul([mem_bf], xa_wk, F32, 1024, 512, i)
        mv = _matmul([mem_bf], xa_wv, F32, 1024, 512, i)
        yp, yp_bf, st_p, ys, ys_bf, st_s = _layer_pair(
            gp, gp_rw, gs, yp, yp_bf, ys, ys_bf, tabs_p, tabs_s, zeros_p, st_s_in, mk.reshape(bp, N_MEM, D_MODEL),
            mv.reshape(bp, N_MEM, D_MODEL), cache_mem_k, cache_mem_v, lp, big, i, expand)
        for lst, val in zip(outs_p, st_p + (mk.reshape(bp, N_MEM, XA_H, XA_HD), mv.reshape(bp, N_MEM, XA_H, XA_HD))):
            lst.append(val)
        for lst, val in zip(outs_s, st_s):
            lst.append(val)
    return (yp.reshape(bp, tp, D_MODEL), ys.reshape(bs, ts, D_MODEL),
            *[jnp.stack(v) for v in outs_p], *[jnp.stack(v) for v in outs_s])
```

```python
import functools
import math

import jax
import jax.numpy as jnp
from jax import lax
from jax.experimental import pallas as pl
from jax.experimental.pallas import tpu as pltpu

F32 = jnp.float32
BF16 = jnp.bfloat16

D_MODEL = 2048
DEPTH = 2
PAST_LEN = 16384
GROUP = 512
RW_H, RW_HD = 8, 64
RW_IN = 1792
RW_LN_EPS = 64e-5
MB_H, MB_HD, MB_N, MB_G = 8, 64, 128, 2
MB_CONV = 4
MB_CONV_DIM = 1024
GLA_H, GLA_DK, GLA_DV = 4, 64, 128
GLA_QK = 256
GLA_LORA = 16
GLA_TAU = 16.0
GLA_SAFE_EXP = 60.0
RET_H, RET_HD = 4, 128
ROPE_BASE = 10000.0
N_MEM = 256
XA_H, XA_HD = 4, 512
D_FF = 5632
DN_ALPHA = (2 * DEPTH) ** 0.25
RET_LOG_GAMMA = tuple(math.log1p(-(2.0 ** (-5.0 - h))) for h in range(RET_H))

N_PAD = 7168
COL_MB_DT = 1792
COL_GLA_GK = 1920
COL_MB_XBC = 2048
COL_MB_Z = 3072
COL_GLA_V = 3584
COL_GLA_G = 4096
COL_GLA_Q = 4608
COL_GLA_K = 4864
COL_RET = 5120

CHUNK = 128
VMEM_LIMIT = 56 * 1024 * 1024


def _cparams(n_axes, vmem=VMEM_LIMIT):
    return pltpu.CompilerParams(dimension_semantics=("arbitrary",) * n_axes, vmem_limit_bytes=vmem)


def _dot(a, b):
    return jnp.dot(a.astype(BF16), b.astype(BF16), preferred_element_type=F32)


def _dot_nt(a, b):
    return lax.dot_general(a.astype(BF16), b.astype(BF16), (((1,), (1,)), ((), ())), preferred_element_type=F32)


def _dot_tn(a, b):
    return lax.dot_general(a.astype(BF16), b.astype(BF16), (((0,), (0,)), ((), ())), preferred_element_type=F32)


def _sigmoid(x):
    return 1.0 / (1.0 + jnp.exp(-x))


def _silu(x):
    return x * _sigmoid(x)


def _softplus(x):
    return jnp.maximum(x, 0.0) + jnp.log(1.0 + jnp.exp(-jnp.abs(x)))


def _split_dot(m_bf16, x):
    hi = x.astype(BF16)
    r1 = x - hi.astype(F32)
    mid = r1.astype(BF16)
    lo = (r1 - mid.astype(F32)).astype(BF16)
    return (jnp.dot(m_bf16, hi, preferred_element_type=F32)
            + jnp.dot(m_bf16, mid, preferred_element_type=F32)
            + jnp.dot(m_bf16, lo, preferred_element_type=F32))


def _spread(x, sel_bf16):
    hi = x.astype(BF16)
    r1 = x - hi.astype(F32)
    mid = r1.astype(BF16)
    lo = (r1 - mid.astype(F32)).astype(BF16)
    return (jnp.dot(hi, sel_bf16, preferred_element_type=F32)
            + jnp.dot(mid, sel_bf16, preferred_element_type=F32)
            + jnp.dot(lo, sel_bf16, preferred_element_type=F32))


def _head_sums(x, ones_bf16):
    hi = x.astype(BF16)
    lo = (x - hi.astype(F32)).astype(BF16)
    return (jnp.dot(hi, ones_bf16, preferred_element_type=F32)
            + jnp.dot(lo, ones_bf16, preferred_element_type=F32))


def _mo(x, m):
    return x if isinstance(x, int) else pl.multiple_of(x, m)


def _seg_masks(c, seg):
    sh = jnp.int32(int(math.log2(seg)))
    r = lax.broadcasted_iota(jnp.int32, (c, c), 0)
    q = lax.broadcasted_iota(jnp.int32, (c, c), 1)
    same = lax.shift_right_arithmetic(r, sh) == lax.shift_right_arithmetic(q, sh)
    incl = jnp.logical_and(same, r >= q)
    strict = jnp.logical_and(same, r > q)
    return incl, strict, same, r, q


def _mm_kernel(*refs, k_sizes):
    n_x = len(k_sizes)
    x_refs, w_ref, o_ref, wbf = refs[:n_x], refs[n_x], refs[n_x + 1], refs[n_x + 2]

    @pl.when(pl.program_id(1) == 0)
    def _():
        wbf[...] = w_ref[...].astype(BF16)

    acc = None
    off = 0
    for xr, ks in zip(x_refs, k_sizes):
        part = jnp.dot(xr[...].astype(BF16), wbf[off:off + ks, :], preferred_element_type=F32)
        acc = part if acc is None else acc + part
        off += ks
    o_ref[...] = acc.astype(o_ref.dtype)


def _w_spec(w, tn, layer):
    k = w.shape[-2]
    if w.ndim == 2:
        return pl.BlockSpec((k, tn), lambda j, i: (0, j))
    return pl.BlockSpec((None, k, tn), lambda j, i: (layer, 0, j))


def _matmul(xs, w, out_dtype, tm, tn, layer=None):
    m = xs[0].shape[0]
    k, n = w.shape[-2:]
    k_sizes = tuple(x.shape[1] for x in xs)
    assert sum(k_sizes) == k and m % tm == 0 and n % tn == 0
    in_specs = [pl.BlockSpec((tm, ks), lambda j, i: (i, 0)) for ks in k_sizes]
    in_specs.append(_w_spec(w, tn, layer))
    return pl.pallas_call(
        functools.partial(_mm_kernel, k_sizes=k_sizes),
        grid=(n // tn, m // tm),
        in_specs=in_specs,
        out_specs=pl.BlockSpec((tm, tn), lambda j, i: (i, j)),
        out_shape=jax.ShapeDtypeStruct((m, n), out_dtype),
        scratch_shapes=[pltpu.VMEM((k, tn), BF16)],
        compiler_params=_cparams(2),
    )(*xs, w)


def _ffn_gu_kernel(x_ref, wg_ref, wu_ref, o_ref, wg_bf, wu_bf):
    @pl.when(pl.program_id(1) == 0)
    def _():
        wg_bf[...] = wg_ref[...].astype(BF16)
        wu_bf[...] = wu_ref[...].astype(BF16)

    x = x_ref[...]
    gate = jnp.dot(x, wg_bf[...], preferred_element_type=F32)
    up = jnp.dot(x, wu_bf[...], preferred_element_type=F32)
    o_ref[...] = (_silu(gate) * up).astype(o_ref.dtype)


def _ffn_gate_up(x_bf, wg, wu, tm, tn, layer):
    m, k = x_bf.shape
    n = wg.shape[-1]
    return pl.pallas_call(
        _ffn_gu_kernel,
        grid=(n // tn, m // tm),
        in_specs=[pl.BlockSpec((tm, k), lambda j, i: (i, 0)), _w_spec(wg, tn, layer), _w_spec(wu, tn, layer)],
        out_specs=pl.BlockSpec((tm, tn), lambda j, i: (i, j)),
        out_shape=jax.ShapeDtypeStruct((m, n), BF16),
        scratch_shapes=[pltpu.VMEM((k, tn), BF16), pltpu.VMEM((k, tn), BF16)],
        compiler_params=_cparams(2),
    )(x_bf, wg, wu)


def _mm_ln_kernel(*refs, k_sizes, cast_w):
    n_x = len(k_sizes)
    x_refs = refs[:n_x]
    w_ref, r_ref, g_ref, b_ref, of_ref, ob_ref = refs[n_x:n_x + 6]
    if cast_w:
        wbf = refs[n_x + 6]

        @pl.when(pl.program_id(0) == 0)
        def _():
            wbf[...] = w_ref[...].astype(BF16)
    else:
        wbf = w_ref
    acc = None
    off = 0
    for xr, ks in zip(x_refs, k_sizes):
        part = jnp.dot(xr[...].astype(BF16), wbf[off:off + ks, :], preferred_element_type=F32)
        acc = part if acc is None else acc + part
        off += ks
    z = DN_ALPHA * r_ref[...] + acc
    zc = z - jnp.mean(z, axis=-1, keepdims=True)
    var = jnp.mean(zc * zc, axis=-1, keepdims=True)
    out = zc * lax.rsqrt(var + 1e-5) * g_ref[...] + b_ref[...]
    of_ref[...] = out
    ob_ref[...] = out.astype(BF16)


def _matmul_res_ln(xs, w, layer, resid, g, b, tm):
    m, n = resid.shape
    k = w.shape[-2]
    k_sizes = tuple(x.shape[1] for x in xs)
    cast_w = w.dtype != BF16
    assert sum(k_sizes) == k and m % tm == 0 and w.shape[-1] == n
    row = pl.BlockSpec((tm, n), lambda i: (i, 0))
    vec = pl.BlockSpec((1, n), lambda i: (0, 0))
    in_specs = [pl.BlockSpec((tm, ks), lambda i: (i, 0)) for ks in k_sizes]
    in_specs.append(pl.BlockSpec((None, k, n), lambda i: (layer, 0, 0), pipeline_mode=pl.Buffered(1)))
    in_specs += [row, vec, vec]
    return pl.pallas_call(
        functools.partial(_mm_ln_kernel, k_sizes=k_sizes, cast_w=cast_w),
        grid=(m // tm,),
        in_specs=in_specs,
        out_specs=[row, row],
        out_shape=[jax.ShapeDtypeStruct((m, n), F32), jax.ShapeDtypeStruct((m, n), BF16)],
        scratch_shapes=[pltpu.VMEM((k, n), BF16)] if cast_w else [],
        compiler_params=_cparams(1),
    )(*xs, w, resid, g.reshape(1, n), b.reshape(1, n))


def _xattn_kernel(q_ref, k_ref, v_ref, o_ref, *, nb, tq):
    for j in range(nb):
        rows = slice(j * tq, (j + 1) * tq)
        for h in range(XA_H):
            cols = slice(h * XA_HD, (h + 1) * XA_HD)
            s = _dot_nt(q_ref[rows, cols], k_ref[j, :, cols]) * (XA_HD ** -0.5)
            e = jnp.exp(s - jnp.max(s, axis=-1, keepdims=True))
            pr = e / jnp.sum(e, axis=-1, keepdims=True)
            o_ref[rows, cols] = _dot(pr, v_ref[j, :, cols]).astype(o_ref.dtype)


def _xattn_cache_body(q_ref, k_ref, v_ref, o_ref, nb, tq):
    nr = N_MEM * XA_H
    r = lax.broadcasted_iota(jnp.int32, (nr, XA_H * tq), 0)
    q = lax.broadcasted_iota(jnp.int32, (nr, XA_H * tq), 1)
    own = (r & (XA_H - 1)) == lax.shift_right_arithmetic(q, jnp.int32(int(math.log2(tq))))
    for j in range(nb):
        rows = slice(j * tq, (j + 1) * tq)
        kf = k_ref[j].reshape(nr, XA_HD)
        vf = v_ref[j].reshape(nr, XA_HD)
        qcat = jnp.concatenate([q_ref[rows, h * XA_HD:(h + 1) * XA_HD] for h in range(XA_H)], axis=0)
        s = jnp.where(own, _dot_nt(kf, qcat) * (XA_HD ** -0.5), -jnp.inf)
        e = jnp.exp(s - jnp.max(s, axis=0, keepdims=True))
        pr = e / jnp.sum(e, axis=0, keepdims=True)
        o = _dot_tn(pr, vf)
        for h in range(XA_H):
            o_ref[rows, h * XA_HD:(h + 1) * XA_HD] = o[h * tq:(h + 1) * tq, :].astype(o_ref.dtype)


def _ffn_xattn_kernel(x_ref, wg_ref, wu_ref, q_ref, k_ref, v_ref, o_ref, att_ref, wg_bf, wu_bf, *, nb, tq, n_att):
    _ffn_gu_kernel(x_ref, wg_ref, wu_ref, o_ref, wg_bf, wu_bf)
    step = pl.program_id(0) * pl.num_programs(1) + pl.program_id(1)

    @pl.when(step < n_att)
    def _():
        _xattn_cache_body(q_ref, k_ref, v_ref, att_ref, nb, tq)


def _ffn_gate_up_with_cache_attn(x_bf, wg, wu, tm, tn, layer, q, cache_k, cache_v, bsz, t, nb):
    m, k = x_bf.shape
    n = wg.shape[-1]
    n_inner = m // tm
    n_att = bsz // nb
    assert (n // tn) * n_inner >= n_att
    rows = nb * t

    def blk(j, i):
        return jnp.minimum(j * n_inner + i, n_att - 1)

    kv_spec = pl.BlockSpec((None, nb, N_MEM, XA_H, XA_HD), lambda j, i: (layer, blk(j, i), 0, 0, 0))
    q_spec = pl.BlockSpec((rows, D_MODEL), lambda j, i: (blk(j, i), 0))
    return pl.pallas_call(
        functools.partial(_ffn_xattn_kernel, nb=nb, tq=t, n_att=n_att),
        grid=(n // tn, n_inner),
        in_specs=[pl.BlockSpec((tm, k), lambda j, i: (i, 0)), _w_spec(wg, tn, layer), _w_spec(wu, tn, layer),
                  q_spec, kv_spec, kv_spec],
        out_specs=[pl.BlockSpec((tm, tn), lambda j, i: (i, j)), q_spec],
        out_shape=[jax.ShapeDtypeStruct((m, n), BF16), jax.ShapeDtypeStruct((bsz * t, D_MODEL), F32)],
        scratch_shapes=[pltpu.VMEM((k, tn), BF16), pltpu.VMEM((k, tn), BF16)],
        compiler_params=_cparams(2),
    )(x_bf, wg, wu, q, cache_k, cache_v)


def _cross_attn(q, mem_k, mem_v, bsz, t, nb, tq, out_dtype):
    nt = t // tq
    rows = nb * tq
    kv_spec = pl.BlockSpec((nb, N_MEM, D_MODEL), lambda i, j: (i, 0, 0))
    return pl.pallas_call(
        functools.partial(_xattn_kernel, nb=nb, tq=tq),
        grid=(bsz // nb, nt),
        in_specs=[pl.BlockSpec((rows, D_MODEL), lambda i, j: (i * nt + j, 0)), kv_spec, kv_spec],
        out_specs=pl.BlockSpec((rows, D_MODEL), lambda i, j: (i * nt + j, 0)),
        out_shape=jax.ShapeDtypeStruct((bsz * t, D_MODEL), out_dtype),
        compiler_params=_cparams(2),
    )(q, mem_k, mem_v)


def _state_io(carry, h0_ref, hout_ref, hst):
    if carry:
        return (lambda s, h: hst[h]), (lambda s, h, val: hst.__setitem__(h, val))
    return (lambda s, h: h0_ref[s, h]), (lambda s, h, val: hout_ref.__setitem__((s, h), val))


def _for_segments(nseg, fn):
    if nseg == 1:
        fn(0)
    else:
        def body(s, c):
            fn(s)
            return c
        lax.fori_loop(0, nseg, body, 0, unroll=True)


def _ret_kernel(q_ref, k_ref, v_ref, g_ref, cos_ref, sin_ref, gn_ref, h0_ref, o_ref, hout_ref,
                hst, qs_sc, ks_sc, y_sc, *, rows, seg, carry):
    c = CHUNK
    nseg = c // seg
    if carry:
        @pl.when(pl.program_id(1) == 0)
        def _():
            hst[...] = h0_ref[0]
    get_h, set_h = _state_io(carry, h0_ref, hout_ref, hst)
    incl, _, _, r_i, c_i = _seg_masks(c, seg)
    dpos = (r_i - c_i).astype(F32)
    tau = (lax.broadcasted_iota(jnp.int32, (c, RET_HD), 0) & (seg - 1)).astype(F32)
    even = (lax.broadcasted_iota(jnp.int32, (c, GROUP), 1) & 1) == 0

    for ci in range(rows // c):
        rws = slice(ci * c, (ci + 1) * c)
        cosb = jnp.concatenate([cos_ref[rws, :]] * RET_H, axis=1)
        sinb = jnp.concatenate([sin_ref[rws, :]] * RET_H, axis=1)

        def rot(x):
            swapped = jnp.where(even, pltpu.roll(x, GROUP - 1, 1), pltpu.roll(x, 1, 1))
            return x * cosb + swapped * sinb

        qr = rot(q_ref[rws, :])
        kr = rot(k_ref[rws, :]) * (RET_HD ** -0.5)
        for h in range(RET_H):
            lgam = RET_LOG_GAMMA[h]
            cols = slice(h * RET_HD, (h + 1) * RET_HD)
            qh, kh = qr[:, cols], kr[:, cols]
            dm = jnp.where(incl, jnp.exp(dpos * lgam), 0.0)
            y_sc[...] = _dot(_dot_nt(qh, kh) * dm, v_ref[rws, cols])
            qs_sc[...] = qh * jnp.exp((tau + 1.0) * lgam)
            ks_sc[...] = kh * jnp.exp((seg - 1.0 - tau) * lgam)
            cd = math.exp(seg * lgam)

            def seg_step(s, h=h, cols=cols, cd=cd, ci=ci):
                sr = pl.ds(_mo(s * seg, seg), seg)
                vr = pl.ds(_mo(ci * c + s * seg, seg), seg)
                hs = get_h(s, h)
                y_sc[sr, :] += _dot(qs_sc[sr, :], hs)
                set_h(s, h, cd * hs + _dot_tn(ks_sc[sr, :], v_ref[vr, cols]))

            _for_segments(nseg, seg_step)
            y = y_sc[...]
            yn = y * lax.rsqrt(jnp.mean(y * y, axis=-1, keepdims=True) + 1e-5) * gn_ref[:, cols]
            o_ref[rws, cols] = (yn * _silu(g_ref[rws, cols])).astype(o_ref.dtype)

    if carry:
        @pl.when(pl.program_id(1) == pl.num_programs(1) - 1)
        def _():
            hout_ref[0] = hst[...]


def _mamba_kernel(z_ref, xbc_ref, dt_ref, cprev_ref, h0_ref, cw_ref, cb_ref, dtb_ref, alog_ref, dd_ref,
                  ng_ref, e64_ref, e128_ref, o_ref, cout_ref, hout_ref,
                  hst, xpad, act_sc, qs_sc, ks_sc, xdt_sc, xs_sc, yi_sc, yf_sc, etot_sc, *, rows, seg, carry):
    c = CHUNK
    nseg = c // seg
    get_h, set_h = _state_io(carry, h0_ref, hout_ref, hst)

    def conv_act(window, n):
        acc = cb_ref[...] + window[5:5 + n] * cw_ref[0:1, :]
        for i in range(1, MB_CONV):
            acc = acc + window[5 + i:5 + i + n] * cw_ref[i:i + 1, :]
        return _silu(acc)

    if carry:
        @pl.when(pl.program_id(1) == 0)
        def _():
            hst[...] = h0_ref[0]
            xpad[0:8, :] = jnp.zeros((8, MB_CONV_DIM), F32)
            xpad[5:8, :] = cprev_ref[0]

        xpad[8:8 + rows, :] = xbc_ref[...]
        for ci in range(rows // c):
            acc = cb_ref[...] + xpad[ci * c + 5:ci * c + 5 + c, :] * cw_ref[0:1, :]
            for i in range(1, MB_CONV):
                acc = acc + xpad[ci * c + 5 + i:ci * c + 5 + i + c, :] * cw_ref[i:i + 1, :]
            act_sc[ci * c:(ci + 1) * c, :] = _silu(acc)
        xpad[0:8, :] = xpad[rows:rows + 8, :]

        @pl.when(pl.program_id(1) == pl.num_programs(1) - 1)
        def _():
            cout_ref[0] = xpad[5:8, :]
    else:
        def conv_seq(s, carry_):
            sr = pl.ds(pl.multiple_of(s * seg, seg), seg)
            xpad[5:8, :] = cprev_ref[s]
            xs = xbc_ref[sr, :]
            window = jnp.concatenate([xpad[0:8, :], xs], axis=0)
            act_sc[sr, :] = conv_act(window, seg)
            cout_ref[s] = xs[seg - 3:seg]
            return carry_

        xpad[0:8, :] = jnp.zeros((8, MB_CONV_DIM), F32)
        lax.fori_loop(0, rows // seg, conv_seq, 0)

    incl, _, same, _, _ = _seg_masks(c, seg)
    lt_bf = jnp.where(incl, 1.0, 0.0).astype(BF16)
    same_bf = jnp.where(same, 1.0, 0.0).astype(BF16)
    a_neg = -jnp.exp(alog_ref[...])

    for ci in range(rows // c):
        rws = slice(ci * c, (ci + 1) * c)
        dtv = _softplus(dt_ref[rws, :] + dtb_ref[...])
        la = dtv * a_neg
        cum = _split_dot(lt_bf, la)
        tot = _split_dot(same_bf, la)
        cum_t = cum.T
        etot_sc[...] = jnp.exp(tot)
        cum_cols = _spread(cum, e128_ref[...])
        cum_full = _spread(cum, e64_ref[...])
        xh_all = act_sc[rws, 0:GROUP]
        xdt_all = xh_all * _spread(dtv, e64_ref[...])
        xdt_sc[...] = xdt_all
        xs_sc[...] = xdt_all * jnp.exp(_spread(tot, e64_ref[...]) - cum_full)
        for g in range(MB_G):
            cg = act_sc[rws, 768 + g * MB_N:768 + (g + 1) * MB_N]
            bg = act_sc[rws, 512 + g * MB_N:512 + (g + 1) * MB_N]
            gmat = _dot_nt(cg, bg)
            qs_sc[...] = cg
            ks_sc[...] = bg
            for hh in range(MB_H // MB_G):
                h = g * (MB_H // MB_G) + hh
                cols = slice(h * MB_HD, (h + 1) * MB_HD)
                lmat = jnp.exp(jnp.where(incl, cum_cols[:, h * c:(h + 1) * c] - cum_t[h:h + 1, :], -jnp.inf))
                yf_sc[:, cols] = _dot(gmat * lmat, xdt_sc[:, cols])

                def seg_step(s, h=h, cols=cols):
                    sr = pl.ds(_mo(s * seg, seg), seg)
                    first = pl.ds(_mo(s * seg, seg), 1)
                    hs = get_h(s, h)
                    yi_sc[sr, cols] = _dot_nt(qs_sc[sr, :], hs)
                    set_h(s, h, etot_sc[first, h:h + 1] * hs + _dot_tn(xs_sc[sr, cols], ks_sc[sr, :]))

                _for_segments(nseg, seg_step)
        y_all = yf_sc[...] + yi_sc[...] * jnp.exp(cum_full) + dd_ref[...] * xh_all
        yz = y_all * _silu(z_ref[rws, :])
        gw = GROUP // MB_G
        for g in range(MB_G):
            cols = slice(g * gw, (g + 1) * gw)
            part = yz[:, cols]
            nrm = part * lax.rsqrt(jnp.mean(part * part, axis=-1, keepdims=True) + 1e-5) * ng_ref[:, cols]
            o_ref[rws, cols] = nrm.astype(o_ref.dtype)

    if carry:
        @pl.when(pl.program_id(1) == pl.num_programs(1) - 1)
        def _():
            hout_ref[0] = hst[...]


def _gla_kernel(q_ref, k_ref, v_ref, gk_ref, g_ref, h0_ref, gkup_ref, gkb_ref, ng_ref, e_ref,
                o_ref, hout_ref,
                hst, qe_sc, ke_sc, etot_sc, p_sc, y_sc, *, rows, seg, carry):
    c = CHUNK
    nsub = c // seg
    if carry:
        @pl.when(pl.program_id(1) == 0)
        def _():
            for h in range(GLA_H):
                hst[h] = h0_ref[0, h].T
    incl, _, same, _, _ = _seg_masks(c, seg)
    lt_bf = jnp.where(incl, 1.0, 0.0).astype(BF16)
    same_bf = jnp.where(same, 1.0, 0.0).astype(BF16)
    row_i = lax.broadcasted_iota(jnp.int32, (seg, GLA_QK), 0)
    kcs = [slice(h * GLA_DK, (h + 1) * GLA_DK) for h in range(GLA_H)]
    vcs = [slice(h * GLA_DV, (h + 1) * GLA_DV) for h in range(GLA_H)]

    for ci in range(rows // c):
        rws = slice(ci * c, (ci + 1) * c)
        pre = _dot(gk_ref[rws, :], gkup_ref[...]) + gkb_ref[...]
        la = -_softplus(-pre) * (1.0 / GLA_TAU)
        cum = _split_dot(lt_bf, la)
        tot = _split_dot(same_bf, la)
        qv = q_ref[rws, :] * (GLA_DK ** -0.5)
        kv = k_ref[rws, :]
        qe_sc[...] = qv * jnp.exp(cum)
        ke_sc[...] = kv * jnp.exp(tot - cum)
        etot_sc[...] = jnp.exp(tot)

        safe = (jnp.max(-cum) < GLA_SAFE_EXP) if carry else False

        def factored():
            kinv = kv * jnp.exp(-cum)
            for h in range(GLA_H):
                sc = jnp.where(incl, _dot_nt(qe_sc[:, kcs[h]], kinv[:, kcs[h]]), 0.0)
                y_sc[:, vcs[h]] = _dot(sc, v_ref[rws, vcs[h]])

        def pairwise():
            for u in range(nsub):
                sr = slice(u * seg, (u + 1) * seg)
                cu, qu = cum[sr], qv[sr]
                for j in range(seg):
                    r0 = u * seg + j
                    pj = qu * jnp.exp(jnp.where(row_i >= j, cu - cum[r0:r0 + 1], -jnp.inf)) * kv[r0:r0 + 1]
                    p_sc[u, j * seg:(j + 1) * seg, :] = pj
            for u in range(nsub):
                rm = _dot(p_sc[u], e_ref[...])
                y = rm[0:seg, :] * v_ref[ci * c + u * seg:ci * c + u * seg + 1, :]
                for j in range(1, seg):
                    r0 = ci * c + u * seg + j
                    y = y + rm[j * seg:(j + 1) * seg, :] * v_ref[r0:r0 + 1, :]
                y_sc[u * seg:(u + 1) * seg, :] = y

        if carry:
            pl.when(safe)(factored)
            pl.when(jnp.logical_not(safe))(pairwise)
        else:
            pairwise()
        states = [hst[h] for h in range(GLA_H)] if carry else None
        for u in range(nsub):
            sr = slice(u * seg, (u + 1) * seg)
            vr = slice(ci * c + u * seg, ci * c + (u + 1) * seg)
            hts = [states[h] if carry else h0_ref[u, h].T for h in range(GLA_H)]
            for h in range(GLA_H):
                y_sc[sr, vcs[h]] += _dot_nt(qe_sc[sr, kcs[h]], hts[h])
            upds = [_dot_tn(v_ref[vr, vcs[h]], ke_sc[sr, kcs[h]]) for h in range(GLA_H)]
            for h in range(GLA_H):
                new = hts[h] * etot_sc[u * seg:u * seg + 1, kcs[h]] + upds[h]
                if carry:
                    states[h] = new
                else:
                    hout_ref[u, h] = new.T
        if carry:
            for h in range(GLA_H):
                hst[h] = states[h]

        for h in range(GLA_H):
            y = y_sc[:, vcs[h]]
            yn = y * lax.rsqrt(jnp.mean(y * y, axis=-1, keepdims=True) + 1e-5) * ng_ref[:, vcs[h]]
            o_ref[rws, vcs[h]] = (yn * _silu(g_ref[rws, vcs[h]])).astype(o_ref.dtype)

    if carry:
        @pl.when(pl.program_id(1) == pl.num_programs(1) - 1)
        def _():
            for h in range(GLA_H):
                hout_ref[0, h] = hst[h].T


def _rwkv_kernel(p_ref, sp_ref, h0_ref, mu_ref, w0_ref, wup_ref, a0_ref, aup_ref, gup_ref, kk_ref, ka_ref,
                 rk_ref, lng_ref, lnb_ref, ones_ref, o_ref, hout_ref,
                 hst, last_sc, prev_sc, w_sc, y_sc, bt_sc, kt_sc, v_sc, rt_sc, u_sc, rkb_sc, etot_sc, out_sc, kap_sc,
                 pw_sc, t_sc, ak_sc, pb_sc, pk_sc, *, rows, seg, sub, carry):
    c = CHUNK
    assert rows == c
    nsub = c // sub
    n_iter = int(math.log2(sub)) - 1
    npair = RW_H // 2
    pw_ = 2 * RW_HD
    pr_i = lax.broadcasted_iota(jnp.int32, (pw_, pw_), 0)
    pc_i = lax.broadcasted_iota(jnp.int32, (pw_, pw_), 1)
    bd_mask = (pr_i >= RW_HD) == (pc_i >= RW_HD)
    low_rows = lax.broadcasted_iota(jnp.int32, (pw_, RW_HD), 0) < RW_HD

    def to_bd(stack):
        return jnp.where(bd_mask, jnp.concatenate([stack, stack], axis=1), 0.0)

    def from_bd(bd):
        return jnp.where(low_rows, bd[:, 0:RW_HD], bd[:, RW_HD:pw_])

    if carry:
        @pl.when(pl.program_id(1) == 0)
        def _():
            for pi in range(npair):
                hst[pi] = to_bd(h0_ref[0, pi])
            last_sc[...] = jnp.broadcast_to(sp_ref[0], (8, RW_IN))

        prev_row = jnp.broadcast_to(last_sc[0:1, :], (c, RW_IN))
    else:
        for s in range(rows // seg):
            prev_sc[s * seg:(s + 1) * seg, :] = jnp.broadcast_to(sp_ref[s], (seg, RW_IN))
        prev_row = prev_sc[...]
    incl, strict, same, r_i, c_i = _seg_masks(c, sub)
    lt_bf = jnp.where(incl, 1.0, 0.0).astype(BF16)
    same_bf = jnp.where(same, 1.0, 0.0).astype(BF16)
    eye = jnp.where(r_i == c_i, 1.0, 0.0)
    row_w = lax.broadcasted_iota(jnp.int32, (c, RW_IN), 0)
    first_row = (row_w & ((c if carry else seg) - 1)) == 0

    p = p_ref[...]
    prev = jnp.where(first_row, prev_row, pltpu.roll(p, 1, 0))
    xs = p + (prev - p) * mu_ref[...]
    r = xs[:, 0:GROUP]
    k = xs[:, GROUP:2 * GROUP]
    v = xs[:, 2 * GROUP:3 * GROUP]
    wa = xs[:, 3 * GROUP:3 * GROUP + 128]
    gd = xs[:, 3 * GROUP + 128:RW_IN]
    w = -_softplus(-(w0_ref[...] + _dot(jnp.tanh(wa), wup_ref[...]))) - 0.5
    ld = -jnp.exp(w)
    a = _sigmoid(a0_ref[...] + _dot(wa, aup_ref[...]))
    gate = _dot(_sigmoid(gd), gup_ref[...])
    kkr = k * kk_ref[...]
    k2 = k * (1.0 + (a - 1.0) * ka_ref[...])
    cum = _split_dot(lt_bf, ld)
    tot = _split_dot(same_bf, ld)
    ecum = jnp.exp(cum)
    einv = jnp.exp(-cum)
    eprev = jnp.exp(cum - ld)
    rkb_sc[...] = r * k2 * rk_ref[...]
    etot_sc[...] = jnp.exp(tot)
    v_sc[...] = v
    rtil = r * ecum
    rt_sc[...] = rtil
    kt_sc[...] = k2 * einv
    rn = lax.rsqrt(jnp.maximum(_head_sums(kkr * kkr, ones_ref[...]), 1e-24))
    braw = kkr * a * einv * rn
    kraw = kkr * eprev * rn

    heads = [slice(h * RW_HD, (h + 1) * RW_HD) for h in range(RW_H)]
    for h, cols in enumerate(heads):
        kap = kraw[:, cols]
        bt = braw[:, cols]
        kap_sc[:, cols] = kap
        bt_sc[:, cols] = bt
        m1 = _dot_nt(jnp.concatenate([kap, rtil[:, cols]], axis=0), jnp.concatenate([bt, kt_sc[:, cols]], axis=0))
        x = jnp.where(strict, -m1[0:c, 0:c], 0.0)
        pw_sc[0, h] = x
        t_sc[h] = eye + x
        ak_sc[h] = jnp.where(strict, m1[0:c, c:2 * c], 0.0)
        pb_sc[h] = jnp.where(incl, m1[c:2 * c, 0:c], 0.0)
        pk_sc[h] = jnp.where(incl, m1[c:2 * c, c:2 * c], 0.0)
    for h, cols in enumerate(heads):
        y_sc[:, cols] = _dot(ak_sc[h], v_sc[:, cols])
    for it in range(n_iter):
        src, dst = it % 2, (it + 1) % 2
        for h in range(RW_H):
            pw = pw_sc[src, h]
            pw_sc[dst, h] = _dot(pw, pw)
        for h in range(RW_H):
            tm = t_sc[h]
            t_sc[h] = tm + _dot(tm, pw_sc[dst, h])
    for h, cols in enumerate(heads):
        wy = _dot(t_sc[h], jnp.concatenate([kap_sc[:, cols], y_sc[:, cols]], axis=1))
        w_sc[:, cols] = wy[:, 0:RW_HD]
        y_sc[:, cols] = wy[:, RW_HD:2 * RW_HD]

    states = [hst[pi] for pi in range(npair)] if carry else None
    units = [(s, pi) for s in range(nsub) for pi in range(npair)]
    group = npair if carry else 2 * npair
    for g0 in range(0, len(units), group):
        grp_units = units[g0:g0 + group]
        sts, wrs, uus, upds = [], [], [], []
        for s, pi in grp_units:
            sts.append(states[pi] if carry else to_bd(h0_ref[s, pi]))
        for (s, pi), st in zip(grp_units, sts):
            sr, pc = slice(s * sub, (s + 1) * sub), slice(pi * pw_, (pi + 1) * pw_)
            wrs.append(_dot_nt(jnp.concatenate([w_sc[sr, pc], rt_sc[sr, pc]], axis=0), st))
        for (s, pi), wr in zip(grp_units, wrs):
            sr, pc = slice(s * sub, (s + 1) * sub), slice(pi * pw_, (pi + 1) * pw_)
            uu = -wr[0:sub, :] - y_sc[sr, pc]
            u_sc[sr, pc] = uu
            out_sc[sr, pc] = wr[sub:2 * sub, :]
            uus.append(uu)
        for (s, pi), uu in zip(grp_units, uus):
            sr, pc = slice(s * sub, (s + 1) * sub), slice(pi * pw_, (pi + 1) * pw_)
            upds.append(_dot_tn(jnp.concatenate([uu, v_sc[sr, pc]], axis=0),
                                jnp.concatenate([bt_sc[sr, pc], kt_sc[sr, pc]], axis=0)))
        for (s, pi), st, upd in zip(grp_units, sts, upds):
            pc = slice(pi * pw_, (pi + 1) * pw_)
            new = (st + jnp.where(bd_mask, upd, 0.0)) * etot_sc[s * sub:s * sub + 1, pc]
            if carry:
                states[pi] = new
            else:
                hout_ref[s, pi] = from_bd(new)
    if carry:
        for pi in range(npair):
            hst[pi] = states[pi]

    for h, cols in enumerate(heads):
        out_sc[:, cols] += _dot(jnp.concatenate([pb_sc[h], pk_sc[h]], axis=1),
                                jnp.concatenate([u_sc[:, cols], v_sc[:, cols]], axis=0))
    ones = ones_ref[...]
    o_all = out_sc[...]
    oc = o_all - _head_sums(o_all, ones) * (1.0 / RW_HD)
    var = _head_sums(oc * oc, ones) * (1.0 / RW_HD)
    on = oc * lax.rsqrt(var + RW_LN_EPS) * lng_ref[...] + lnb_ref[...]
    out_sc[...] = on + _head_sums(rkb_sc[...], ones) * v_sc[...]
    o_ref[...] = (out_sc[...] * gate).astype(o_ref.dtype)

    if carry:
        last_sc[...] = jnp.broadcast_to(p_ref[rows - 1:rows, :], (8, RW_IN))

        @pl.when(pl.program_id(1) == pl.num_programs(1) - 1)
        def _():
            for pi in range(npair):
                hout_ref[0, pi] = from_bd(hst[pi])


class _Group:
    def __init__(self, bsz, t, rows):
        self.bsz, self.t, self.rows = bsz, t, rows
        self.carry = t >= CHUNK
        self.nb = 1 if self.carry else rows // t
        self.nt = t // rows if self.carry else 1
        self.grid = (bsz // self.nb, self.nt)
        self.seg = CHUNK if self.carry else t

    def rows_spec(self, width, col_block):
        nt = self.nt
        return pl.BlockSpec((self.rows, width), lambda i, j: (i * nt + j, col_block))

    def state_spec(self, shape, layer=None):
        zeros = (0,) * len(shape)
        if layer is None:
            return pl.BlockSpec((self.nb,) + tuple(shape), lambda i, j: (i,) + zeros)
        return pl.BlockSpec((None, self.nb) + tuple(shape), lambda i, j: (layer, i) + zeros)

    def out_rows(self, width, dtype=BF16):
        nt = self.nt
        return (pl.BlockSpec((self.rows, width), lambda i, j: (i * nt + j, 0)),
                jax.ShapeDtypeStruct((self.bsz * self.t, width), dtype))


def _vec_spec(shape):
    zeros = (0,) * len(shape)
    return pl.BlockSpec(tuple(shape), lambda i, j: zeros)


def _retention(grp, p, cos_t, sin_t, h0, sl, gn):
    c = CHUNK
    shape = (RET_H, RET_HD, RET_HD)
    in_specs = [grp.rows_spec(GROUP, COL_RET // GROUP + n) for n in range(4)]
    tab = pl.BlockSpec((grp.rows, RET_HD), (lambda i, j: (j, 0)) if grp.carry else (lambda i, j: (0, 0)))
    in_specs += [tab, tab, _vec_spec((1, GROUP)), grp.state_spec(shape, sl)]
    o_spec, o_shape = grp.out_rows(GROUP)
    return pl.pallas_call(
        functools.partial(_ret_kernel, rows=grp.rows, seg=grp.seg, carry=grp.carry),
        grid=grp.grid,
        in_specs=in_specs,
        out_specs=[o_spec, grp.state_spec(shape)],
        out_shape=[o_shape, jax.ShapeDtypeStruct(h0.shape[1:], F32)],
        scratch_shapes=[pltpu.VMEM(shape, F32), pltpu.VMEM((c, RET_HD), F32),
                        pltpu.VMEM((c, RET_HD), F32), pltpu.VMEM((c, RET_HD), F32)],
        compiler_params=_cparams(2),
    )(p, p, p, p, cos_t, sin_t, gn.reshape(1, GROUP), h0)


def _mamba(grp, p, cprev, h0, sl, lp):
    c = CHUNK
    cshape, hshape = (MB_CONV - 1, MB_CONV_DIM), (MB_H, MB_HD, MB_N)
    h0 = jnp.swapaxes(h0, -1, -2)
    pad8 = lambda a: jnp.pad(a.reshape(1, MB_H), ((0, 0), (0, 128 - MB_H)))
    head = jnp.arange(128)[:, None]
    e64 = (head == jnp.arange(GROUP)[None, :] // MB_HD).astype(BF16)
    e128 = (head == jnp.arange(MB_H * CHUNK)[None, :] // CHUNK).astype(BF16)
    in_specs = [grp.rows_spec(GROUP, COL_MB_Z // GROUP), grp.rows_spec(MB_CONV_DIM, COL_MB_XBC // MB_CONV_DIM),
                grp.rows_spec(128, COL_MB_DT // 128),
                grp.state_spec(cshape, sl), grp.state_spec(hshape, sl),
                _vec_spec((MB_CONV, MB_CONV_DIM)), _vec_spec((1, MB_CONV_DIM)), _vec_spec((1, 128)),
                _vec_spec((1, 128)), _vec_spec((1, GROUP)), _vec_spec((1, GROUP)),
                _vec_spec((128, GROUP)), _vec_spec((128, MB_H * CHUNK))]
    o_spec, o_shape = grp.out_rows(GROUP)
    return pl.pallas_call(
        functools.partial(_mamba_kernel, rows=grp.rows, seg=grp.seg, carry=grp.carry),
        grid=grp.grid,
        in_specs=in_specs,
        out_specs=[o_spec, grp.state_spec(cshape), grp.state_spec(hshape)],
        out_shape=[o_shape, jax.ShapeDtypeStruct(cprev.shape[1:], F32), jax.ShapeDtypeStruct(h0.shape[1:], F32)],
        scratch_shapes=[pltpu.VMEM(hshape, F32),
                        pltpu.VMEM(((grp.rows if grp.carry else 0) + 8, MB_CONV_DIM), F32),
                        pltpu.VMEM((grp.rows, MB_CONV_DIM), F32),
                        pltpu.VMEM((c, MB_N), F32), pltpu.VMEM((c, MB_N), F32), pltpu.VMEM((c, GROUP), F32),
                        pltpu.VMEM((c, GROUP), F32), pltpu.VMEM((c, GROUP), F32), pltpu.VMEM((c, GROUP), F32),
                        pltpu.VMEM((c, 128), F32)],
        compiler_params=_cparams(2),
    )(p, p, p, cprev, h0, lp['mb_conv_w'], lp['mb_conv_b'].reshape(1, MB_CONV_DIM), pad8(lp['mb_dt_bias']),
      pad8(lp['mb_a_log']), jnp.repeat(lp['mb_d'], MB_HD).reshape(1, GROUP), lp['mb_norm_g'].reshape(1, GROUP),
      e64, e128)


def _gla(grp, p, h0, sl, lp, expand):
    c = CHUNK
    shape = (GLA_H, GLA_DK, GLA_DV)
    sub = 64 if grp.carry else grp.seg
    gk_up = jnp.pad(lp['gla_gk_up'], ((0, 128 - GLA_LORA), (0, 0)))
    in_specs = [grp.rows_spec(GLA_QK, COL_GLA_Q // GLA_QK), grp.rows_spec(GLA_QK, COL_GLA_K // GLA_QK),
                grp.rows_spec(GROUP, COL_GLA_V // GROUP), grp.rows_spec(128, COL_GLA_GK // 128),
                grp.rows_spec(GROUP, COL_GLA_G // GROUP), grp.state_spec(shape, sl),
                _vec_spec((128, GLA_QK)), _vec_spec((1, GLA_QK)), _vec_spec((1, GROUP)), _vec_spec((GLA_QK, GROUP))]
    o_spec, o_shape = grp.out_rows(GROUP)
    return pl.pallas_call(
        functools.partial(_gla_kernel, rows=grp.rows, seg=sub, carry=grp.carry),
        grid=grp.grid,
        in_specs=in_specs,
        out_specs=[o_spec, grp.state_spec(shape)],
        out_shape=[o_shape, jax.ShapeDtypeStruct(h0.shape[1:], F32)],
        scratch_shapes=[pltpu.VMEM((GLA_H, GLA_DV, GLA_DK), F32)]
        + [pltpu.VMEM((c, GLA_QK), F32) for _ in range(3)]
        + [pltpu.VMEM((c // sub, sub * sub, GLA_QK), F32), pltpu.VMEM((c, GROUP), F32)],
        compiler_params=_cparams(2),
    )(p, p, p, p, p, h0, gk_up, lp['gla_gk_b'].reshape(1, GLA_QK), lp['gla_norm_g'].reshape(1, GROUP), expand)


def _rwkv(grp, p, shift_prev, h0, sl, lp):
    c = CHUNK
    head_ones = (jnp.arange(GROUP)[:, None] // RW_HD == jnp.arange(GROUP)[None, :] // RW_HD).astype(BF16)
    shape = (RW_H // 2, 2 * RW_HD, RW_HD)
    h0 = h0.reshape(h0.shape[:2] + shape)
    sub = 64 if grp.carry else grp.seg
    row = lambda a: a.reshape(1, -1)
    w_up = jnp.pad(lp['rw_w_up'], ((0, 64), (0, 0)))
    a_up = jnp.pad(lp['rw_a_up'], ((64, 0), (0, 0)))
    in_specs = [grp.rows_spec(RW_IN, 0), grp.state_spec((1, RW_IN), sl), grp.state_spec(shape, sl),
                _vec_spec((1, RW_IN)), _vec_spec((1, GROUP)), _vec_spec((128, GROUP)), _vec_spec((1, GROUP)),
                _vec_spec((128, GROUP)), _vec_spec((128, GROUP))] + [_vec_spec((1, GROUP))] * 5 + [_vec_spec((GROUP, GROUP))]
    o_spec, o_shape = grp.out_rows(GROUP)
    wide = lambda: pltpu.VMEM((c, GROUP), F32)
    return pl.pallas_call(
        functools.partial(_rwkv_kernel, rows=grp.rows, seg=grp.seg, sub=sub, carry=grp.carry),
        grid=grp.grid,
        in_specs=in_specs,
        out_specs=[o_spec, grp.state_spec(shape)],
        out_shape=[o_shape, jax.ShapeDtypeStruct(h0.shape[1:], F32)],
        scratch_shapes=[pltpu.VMEM((RW_H // 2, 2 * RW_HD, 2 * RW_HD), F32), pltpu.VMEM((8, RW_IN), F32),
                        pltpu.VMEM((c, RW_IN), F32)] + [wide() for _ in range(11)]
        + [pltpu.VMEM((2, RW_H, c, c), F32)] + [pltpu.VMEM((RW_H, c, c), F32) for _ in range(4)],
        compiler_params=_cparams(2),
    )(p, shift_prev.reshape(shift_prev.shape[0], -1, 1, RW_IN), h0, row(lp['rw_mu']), row(lp['rw_w0']), w_up,
      row(lp['rw_a0']), a_up, lp['rw_g_up'], row(lp['rw_k_k']), row(lp['rw_k_a']), row(lp['rw_r_k']),
      row(lp['rw_ln_g']), row(lp['rw_ln_b']), head_ones)


def _rope_tables(pos0, t):
    half = RET_HD // 2
    inv = 1.0 / (ROPE_BASE ** jnp.linspace(0.0, 1.0, half, dtype=F32))
    pos = pos0 + jnp.arange(t, dtype=F32)
    ang = pos[:, None] * inv[None, :]
    cos, sin = jnp.cos(ang), jnp.sin(ang)
    cos_t = jnp.stack([cos, cos], axis=-1).reshape(t, RET_HD)
    sin_t = jnp.stack([-sin, sin], axis=-1).reshape(t, RET_HD)
    return cos_t, sin_t


def _pad_w_in(w):
    z = lambda n: jnp.zeros(w.shape[:-1] + (n,), w.dtype)
    return jnp.concatenate([
        w[..., 0:1792], w[..., 3328:3336], z(120), w[..., 4360:4376], z(112), w[..., 2304:3328],
        w[..., 1792:2304], w[..., 3848:4360], w[..., 4376:4888], w[..., 3336:3592], w[..., 3592:3848],
        w[..., 4888:6936]], axis=-1)


def _mix_block(grp, grp_rw, x, x_bf, pos_tabs, states, sl, lp, big, wl, expand, tm):
    rw_shift, rw_state, mb_conv, mb_state, gla_state, ret_state = states
    bsz, t = grp.bsz, grp.t
    p = _matmul([x_bf], big['w_in'], F32, tm, 1024, wl)
    o_rw, rw_new = _rwkv(grp_rw, p, rw_shift, rw_state, sl, lp)
    rw_new = rw_new.reshape(bsz, RW_H, RW_HD, RW_HD)
    o_mb, conv_new, mb_new = _mamba(grp, p, mb_conv, mb_state, sl, lp)
    mb_new = jnp.swapaxes(mb_new, -1, -2)
    o_gl, gla_new = _gla(grp, p, gla_state, sl, lp, expand)
    o_rt, ret_new = _retention(grp, p, pos_tabs[0], pos_tabs[1], ret_state, sl, lp['ret_norm_g'])
    shift_new = p.reshape(bsz, t, N_PAD)[:, t - 1, 0:RW_IN]
    x, x_bf = _matmul_res_ln([o_rw, o_mb, o_gl, o_rt], big['w_out'], wl, x, lp['ln1_g'], lp['ln1_b'],
                             512 if grp.carry else 256)
    q = _matmul([x_bf], big['xa_wq'], BF16 if grp.carry else F32, tm, 1024, wl)
    return x, q, (shift_new, rw_new, conv_new, mb_new, gla_new, ret_new)


def _layer_pair(gp, gp_rw, gs, xp, xp_bf, xs, xs_bf, tabs_p, tabs_s, zeros_p, st_s_in, mk, mv, cache_k, cache_v,
                lp, big, wl, expand):
    xp, qp, st_p = _mix_block(gp, gp_rw, xp, xp_bf, tabs_p, zeros_p, 0, lp, big, wl, expand, 1024)
    att_p = _cross_attn(qp, mk, mv, gp.bsz, gp.t, 1, 1024, BF16)
    xp, xp_bf = _matmul_res_ln([att_p], big['xa_wo'], wl, xp, lp['ln2_g'], lp['ln2_b'], 512)
    xs, qs, st_s = _mix_block(gs, gs, xs, xs_bf, tabs_s, st_s_in, wl, lp, big, wl, expand, 1024)
    hid_p, att_s = _ffn_gate_up_with_cache_attn(xp_bf, big['ffn_w_gate'], big['ffn_w_up'], 1024, 512, wl,
                                                qs, cache_k, cache_v, gs.bsz, gs.t, 2)
    xp, xp_bf = _matmul_res_ln([hid_p], big['ffn_w_down'], wl, xp, lp['ln3_g'], lp['ln3_b'], 256)
    xs, xs_bf = _matmul_res_ln([att_s], big['xa_wo'], wl, xs, lp['ln2_g'], lp['ln2_b'], 256)
    hid_s = _ffn_gate_up(xs_bf, big['ffn_w_gate'], big['ffn_w_up'], 1024, 512, wl)
    xs, xs_bf = _matmul_res_ln([hid_s], big['ffn_w_down'], wl, xs, lp['ln3_g'], lp['ln3_b'], 256)
    return xp, xp_bf, st_p, xs, xs_bf, st_s


def kernel(x_prompt, x_sample, state_rwkv_shift, state_rwkv_wkv, state_mamba_conv, state_mamba_ssm, state_gla,
           state_ret, cache_mem_k, cache_mem_v, mem_prompt, w_in, w_out, ln1_g, ln1_b, rw_mu, rw_w0, rw_w_up,
           rw_a0, rw_a_up, rw_g_up, rw_k_k, rw_k_a, rw_r_k, rw_ln_g, rw_ln_b, mb_conv_w, mb_conv_b, mb_dt_bias,
           mb_a_log, mb_d, mb_norm_g, gla_gk_up, gla_gk_b, gla_norm_g, ret_norm_g, ln2_g, ln2_b, xa_wq, xa_wk,
           xa_wv, xa_wo, ln3_g, ln3_b, ffn_w_gate, ffn_w_up, ffn_w_down):
    small = dict(
        ln1_g=ln1_g, ln1_b=ln1_b, rw_mu=rw_mu, rw_w0=rw_w0, rw_w_up=rw_w_up, rw_a0=rw_a0,
        rw_a_up=rw_a_up, rw_g_up=rw_g_up, rw_k_k=rw_k_k, rw_k_a=rw_k_a, rw_r_k=rw_r_k, rw_ln_g=rw_ln_g,
        rw_ln_b=rw_ln_b, mb_conv_w=mb_conv_w, mb_conv_b=mb_conv_b, mb_dt_bias=mb_dt_bias, mb_a_log=mb_a_log,
        mb_d=mb_d, mb_norm_g=mb_norm_g, gla_gk_up=gla_gk_up, gla_gk_b=gla_gk_b, gla_norm_g=gla_norm_g,
        ret_norm_g=ret_norm_g, ln2_g=ln2_g, ln2_b=ln2_b, ln3_g=ln3_g, ln3_b=ln3_b)
    big = dict(w_in=_pad_w_in(w_in), w_out=w_out, xa_wq=xa_wq, xa_wo=xa_wo, ffn_w_gate=ffn_w_gate,
               ffn_w_up=ffn_w_up, ffn_w_down=ffn_w_down.astype(BF16))
    bp, tp, _ = x_prompt.shape
    bs, ts, _ = x_sample.shape
    gp = _Group(bp, tp, 512)
    gp_rw = _Group(bp, tp, CHUNK)
    gs = _Group(bs, ts, CHUNK)
    expand = (jnp.arange(GLA_QK)[:, None] // GLA_DK == jnp.arange(GROUP)[None, :] // GLA_DV).astype(BF16)
    tabs_p = _rope_tables(0.0, tp)
    tabs_s = tuple(jnp.tile(tb, (gs.nb, 1)) for tb in _rope_tables(float(PAST_LEN), ts))
    zeros_p = (jnp.zeros((1, bp, RW_IN), F32), jnp.zeros((1, bp, RW_H, RW_HD, RW_HD), F32),
               jnp.zeros((1, bp, MB_CONV - 1, MB_CONV_DIM), F32), jnp.zeros((1, bp, MB_H, MB_N, MB_HD), F32),
               jnp.zeros((1, bp, GLA_H, GLA_DK, GLA_DV), F32), jnp.zeros((1, bp, RET_H, RET_HD, RET_HD), F32))
    st_s_in = (state_rwkv_shift, state_rwkv_wkv, state_mamba_conv, state_mamba_ssm, state_gla, state_ret)

    yp = x_prompt.reshape(bp * tp, D_MODEL)
    ys = x_sample.reshape(bs * ts, D_MODEL)
    yp_bf, ys_bf = yp, ys
    mem_bf = mem_prompt.reshape(bp * N_MEM, D_MODEL).astype(BF16)
    outs_p = [[] for _ in range(8)]
    outs_s = [[] for _ in range(6)]
    for i in range(DEPTH):
        lp = {name: val[i] for name, val in small.items()}
        mk = _matmul([mem_bf], xa_wk, F32, 1024, 512, i)
        mv = _matmul([mem_bf], xa_wv, F32, 1024, 512, i)
        yp, yp_bf, st_p, ys, ys_bf, st_s = _layer_pair(
            gp, gp_rw, gs, yp, yp_bf, ys, ys_bf, tabs_p, tabs_s, zeros_p, st_s_in, mk.reshape(bp, N_MEM, D_MODEL),
            mv.reshape(bp, N_MEM, D_MODEL), cache_mem_k, cache_mem_v, lp, big, i, expand)
        for lst, val in zip(outs_p, st_p + (mk.reshape(bp, N_MEM, XA_H, XA_HD), mv.reshape(bp, N_MEM, XA_H, XA_HD))):
            lst.append(val)
        for lst, val in zip(outs_s, st_s):
            lst.append(val)
    return (yp.reshape(bp, tp, D_MODEL), ys.reshape(bs, ts, D_MODEL),
            *[jnp.stack(v) for v in outs_p], *[jnp.stack(v) for v in outs_s])
```
